```python
import math
import jax
import jax.numpy as jnp
from jax import lax
import numpy as np

D_MODEL = 1024
BATCH = 16
SEQ = 256
DEPTH = 2
DEC_BATCH = 2
DEC_SEQ = 2048
PAST_LEN = 512

GRID_W = 64
EPS = 1e-6
ATTN_BLOCK = 128
ROPE_BASE = 10000.0

A_HEADS = 16
A_NOPE = 64
A_ROPE = 32
A_QK = A_NOPE + A_ROPE
A_V = 64
A_QLORA = 512
A_KVLORA = 256

B_HEADS = 4
B_DK = 128
B_DV = 256
B_GATE_RANK = 16
B_GATE_NORM = 16.0
B_CHUNK = 16

C_HEADS = 16
C_HEADDIM = 64
C_INNER = C_HEADS * C_HEADDIM
C_GROUPS = 2
C_STATE = 128
C_XBC = C_INNER + 2 * C_GROUPS * C_STATE
C_CONV = 5
C_CHUNK = 64

N_EXPERTS = 16
N_EXPERT_GROUPS = 4
TOP_K = 2
D_EXPERT = 512

IN_SPLITS = (
    ('a_cq', A_QLORA), ('a_ckv', A_KVLORA), ('a_krope', A_ROPE),
    ('b_q', B_HEADS * B_DK), ('b_k', B_HEADS * B_DK), ('b_v', B_HEADS * B_DV),
    ('b_og', B_HEADS * B_DV), ('b_gf', B_GATE_RANK), ('b_gb', B_GATE_RANK),
    ('c_z', C_INNER), ('c_xbc', C_XBC), ('c_dtf', C_HEADS), ('c_dtb', C_HEADS),
    ('gates', 3 * D_MODEL),
)
D_IN = (A_QLORA + A_KVLORA + A_ROPE + 2 * B_HEADS * B_DK + 2 * B_HEADS * B_DV
        + 2 * B_GATE_RANK + C_INNER + C_XBC + 2 * C_HEADS + 3 * D_MODEL)

kernel_name = 'mla_gla_ssd_moe_prefix_diffusion'


def rmsnorm(x, w):
    xf = x.astype(jnp.float32)
    y = xf * lax.rsqrt(jnp.mean(xf * xf, axis=-1, keepdims=True) + EPS)
    return (y * w.astype(jnp.float32)).astype(x.dtype)


def ada_rmsnorm(x, w, shift, scale):
    return rmsnorm(x, w) * (1.0 + scale) + shift


def modulation(cvec, w, b):
    m = jax.nn.silu(cvec) @ w + b
    return m.reshape(m.shape[:-1] + (6, D_MODEL))


def split_proj(h, w):
    u = h @ w
    parts = {}
    start = 0
    for name, size in IN_SPLITS:
        parts[name] = u[..., start:start + size]
        start += size
    return parts


def flip(t):
    return jnp.flip(t, axis=1)


def axial_rope_angles(rows):
    t = jnp.arange(rows * GRID_W)
    row = (t // GRID_W).astype(jnp.float32)
    col = (t % GRID_W).astype(jnp.float32)
    n_freq = A_ROPE // 4
    inv_freq = 1.0 / (ROPE_BASE ** (jnp.arange(n_freq, dtype=jnp.float32) / n_freq))
    ang = jnp.concatenate([row[:, None] * inv_freq, col[:, None] * inv_freq], axis=-1)
    return jnp.cos(ang), jnp.sin(ang)


def rope_tail(t, cos, sin):
    head, tail = t[..., :A_NOPE], t[..., A_NOPE:]
    pairs = tail.astype(jnp.float32).reshape(tail.shape[:-1] + (A_ROPE // 2, 2))
    x1, x2 = pairs[..., 0], pairs[..., 1]
    cs, sn = cos[None, :, None, :], sin[None, :, None, :]
    rot = jnp.stack([x1 * cs - x2 * sn, x1 * sn + x2 * cs], axis=-1).reshape(tail.shape)
    return jnp.concatenate([head, rot.astype(t.dtype)], axis=-1)


def mla_queries(cq, P, l):
    q = rmsnorm(cq, P['a_q_norm'][l]) @ P['a_wq'][l]
    q = q.reshape(q.shape[:2] + (A_HEADS, A_QK))
    return rmsnorm(q, P['a_qk_qnorm'][l])


def mla_keys_values(ckv, krope, P, l):
    kv = (ckv @ P['a_wkv'][l]).reshape(ckv.shape[:2] + (A_HEADS, A_NOPE + A_V))
    k_nope, v = kv[..., :A_NOPE], kv[..., A_NOPE:]
    k_rope = jnp.broadcast_to(krope[:, :, None, :], k_nope.shape[:3] + (A_ROPE,))
    k = rmsnorm(jnp.concatenate([k_nope, k_rope], axis=-1), P['a_qk_knorm'][l])
    return k, v


def attend(q, k, v):
    Bsz, Lq, H, dk = q.shape
    nb = Lq // ATTN_BLOCK
    scale = dk ** -0.5
    qb = jnp.moveaxis(q.reshape(Bsz, nb, ATTN_BLOCK, H, dk), 1, 0)

    def one_block(qblk):
        s = jnp.einsum('bqhd,bkhd->bhqk', qblk, k).astype(jnp.float32) * scale
        p = jax.nn.softmax(s, axis=-1).astype(v.dtype)
        return jnp.einsum('bhqk,bkhd->bqhd', p, v)

    o = lax.map(one_block, qb)
    return jnp.moveaxis(o, 0, 1).reshape(Bsz, Lq, H, v.shape[-1])


def gla_log_decay(low, w, b):
    return jax.nn.log_sigmoid((low @ w + b).astype(jnp.float32)) / B_GATE_NORM


def gla_chunked(q, k, v, logg, s0):
    Bsz, L, H, DK = q.shape
    DV = v.shape[-1]
    n, C = L // B_CHUNK, B_CHUNK
    qc = q.astype(jnp.float32).reshape(Bsz, n, C, H, DK) * (DK ** -0.5)
    kc = k.astype(jnp.float32).reshape(Bsz, n, C, H, DK)
    vc = v.astype(jnp.float32).reshape(Bsz, n, C, H, DV)
    bcum = jnp.cumsum(logg.reshape(Bsz, n, C, H, DK), axis=2)
    causal = jnp.tril(jnp.ones((C, C), dtype=bool))[:, :, None, None]
    rel = jnp.exp(jnp.where(causal, bcum[:, :, :, None] - bcum[:, :, None, :], -jnp.inf))
    attn = jnp.einsum('bnihd,bnjhd,bnijhd->bnhij', qc, kc, rel)
    o_intra = jnp.einsum('bnhij,bnjhv->bnihv', attn, vc)
    btot = bcum[:, :, -1]
    ds = jnp.einsum('bnjhd,bnjhv->bnhdv', kc * jnp.exp(btot[:, :, None] - bcum), vc)

    def step(s, inp):
        dec, d = inp
        return dec[..., None] * s + d, s

    s_fin, s_prev = lax.scan(step, s0.astype(jnp.float32),
                             (jnp.moveaxis(jnp.exp(btot), 1, 0), jnp.moveaxis(ds, 1, 0)))
    o_inter = jnp.einsum('bnihd,bnhdv->bnihv', qc * jnp.exp(bcum), jnp.moveaxis(s_prev, 0, 1))
    o = (o_intra + o_inter).reshape(Bsz, L, H, DV)
    return o.astype(v.dtype), s_fin


def ssd_chunked(x, dt, A, Bm, Cm, s0):
    Bsz, L, H, P = x.shape
    G, N = Bm.shape[2], Bm.shape[3]
    R = H // G
    n, C = L // C_CHUNK, C_CHUNK
    xc = x.astype(jnp.float32).reshape(Bsz, n, C, G, R, P)
    dtc = dt.reshape(Bsz, n, C, G, R)
    Bc = Bm.astype(jnp.float32).reshape(Bsz, n, C, G, N)
    Cc = Cm.astype(jnp.float32).reshape(Bsz, n, C, G, N)
    acum = jnp.cumsum(dtc * A.reshape(G, R), axis=2)
    causal = jnp.tril(jnp.ones((C, C), dtype=bool))[:, :, None, None]
    Lm = jnp.exp(jnp.where(causal, acum[:, :, :, None] - acum[:, :, None, :], -jnp.inf))
    CB = jnp.einsum('bnigs,bnjgs->bnijg', Cc, Bc)
    y_diag = jnp.einsum('bnijg,bnijgr,bnjgr,bnjgrp->bnigrp', CB, Lm, dtc, xc)
    atot = acum[:, :, -1]
    wj = jnp.exp(atot[:, :, None] - acum) * dtc
    ds = jnp.einsum('bnjgr,bnjgrp,bnjgs->bngrps', wj, xc, Bc)

    def step(s, inp):
        dec, d = inp
        return dec[..., None, None] * s + d, s

    s0g = s0.astype(jnp.float32).reshape(Bsz, G, R, P, N)
    s_fin, s_prev = lax.scan(step, s0g, (jnp.moveaxis(jnp.exp(atot), 1, 0), jnp.moveaxis(ds, 1, 0)))
    y_off = jnp.einsum('bnigs,bngrps,bnigr->bnigrp', Cc, jnp.moveaxis(s_prev, 0, 1), jnp.exp(acum))
    y = (y_diag + y_off).reshape(Bsz, L, H, P)
    return y.astype(x.dtype), s_fin.reshape(Bsz, H, P, N)


def dwconv_centred(x, w, b):
    K, Ch = w.shape
    y = lax.conv_general_dilated(x, w[:, None, :], window_strides=(1,),
                                 padding=[((K - 1) // 2, (K - 1) // 2)],
                                 dimension_numbers=('NWC', 'WIO', 'NWC'),
                                 feature_group_count=Ch)
    return y + b


def gla_branch(u, P, l, s0_fwd, s0_bwd):
    Bsz, L = u['b_q'].shape[:2]
    hk = (Bsz, L, B_HEADS, B_DK)
    hv = (Bsz, L, B_HEADS, B_DV)
    q = u['b_q'].reshape(hk)
    k = u['b_k'].reshape(hk)
    v = u['b_v'].reshape(hv)
    g_f = gla_log_decay(u['b_gf'], P['b_wg'][l, 0], P['b_bg'][l, 0]).reshape(hk)
    g_b = gla_log_decay(u['b_gb'], P['b_wg'][l, 1], P['b_bg'][l, 1]).reshape(hk)
    o_f, s_f = gla_chunked(q, k, v, g_f, s0_fwd)
    o_b, s_b = gla_chunked(flip(q), flip(k), flip(v), flip(g_b), s0_bwd)
    o = rmsnorm(o_f + flip(o_b), P['b_onorm'][l]) * jax.nn.silu(u['b_og']).reshape(hv)
    return o.reshape(Bsz, L, B_HEADS * B_DV) @ P['b_wo'][l], s_f, s_b


def ssd_branch(u, P, l, s0_fwd, s0_bwd):
    Bsz, L = u['c_z'].shape[:2]
    xbc = jax.nn.silu(dwconv_centred(u['c_xbc'], P['c_conv_w'][l], P['c_conv_b'][l]))
    xs = xbc[..., :C_INNER].reshape(Bsz, L, C_HEADS, C_HEADDIM)
    Bm = xbc[..., C_INNER:C_INNER + C_GROUPS * C_STATE].reshape(Bsz, L, C_GROUPS, C_STATE)
    Cm = xbc[..., C_INNER + C_GROUPS * C_STATE:].reshape(Bsz, L, C_GROUPS, C_STATE)
    dt_f = jax.nn.softplus(u['c_dtf'].astype(jnp.float32) + P['c_dt_bias'][l, 0].astype(jnp.float32))
    dt_b = jax.nn.softplus(u['c_dtb'].astype(jnp.float32) + P['c_dt_bias'][l, 1].astype(jnp.float32))
    A_f = -jnp.exp(P['c_A_log'][l, 0].astype(jnp.float32))
    A_b = -jnp.exp(P['c_A_log'][l, 1].astype(jnp.float32))
    y_f, s_f = ssd_chunked(xs, dt_f, A_f, Bm, Cm, s0_fwd)
    y_b, s_b = ssd_chunked(flip(xs), flip(dt_b), A_b, flip(Bm), flip(Cm), s0_bwd)
    y = y_f + flip(y_b) + P['c_D'][l][:, None] * xs
    y = rmsnorm(y.reshape(Bsz, L, C_INNER) * jax.nn.silu(u['c_z']), P['c_norm'][l])
    return y @ P['c_wo'][l], s_f, s_b


def mixer_sublayer(h, P, l, ctx_kv, gla0, ssd0, rope):
    u = split_proj(h, P['w_in'][l])
    ckv = rmsnorm(u['a_ckv'], P['a_kv_norm'][l])
    krope = u['a_krope']
    q = mla_queries(u['a_cq'], P, l)
    k, v = mla_keys_values(ckv, krope, P, l)
    if rope is not None:
        q = rope_tail(q, rope[0], rope[1])
        k = rope_tail(k, rope[0], rope[1])
    if ctx_kv is not None:
        k = jnp.concatenate([k, ctx_kv[0]], axis=1)
        v = jnp.concatenate([v, ctx_kv[1]], axis=1)
    Bsz, L = h.shape[:2]
    o_a = attend(q, k, v).reshape(Bsz, L, A_HEADS * A_V) @ P['a_wo'][l]
    o_b, sg_f, sg_b = gla_branch(u, P, l, gla0[0], gla0[1])
    o_c, ss_f, ss_b = ssd_branch(u, P, l, ssd0[0], ssd0[1])
    g = jax.nn.sigmoid(u['gates'].astype(jnp.float32)).astype(h.dtype)
    g = g.reshape(Bsz, L, 3, D_MODEL)
    merged = g[:, :, 0] * o_a + g[:, :, 1] * o_b + g[:, :, 2] * o_c
    out = merged @ P['w_out'][l]
    return out, ckv, krope, jnp.stack([sg_f, sg_b], axis=1), jnp.stack([ss_f, ss_b], axis=1)


def moe(h, P, l):
    Bsz, L, D = h.shape
    t = h.reshape(Bsz * L, D)
    scores = jax.nn.sigmoid((t @ P['router_w']).astype(jnp.float32))
    sel = scores + P['router_bias'].astype(jnp.float32)
    per_group = N_EXPERTS // N_EXPERT_GROUPS
    group_score = jnp.sum(lax.top_k(sel.reshape(-1, N_EXPERT_GROUPS, per_group), TOP_K)[0], axis=-1)
    best = jnp.argmax(group_score, axis=-1)
    in_group = jnp.arange(N_EXPERTS)[None, :] // per_group == best[:, None]
    _, idx = lax.top_k(jnp.where(in_group, sel, -jnp.inf), TOP_K)
    w = jnp.take_along_axis(scores, idx, axis=-1)
    w = w / jnp.sum(w, axis=-1, keepdims=True)
    gate = jnp.einsum('tk,tke->te', w, jax.nn.one_hot(idx, N_EXPERTS, dtype=jnp.float32)).astype(h.dtype)
    hid = jax.nn.silu(jnp.einsum('td,edf->tef', t, P['e_w1'][l])) * jnp.einsum('td,edf->tef', t, P['e_w3'][l])
    out = jnp.einsum('tef,efd->td', hid * gate[:, :, None], P['e_w2'][l])
    return out.reshape(Bsz, L, D)


def trunk_layer(x, cvec, P, l, ctx_kv, gla0, ssd0, rope):
    m = modulation(cvec, P['w_ada'][l], P['b_ada'][l])
    h = ada_rmsnorm(x, P['norm1'][l], m[:, None, 0], m[:, None, 1])
    mix, ckv, krope, s_gla, s_ssd = mixer_sublayer(h, P, l, ctx_kv, gla0, ssd0, rope)
    x = x + m[:, None, 2] * mix
    h = ada_rmsnorm(x, P['norm2'][l], m[:, None, 3], m[:, None, 4])
    x = x + m[:, None, 5] * moe(h, P, l)
    return x, ckv, krope, s_gla, s_ssd


def setup_inputs(seed: int = 0) -> dict:
    key = jax.random.key(seed)
    keys = iter(jax.random.split(key, 40))

    def nrm(shape, scale):
        return scale * jax.random.normal(next(keys), shape, dtype=jnp.float32)

    def gain(shape):
        return 1.0 + 0.05 * jax.random.normal(next(keys), shape, dtype=jnp.float32)

    dt = jnp.exp(jax.random.uniform(next(keys), (DEPTH, 2, C_HEADS), dtype=jnp.float32,
                                    minval=math.log(1e-3), maxval=math.log(1e-1)))
    c_dt_bias = dt + jnp.log(-jnp.expm1(-dt))
    c_A_log = jnp.log(jax.random.uniform(next(keys), (DEPTH, 2, C_HEADS), dtype=jnp.float32,
                                         minval=1.0, maxval=16.0))
    hbk = B_HEADS * B_DK
    hbv = B_HEADS * B_DV
    return {
        'x_prompt': nrm((BATCH, SEQ, D_MODEL), 1.0),
        'x_sample': nrm((DEC_BATCH, DEC_SEQ, D_MODEL), 1.0),
        'cache_ckv': nrm((DEC_BATCH, DEPTH, PAST_LEN, A_KVLORA), 1.0),
        'cache_krope': nrm((DEC_BATCH, DEPTH, PAST_LEN, A_ROPE), 1.0),
        'state_gla': nrm((DEC_BATCH, DEPTH, 2, B_HEADS, B_DK, B_DV), 1.0),
        'state_ssd': nrm((DEC_BATCH, DEPTH, 2, C_HEADS, C_HEADDIM, C_STATE), 0.1),
        'c': nrm((DEC_BATCH, D_MODEL), 1.0),
        'c_ctx': nrm((D_MODEL,), 1.0),
        'w_ada': nrm((DEPTH, D_MODEL, 6 * D_MODEL), 0.5 * D_MODEL ** -0.5),
        'b_ada': nrm((DEPTH, 6 * D_MODEL), 0.02),
        'norm1': gain((DEPTH, D_MODEL)),
        'norm2': gain((DEPTH, D_MODEL)),
        'w_in': nrm((DEPTH, D_MODEL, D_IN), D_MODEL ** -0.5),
        'a_q_norm': gain((DEPTH, A_QLORA)),
        'a_wq': nrm((DEPTH, A_QLORA, A_HEADS * A_QK), A_QLORA ** -0.5),
        'a_kv_norm': gain((DEPTH, A_KVLORA)),
        'a_wkv': nrm((DEPTH, A_KVLORA, A_HEADS * (A_NOPE + A_V)), A_KVLORA ** -0.5),
        'a_qk_qnorm': gain((DEPTH, A_QK)),
        'a_qk_knorm': gain((DEPTH, A_QK)),
        'a_wo': nrm((DEPTH, A_HEADS * A_V, D_MODEL), (A_HEADS * A_V) ** -0.5),
        'b_wg': nrm((DEPTH, 2, B_GATE_RANK, hbk), B_GATE_RANK ** -0.5),
        'b_bg': nrm((DEPTH, 2, hbk), 0.1),
        'b_onorm': gain((DEPTH, B_DV)),
        'b_wo': nrm((DEPTH, hbv, D_MODEL), hbv ** -0.5),
        'c_conv_w': nrm((DEPTH, C_CONV, C_XBC), C_CONV ** -0.5),
        'c_conv_b': nrm((DEPTH, C_XBC), 0.02),
        'c_dt_bias': c_dt_bias,
        'c_A_log': c_A_log,
        'c_D': gain((DEPTH, C_HEADS)),
        'c_norm': gain((DEPTH, C_INNER)),
        'c_wo': nrm((DEPTH, C_INNER, D_MODEL), C_INNER ** -0.5),
        'w_out': nrm((DEPTH, D_MODEL, D_MODEL), D_MODEL ** -0.5),
        'router_w': nrm((D_MODEL, N_EXPERTS), D_MODEL ** -0.5),
        'router_bias': nrm((N_EXPERTS,), 0.01),
        'e_w1': nrm((DEPTH, N_EXPERTS, D_MODEL, D_EXPERT), D_MODEL ** -0.5),
        'e_w3': nrm((DEPTH, N_EXPERTS, D_MODEL, D_EXPERT), D_MODEL ** -0.5),
        'e_w2': nrm((DEPTH, N_EXPERTS, D_EXPERT, D_MODEL), D_EXPERT ** -0.5),
    }


def reference(x_prompt, x_sample, cache_ckv, cache_krope, state_gla, state_ssd, c, c_ctx,
              w_ada, b_ada, norm1, norm2, w_in, a_q_norm, a_wq, a_kv_norm, a_wkv,
              a_qk_qnorm, a_qk_knorm, a_wo, b_wg, b_bg, b_onorm, b_wo, c_conv_w, c_conv_b,
              c_dt_bias, c_A_log, c_D, c_norm, c_wo, w_out, router_w, router_bias,
              e_w1, e_w3, e_w2):
    P = dict(w_ada=w_ada, b_ada=b_ada, norm1=norm1, norm2=norm2, w_in=w_in,
             a_q_norm=a_q_norm, a_wq=a_wq, a_kv_norm=a_kv_norm, a_wkv=a_wkv,
             a_qk_qnorm=a_qk_qnorm, a_qk_knorm=a_qk_knorm, a_wo=a_wo,
             b_wg=b_wg, b_bg=b_bg, b_onorm=b_onorm, b_wo=b_wo,
             c_conv_w=c_conv_w, c_conv_b=c_conv_b, c_dt_bias=c_dt_bias, c_A_log=c_A_log,
             c_D=c_D, c_norm=c_norm, c_wo=c_wo, w_out=w_out,
             router_w=router_w, router_bias=router_bias, e_w1=e_w1, e_w3=e_w3, e_w2=e_w2)

    bp = x_prompt.shape[0]
    z_gla = jnp.zeros((bp, B_HEADS, B_DK, B_DV), jnp.float32)
    z_ssd = jnp.zeros((bp, C_HEADS, C_HEADDIM, C_STATE), jnp.float32)
    xp = x_prompt
    ckvs, kropes, glas, ssds = [], [], [], []
    for l in range(DEPTH):
        xp, ckv, krope, s_gla, s_ssd = trunk_layer(xp, c_ctx[None], P, l, None,
                                                   (z_gla, z_gla), (z_ssd, z_ssd), None)
        ckvs.append(ckv)
        kropes.append(krope)
        glas.append(s_gla)
        ssds.append(s_ssd)
    y_prompt = xp
    new_ckv = jnp.stack(ckvs, axis=1)
    new_krope = jnp.stack(kropes, axis=1)
    new_state_gla = jnp.stack(glas, axis=1).astype(x_prompt.dtype)
    new_state_ssd = jnp.stack(ssds, axis=1).astype(x_prompt.dtype)

    rows = x_sample.shape[1] // GRID_W
    rope = axial_rope_angles(rows)
    xs = x_sample
    for l in range(DEPTH):
        k_ctx, v_ctx = mla_keys_values(cache_ckv[:, l], cache_krope[:, l], P, l)
        xs = trunk_layer(xs, c, P, l, (k_ctx, v_ctx),
                         (state_gla[:, l, 0], state_gla[:, l, 1]),
                         (state_ssd[:, l, 0], state_ssd[:, l, 1]), rope)[0]
    y_sample = xs
    return (y_prompt, y_sample, new_ckv, new_krope, new_state_gla, new_state_ssd)
```

```python
import functools
import math

import numpy as np
import jax
import jax.numpy as jnp
from jax import lax
from jax.experimental import pallas as pl
from jax.experimental.pallas import tpu as pltpu

F32 = jnp.float32
BF16 = jnp.bfloat16

D_MODEL = 1024
BATCH = 16
SEQ = 256
DEPTH = 2
DEC_BATCH = 2
DEC_SEQ = 2048
PAST_LEN = 512
GRID_W = 64
EPS = 1e-6
ROPE_BASE = 10000.0

A_HEADS = 16
A_NOPE = 64
A_ROPE = 32
A_QK = A_NOPE + A_ROPE
A_V = 64
A_QLORA = 512
A_KVLORA = 256

B_HEADS = 4
B_DK = 128
B_DV = 256
B_GATE_RANK = 16
B_GATE_NORM = 16.0

C_HEADS = 16
C_HEADDIM = 64
C_INNER = C_HEADS * C_HEADDIM
C_GROUPS = 2
C_STATE = 128
C_XBC = C_INNER + 2 * C_GROUPS * C_STATE
C_CONV = 5

N_EXPERTS = 16
N_EXPERT_GROUPS = 4
D_EXPERT = 512

N_CTX = BATCH * SEQ
N_DEC = DEC_BATCH * DEC_SEQ
N_TOK = N_CTX + N_DEC

LANE = 128
HEAD_PAD = 128
CHUNK = 128

U_CQ = 0
U_CKV = 512
U_KR = 768
U_QK = 1024
U_V = 2048
U_OG = 3072
U_Z = 4096
U_X = 5120
U_GATES = 6144
U_BC = 9216
U_WIDTH = 9728
S_GF, S_GB, S_DTF, S_DTB, S_KR = 0, 16, 32, 48, 64

_IN_SPLITS = (
    ('a_cq', A_QLORA), ('a_ckv', A_KVLORA), ('a_krope', A_ROPE),
    ('b_q', B_HEADS * B_DK), ('b_k', B_HEADS * B_DK), ('b_v', B_HEADS * B_DV),
    ('b_og', B_HEADS * B_DV), ('b_gf', B_GATE_RANK), ('b_gb', B_GATE_RANK),
    ('c_z', C_INNER), ('c_xbc', C_XBC), ('c_dtf', C_HEADS), ('c_dtb', C_HEADS),
    ('gates', 3 * D_MODEL),
)
_ROPE_PERM = np.concatenate([np.arange(0, A_ROPE, 2), np.arange(1, A_ROPE, 2)])


def _col_slices(w):
    parts, start = {}, 0
    for name, size in _IN_SPLITS:
        parts[name] = w[:, start:start + size]
        start += size
    return parts


def _dot(a, b):
    return jnp.dot(a, b, preferred_element_type=F32)


def _dot_nt(a, b):
    return lax.dot_general(a, b, (((1,), (1,)), ((), ())), preferred_element_type=F32)


def _dot_tn(a, b):
    return lax.dot_general(a, b, (((0,), (0,)), ((), ())), preferred_element_type=F32)


def _split2(x):
    hi = x.astype(BF16)
    lo = (x - hi.astype(F32)).astype(BF16)
    return hi, lo


def _silu(x):
    return x * jax.nn.sigmoid(x)


def _softplus(x):
    return jnp.maximum(x, 0.0) + jnp.log1p(jnp.exp(-jnp.abs(x)))


def _log_sigmoid(x):
    return jnp.minimum(x, 0.0) - jnp.log1p(jnp.exp(-jnp.abs(x)))


def _mod_kernel(c_ref, w_ref, b_ref, o_ref):
    s = _silu(c_ref[...]).astype(BF16)
    o_ref[...] = _dot(s, w_ref[...].astype(BF16)) + b_ref[...]


def _mod_call(cvecs, w_ada, b_ada):
    tn = 1536
    return pl.pallas_call(
        _mod_kernel,
        grid=(DEPTH, 6 * D_MODEL // tn),
        in_specs=[pl.BlockSpec((8, D_MODEL), lambda l, j: (0, 0)),
                  pl.BlockSpec((None, D_MODEL, tn), lambda l, j: (l, 0, j)),
                  pl.BlockSpec((None, 1, tn), lambda l, j: (l, 0, j))],
        out_specs=pl.BlockSpec((None, 8, tn), lambda l, j: (l, 0, j)),
        out_shape=jax.ShapeDtypeStruct((DEPTH, 8, 6 * D_MODEL), F32),
        name="adaln_mod",
    )(cvecs, w_ada, b_ada.reshape(DEPTH, 1, 6 * D_MODEL))


def _mod_row(i, tm, layer):
    n_ctx = N_CTX // tm
    per_b = DEC_SEQ // tm
    return layer * 8 + jnp.where(i < n_ctx, 0, 1 + (i - n_ctx) // per_b)


def _norm_proj_kernel(x_ref, mod_ref, nw_ref, w_ref, ws_ref, u_ref, s_ref, h_scr):
    @pl.when(pl.program_id(1) == 0)
    def _():
        x = x_ref[...]
        m = mod_ref[...]
        y = x * lax.rsqrt(jnp.mean(x * x, axis=-1, keepdims=True) + EPS) * nw_ref[...]
        h = (y * (1.0 + m[:, D_MODEL:2 * D_MODEL]) + m[:, 0:D_MODEL]).astype(BF16)
        h_scr[...] = h
        s_ref[...] = _dot(h, ws_ref[...])

    u_ref[...] = _dot(h_scr[...], w_ref[...]).astype(BF16)


def _norm_proj_call(x, mod3, nw, w_main, w_small, layer):
    tm, tn = 1024, 512
    return pl.pallas_call(
        _norm_proj_kernel,
        grid=(N_TOK // tm, U_WIDTH // tn),
        in_specs=[pl.BlockSpec((tm, D_MODEL), lambda i, j: (i, 0)),
                  pl.BlockSpec((None, 1, 6 * D_MODEL), lambda i, j: (_mod_row(i, tm, layer), 0, 0)),
                  pl.BlockSpec((1, D_MODEL), lambda i, j: (0, 0)),
                  pl.BlockSpec((D_MODEL, tn), lambda i, j: (0, j)),
                  pl.BlockSpec((D_MODEL, LANE), lambda i, j: (0, 0))],
        out_specs=[pl.BlockSpec((tm, tn), lambda i, j: (i, j)),
                   pl.BlockSpec((tm, LANE), lambda i, j: (i, 0))],
        out_shape=[jax.ShapeDtypeStruct((N_TOK, U_WIDTH), BF16),
                   jax.ShapeDtypeStruct((N_TOK, LANE), F32)],
        scratch_shapes=[pltpu.VMEM((tm, D_MODEL), BF16)],
        compiler_params=pltpu.CompilerParams(dimension_semantics=("parallel", "arbitrary")),
        name="norm_in_proj",
    )(x, mod3, nw, w_main, w_small)


def _head_norm_rope(t, gain, cos, sin, lane):
    ss = jnp.sum(t * t, axis=-1, keepdims=True)
    t = t * lax.rsqrt(ss * (1.0 / A_QK) + EPS) * gain
    if cos is not None:
        half = A_ROPE // 2
        swapped = jnp.where(lane < A_NOPE + half, pltpu.roll(t, HEAD_PAD - half, 1), pltpu.roll(t, half, 1))
        t = t * cos + swapped * sin
    return t


def _attn_prep_kernel(*refs, has_q, norm_kv, rope, emit_ckv):
    refs = list(refs)
    cq_ref = refs.pop(0) if has_q else None
    ckv_ref, kr_ref = refs.pop(0), refs.pop(0)
    if has_q:
        qnw_ref, wq_ref, gq_ref = refs.pop(0), refs.pop(0), refs.pop(0)
    kvnw_ref, wkn_ref, wv_ref, gk_ref = refs.pop(0), refs.pop(0), refs.pop(0), refs.pop(0)
    cos = sin = None
    if rope:
        cos, sin = refs.pop(0)[...], refs.pop(0)[...]
    q_ref = refs.pop(0) if has_q else None
    k_ref, v_ref = refs.pop(0), refs.pop(0)
    ckvn_ref = refs.pop(0) if emit_ckv else None

    tm = ckv_ref.shape[0]
    lane = lax.broadcasted_iota(jnp.int32, (tm, HEAD_PAD), 1)

    ckv = ckv_ref[...].astype(F32)
    if norm_kv:
        ckv = ckv * lax.rsqrt(jnp.mean(ckv * ckv, axis=-1, keepdims=True) + EPS) * kvnw_ref[...]
    if emit_ckv:
        ckvn_ref[...] = ckv
    ckv_b = ckv.astype(BF16)
    v_ref[...] = _dot(ckv_b, wv_ref[...]).astype(BF16)
    kf = _dot(ckv_b, wkn_ref[...])
    kr = kr_ref[...].astype(F32)
    gk = gk_ref[...]
    for h in range(A_HEADS):
        sl = slice(h * HEAD_PAD, (h + 1) * HEAD_PAD)
        k_ref[:, sl] = _head_norm_rope(kf[:, sl] + kr, gk, cos, sin, lane).astype(BF16)

    if has_q:
        cq = cq_ref[...].astype(F32)
        cq = cq * lax.rsqrt(jnp.mean(cq * cq, axis=-1, keepdims=True) + EPS) * qnw_ref[...]
        qf = _dot(cq.astype(BF16), wq_ref[...])
        gq = gq_ref[...]
        for h in range(A_HEADS):
            sl = slice(h * HEAD_PAD, (h + 1) * HEAD_PAD)
            q_ref[:, sl] = _head_norm_rope(qf[:, sl], gq, cos, sin, lane).astype(BF16)


def _attn_prep_call(n_rows, row0, srcs, wts, *, has_q, norm_kv, rope, emit_ckv, tables=None):
    tm = 256
    r0 = row0 // tm
    pw = A_HEADS * HEAD_PAD
    full = lambda shape: pl.BlockSpec(shape, lambda i: (0,) * len(shape))
    args, in_specs = [], []
    if len(srcs) == 1:
        u = srcs[0]
        if has_q:
            args.append(u)
            in_specs.append(pl.BlockSpec((tm, A_QLORA), lambda i: (i + r0, U_CQ // A_QLORA)))
        args += [u, u]
        in_specs += [pl.BlockSpec((tm, A_KVLORA), lambda i: (i + r0, U_CKV // A_KVLORA)),
                     pl.BlockSpec((tm, LANE), lambda i: (i + r0, U_KR // LANE))]
    else:
        args += list(srcs)
        in_specs += [pl.BlockSpec((tm, A_KVLORA), lambda i: (i, 0)),
                     pl.BlockSpec((tm, LANE), lambda i: (i, 0))]
    if has_q:
        args += [wts['qnw'], wts['wq'], wts['gq']]
        in_specs += [full((1, A_QLORA)), full((A_QLORA, pw)), full((1, HEAD_PAD))]
    args += [wts['kvnw'], wts['wkn'], wts['wv'], wts['gk']]
    in_specs += [full((1, A_KVLORA)), full((A_KVLORA, pw)), full((A_KVLORA, pw)), full((1, HEAD_PAD))]
    if rope:
        per_seq = DEC_SEQ // tm
        args += list(tables)
        in_specs += [pl.BlockSpec((tm, HEAD_PAD), lambda i: (i % per_seq, 0))] * 2
    out_specs, out_shape = [], []
    n_out = (1 if has_q else 0) + 2
    for _ in range(n_out):
        out_specs.append(pl.BlockSpec((tm, pw), lambda i: (i, 0)))
        out_shape.append(jax.ShapeDtypeStruct((n_rows, pw), BF16))
    if emit_ckv:
        out_specs.append(pl.BlockSpec((tm, A_KVLORA), lambda i: (i, 0)))
        out_shape.append(jax.ShapeDtypeStruct((n_rows, A_KVLORA), F32))
    return pl.pallas_call(
        functools.partial(_attn_prep_kernel, has_q=has_q, norm_kv=norm_kv, rope=rope, emit_ckv=emit_ckv),
        grid=(n_rows // tm,),
        in_specs=in_specs, out_specs=out_specs, out_shape=out_shape,
        compiler_params=pltpu.CompilerParams(dimension_semantics=("parallel",)),
        name="mla_prep",
    )(*args)


def _attn_kernel(*refs, hps, has_ctx):
    if has_ctx:
        q_ref, k_ref, v_ref, kc_ref, vc_ref, o_ref = refs
    else:
        q_ref, k_ref, v_ref, o_ref = refs
    for hh in range(hps):
        sl = slice(hh * HEAD_PAD, (hh + 1) * HEAD_PAD)
        q = q_ref[:, sl]
        s = _dot_nt(q, k_ref[:, sl])
        m = jnp.max(s, axis=-1, keepdims=True)
        if has_ctx:
            s2 = _dot_nt(q, kc_ref[:, sl])
            m = jnp.maximum(m, jnp.max(s2, axis=-1, keepdims=True))
        p = jnp.exp(s - m)
        den = jnp.sum(p, axis=-1, keepdims=True)
        o = _dot(p.astype(BF16), v_ref[:, sl])
        if has_ctx:
            p2 = jnp.exp(s2 - m)
            den = den + jnp.sum(p2, axis=-1, keepdims=True)
            o = o + _dot(p2.astype(BF16), vc_ref[:, sl])
        o_ref[:, sl] = (o / den).astype(BF16)


def _attn_call(q, k, v, ctx_kv, *, n_seq, seq_len, hps, tq):
    pw = A_HEADS * HEAD_PAD
    bw = hps * HEAD_PAD
    nq = seq_len // tq
    in_specs = [pl.BlockSpec((tq, bw), lambda b, h, i: (b * nq + i, h)),
                pl.BlockSpec((seq_len, bw), lambda b, h, i: (b, h)),
                pl.BlockSpec((seq_len, bw), lambda b, h, i: (b, h))]
    args = [q, k, v]
    if ctx_kv is not None:
        in_specs += [pl.BlockSpec((PAST_LEN, bw), lambda b, h, i: (b, h))] * 2
        args += list(ctx_kv)
    return pl.pallas_call(
        functools.partial(_attn_kernel, hps=hps, has_ctx=ctx_kv is not None),
        grid=(n_seq, A_HEADS // hps, nq),
        in_specs=in_specs,
        out_specs=pl.BlockSpec((tq, bw), lambda b, h, i: (b * nq + i, h)),
        out_shape=jax.ShapeDtypeStruct((n_seq * seq_len, pw), BF16),
        compiler_params=pltpu.CompilerParams(dimension_semantics=("parallel", "parallel", "arbitrary")),
        name="mla_attention",
    )(*args)


def _conv_kernel(x_ref, w_ref, b_ref, o_ref):
    x = x_ref[...].astype(F32)
    n = x.shape[0]
    row = lax.broadcasted_iota(jnp.int32, x.shape, 0)
    half = (C_CONV - 1) // 2
    acc = x * w_ref[half:half + 1, :] + b_ref[...]
    for d in range(-half, half + 1):
        if d == 0:
            continue
        shifted = pltpu.roll(x, (-d) % n, 0)
        valid = jnp.logical_and(row + d >= 0, row + d < n)
        acc = acc + jnp.where(valid, shifted, 0.0) * w_ref[half + d:half + d + 1, :]
    o_ref[...] = _silu(acc).astype(BF16)


def _conv_call(u, w8, b, *, n_seq, seq_len, row0):
    tc = 512
    b0 = row0 // seq_len
    nx = C_INNER // tc
    ucol = lambda j: jnp.where(j < nx, U_X // tc + j, U_BC // tc + j - nx)
    return pl.pallas_call(
        _conv_kernel,
        grid=(n_seq, C_XBC // tc),
        in_specs=[pl.BlockSpec((seq_len, tc), lambda s, j: (s + b0, ucol(j))),
                  pl.BlockSpec((8, tc), lambda s, j: (0, j)),
                  pl.BlockSpec((1, tc), lambda s, j: (0, j))],
        out_specs=pl.BlockSpec((seq_len, tc), lambda s, j: (s, j)),
        out_shape=jax.ShapeDtypeStruct((n_seq * seq_len, C_XBC), BF16),
        compiler_params=pltpu.CompilerParams(dimension_semantics=("parallel", "parallel")),
        name="ssd_conv",
    )(u, w8, b)


def _ssd_kernel(*refs, off, bwd, has_init):
    refs = list(refs)
    x_ref, bc_ref, s_ref, dtb_ref, an_ref, tri_ref = (refs.pop(0) for _ in range(6))
    st0_ref = refs.pop(0) if has_init else None
    y_ref, stf_ref, st_scr = refs

    @pl.when(pl.program_id(1) == 0)
    def _():
        st_scr[...] = st0_ref[...] if has_init else jnp.zeros(st_scr.shape, F32)

    t = CHUNK
    dt = _softplus(s_ref[...] + dtb_ref[...])
    a = dt * an_ref[...]
    a_hi, a_lo = _split2(a)
    tri = tri_ref[...]
    acum = _dot(tri, a_hi) + _dot(tri, a_lo)
    acum_t = acum.T
    dt_t = dt.T
    edge = 0 if bwd else t - 1
    atot = acum[edge:edge + 1, :]
    ri = lax.broadcasted_iota(jnp.int32, (t, t), 0)
    ci = lax.broadcasted_iota(jnp.int32, (t, t), 1)
    causal = (ci >= ri) if bwd else (ci <= ri)
    lane = lax.broadcasted_iota(jnp.int32, (t, LANE), 1)
    low = lane < C_HEADDIM
    bc = bc_ref[...]
    gs = C_GROUPS * C_STATE
    pairs_per_group = C_HEADS // C_GROUPS // 2
    cb, bmat, cmat = [], [], []
    for g in range(C_GROUPS):
        bmat.append(bc[:, g * C_STATE:(g + 1) * C_STATE])
        cmat.append(bc[:, gs + g * C_STATE:gs + (g + 1) * C_STATE])
        cb.append(_dot_nt(cmat[g], bmat[g]))
    for m in range(C_HEADS // 2):
        g = m // pairs_per_group
        xp = x_ref[:, m * LANE:(m + 1) * LANE]
        y_diag, e_in, wj = [], [], []
        for hh in range(2):
            la = off + 2 * m + hh
            acol = acum[:, la:la + 1]
            arow = acum_t[la:la + 1, :]
            lm = jnp.exp(jnp.where(causal, acol - arow, -jnp.inf))
            w = (cb[g] * lm * dt_t[la:la + 1, :]).astype(BF16)
            y_diag.append(_dot(w, xp))
            e_in.append(jnp.exp(acol))
            wj.append(jnp.exp(atot[:, la:la + 1] - acol) * dt[:, la:la + 1])
        st = st_scr[m]
        y_off = _dot_nt(cmat[g], st.astype(BF16)) * jnp.where(low, e_in[0], e_in[1])
        y_ref[:, m * LANE:(m + 1) * LANE] = (jnp.where(low, y_diag[0], y_diag[1]) + y_off).astype(BF16)
        xw = (xp.astype(F32) * jnp.where(low, wj[0], wj[1])).astype(BF16)
        la = off + 2 * m
        prow = lax.broadcasted_iota(jnp.int32, (LANE, C_STATE), 0)
        dec = jnp.where(prow < C_HEADDIM, jnp.exp(atot[:, la:la + 1]), jnp.exp(atot[:, la + 1:la + 2]))
        st_scr[m] = dec * st + _dot_tn(xw, bmat[g])

    @pl.when(pl.program_id(1) == pl.num_programs(1) - 1)
    def _():
        stf_ref[...] = st_scr[...]


def _ssd_call(xbc, s_all, dtb, aneg, tri, st0, *, n_seq, seq_len, row0, bwd):
    t = CHUNK
    nch = seq_len // t
    r0 = row0 // t
    off = S_DTB if bwd else S_DTF
    cidx = (lambda c: nch - 1 - c) if bwd else (lambda c: c)
    npair = C_HEADS // 2
    in_specs = [pl.BlockSpec((t, C_INNER), lambda b, c: (b * nch + cidx(c), 0)),
                pl.BlockSpec((t, 2 * C_GROUPS * C_STATE), lambda b, c: (b * nch + cidx(c), C_INNER // (2 * C_GROUPS * C_STATE))),
                pl.BlockSpec((t, LANE), lambda b, c: (r0 + b * nch + cidx(c), 0)),
                pl.BlockSpec((1, LANE), lambda b, c: (0, 0)),
                pl.BlockSpec((1, LANE), lambda b, c: (0, 0)),
                pl.BlockSpec((t, t), lambda b, c: (0, 0))]
    args = [xbc, xbc, s_all, dtb, aneg, tri]
    if st0 is not None:
        in_specs.append(pl.BlockSpec((None, npair, LANE, C_STATE), lambda b, c: (b, 0, 0, 0)))
        args.append(st0)
    return pl.pallas_call(
        functools.partial(_ssd_kernel, off=off, bwd=bwd, has_init=st0 is not None),
        grid=(n_seq, nch),
        in_specs=in_specs,
        out_specs=[pl.BlockSpec((t, C_INNER), lambda b, c: (b * nch + cidx(c), 0)),
                   pl.BlockSpec((None, npair, LANE, C_STATE), lambda b, c: (b, 0, 0, 0))],
        out_shape=[jax.ShapeDtypeStruct((n_seq * seq_len, C_INNER), BF16),
                   jax.ShapeDtypeStruct((n_seq, npair, LANE, C_STATE), F32)],
        scratch_shapes=[pltpu.VMEM((npair, LANE, C_STATE), F32)],
        compiler_params=pltpu.CompilerParams(dimension_semantics=("parallel", "arbitrary")),
        name="ssd_scan",
    )(*args)


def _gla_consts(bwd):
    t = CHUNK
    nlev = int(math.log2(t))
    idx = np.arange(t)
    mats = np.zeros((nlev + 2, t, t), np.float32)
    masks = np.zeros((nlev + 1, t, t), np.float32)
    rowm = np.zeros((nlev, t, LANE), np.float32)
    masks[0] = np.eye(t)
    for lvl in range(nlev):
        s = 1 << lvl
        blk = idx // (2 * s)
        upper = (idx % (2 * s)) >= s
        last_low = blk * 2 * s + s - 1
        rowtok = ~upper if bwd else upper
        for i in range(t):
            r = last_low[i]
            if not bwd:
                if upper[i]:
                    mats[lvl, i, r + 1:i + 1] = 1.0
                else:
                    mats[lvl, i, i + 1:r + 1] = 1.0
            else:
                if upper[i]:
                    mats[lvl, i, r + 1:i] = 1.0
                else:
                    mats[lvl, i, i:r + 1] = 1.0
        masks[lvl + 1] = ((blk[:, None] == blk[None, :]) & rowtok[:, None] & (~rowtok)[None, :])
        rowm[lvl] = rowtok[:, None].astype(np.float32)
    incl = (idx[None, :] >= idx[:, None]) if bwd else (idx[None, :] <= idx[:, None])
    mats[nlev] = incl
    mats[nlev + 1] = 1.0 - incl
    return (jnp.asarray(mats.reshape((nlev + 2) * t, t), BF16), jnp.asarray(masks), jnp.asarray(rowm))


def _gla_kernel(*refs, bwd, has_init):
    refs = list(refs)
    qk_ref, v_ref, s_ref, wgh_ref, wgl_ref, bg_ref, mall_ref, mask_ref, rowm_ref = (refs.pop(0) for _ in range(9))
    st0_ref = refs.pop(0) if has_init else None
    o_ref, stf_ref, st_scr = refs

    @pl.when(pl.program_id(1) == 0)
    def _():
        st_scr[...] = st0_ref[...] if has_init else jnp.zeros(st_scr.shape, F32)

    t = CHUNK
    nlev = rowm_ref.shape[0]
    hk = B_HEADS * B_DK
    s_hi, s_lo = _split2(s_ref[...])
    wgh = wgh_ref[...]
    logit = _dot(s_hi, wgh) + _dot(s_lo, wgh) + _dot(s_hi, wgl_ref[...]) + bg_ref[...]
    g_all = _log_sigmoid(logit) * (1.0 / B_GATE_NORM)
    mall = mall_ref[...]
    edge = 0 if bwd else t - 1
    for h in range(B_HEADS):
        ks = slice(h * B_DK, (h + 1) * B_DK)
        vs = slice(h * B_DV, (h + 1) * B_DV)
        q = qk_ref[:, ks].astype(F32) * (B_DK ** -0.5)
        k = qk_ref[:, hk + h * B_DK:hk + (h + 1) * B_DK]
        kf = k.astype(F32)
        v = v_ref[:, vs]
        e_all = jnp.exp(_dot(mall, g_all[:, ks].astype(BF16)))
        amat = _dot_nt(q.astype(BF16), k) * mask_ref[0]
        for lvl in range(nlev):
            e = e_all[lvl * t:(lvl + 1) * t]
            rm = rowm_ref[lvl]
            qe = (q * e * rm).astype(BF16)
            ke = (kf * e * (1.0 - rm)).astype(BF16)
            amat = amat + _dot_nt(qe, ke) * mask_ref[lvl + 1]
        e_in = e_all[nlev * t:(nlev + 1) * t]
        e_out = e_all[(nlev + 1) * t:(nlev + 2) * t]
        st = st_scr[h]
        o = _dot(amat.astype(BF16), v) + _dot_nt((q * e_in).astype(BF16), st.astype(BF16))
        o_ref[:, vs] = o.astype(BF16)
        st_scr[h] = st * e_in[edge:edge + 1, :] + _dot_tn(v, (kf * e_out).astype(BF16))

    @pl.when(pl.program_id(1) == pl.num_programs(1) - 1)
    def _():
        stf_ref[...] = st_scr[...]


def _gla_call(u, s_all, wgh, wgl, bg, consts, st0, *, n_seq, seq_len, row0, bwd):
    t = CHUNK
    nch = seq_len // t
    r0 = row0 // t
    hk, hv = B_HEADS * B_DK, B_HEADS * B_DV
    cidx = (lambda c: nch - 1 - c) if bwd else (lambda c: c)
    mall, masks, rowm = consts
    full = lambda a: pl.BlockSpec(a.shape, lambda b, c: (0,) * a.ndim)
    in_specs = [pl.BlockSpec((t, 2 * hk), lambda b, c: (r0 + b * nch + cidx(c), U_QK // (2 * hk))),
                pl.BlockSpec((t, hv), lambda b, c: (r0 + b * nch + cidx(c), U_V // hv)),
                pl.BlockSpec((t, LANE), lambda b, c: (r0 + b * nch + cidx(c), 0)),
                full(wgh), full(wgl), full(bg), full(mall), full(masks), full(rowm)]
    args = [u, u, s_all, wgh, wgl, bg, mall, masks, rowm]
    if st0 is not None:
        in_specs.append(pl.BlockSpec((None, B_HEADS, B_DV, B_DK), lambda b, c: (b, 0, 0, 0)))
        args.append(st0)
    return pl.pallas_call(
        functools.partial(_gla_kernel, bwd=bwd, has_init=st0 is not None),
        grid=(n_seq, nch),
        in_specs=in_specs,
        out_specs=[pl.BlockSpec((t, hv), lambda b, c: (b * nch + cidx(c), 0)),
                   pl.BlockSpec((None, B_HEADS, B_DV, B_DK), lambda b, c: (b, 0, 0, 0))],
        out_shape=[jax.ShapeDtypeStruct((n_seq * seq_len, hv), BF16),
                   jax.ShapeDtypeStruct((n_seq, B_HEADS, B_DV, B_DK), F32)],
        scratch_shapes=[pltpu.VMEM((B_HEADS, B_DV, B_DK), F32)],
        compiler_params=pltpu.CompilerParams(dimension_semantics=("parallel", "arbitrary")),
        name="gla_scan",
    )(*args)


def _merge_kernel(x_ref, mod_ref, oa_ref, gf_ref, gb_ref, og_ref, yf_ref, yb_ref, xc_ref, z_ref, gt_ref,
                  awo_ref, onw_ref, bwo_ref, dexp_ref, cnw_ref, cwo_ref, wout_ref, o_ref):
    o_a = _dot(oa_ref[...], awo_ref[...])

    og = gf_ref[...].astype(F32) + gb_ref[...].astype(F32)
    onw = onw_ref[...]
    parts = []
    for h in range(B_HEADS):
        th = og[:, h * B_DV:(h + 1) * B_DV]
        parts.append(th * lax.rsqrt(jnp.mean(th * th, axis=-1, keepdims=True) + EPS) * onw)
    ob_in = jnp.concatenate(parts, axis=-1) * _silu(og_ref[...].astype(F32))
    o_b = _dot(ob_in.astype(BF16), bwo_ref[...])

    y = yf_ref[...].astype(F32) + yb_ref[...].astype(F32) + dexp_ref[...] * xc_ref[...].astype(F32)
    y = y * _silu(z_ref[...].astype(F32))
    y = y * lax.rsqrt(jnp.mean(y * y, axis=-1, keepdims=True) + EPS) * cnw_ref[...]
    o_c = _dot(y.astype(BF16), cwo_ref[...])

    d = D_MODEL
    merged = (jax.nn.sigmoid(gt_ref[:, 0:d].astype(F32)) * o_a
              + jax.nn.sigmoid(gt_ref[:, d:2 * d].astype(F32)) * o_b
              + jax.nn.sigmoid(gt_ref[:, 2 * d:3 * d].astype(F32)) * o_c)
    out = _dot(merged.astype(BF16), wout_ref[...])
    o_ref[...] = x_ref[...] + mod_ref[:, 2 * d:3 * d] * out


def _merge_call(x, mod3, u, o_attn, g_f, g_b, y_f, y_b, xbc, wts, *, n_rows, row0, layer):
    tm = 256
    r0 = row0 // tm
    d = D_MODEL
    pw = A_HEADS * HEAD_PAD
    loc = lambda w: pl.BlockSpec((tm, w), lambda i: (i, 0))
    full = lambda a: pl.BlockSpec(a.shape, lambda i: (0,) * a.ndim)
    in_specs = [pl.BlockSpec((tm, d), lambda i: (i + r0, 0)),
                pl.BlockSpec((None, 1, 6 * d), lambda i: (_mod_row(i + r0, tm, layer), 0, 0)),
                loc(pw), loc(d), loc(d),
                pl.BlockSpec((tm, d), lambda i: (i + r0, U_OG // d)),
                loc(d), loc(d),
                pl.BlockSpec((tm, d), lambda i: (i, 0)),
                pl.BlockSpec((tm, d), lambda i: (i + r0, U_Z // d)),
                pl.BlockSpec((tm, 3 * d), lambda i: (i + r0, U_GATES // (3 * d)))]
    w_args = [wts[n] for n in ('awo', 'onw', 'bwo', 'dexp', 'cnw', 'cwo', 'wout')]
    in_specs += [full(a) for a in w_args]
    return pl.pallas_call(
        _merge_kernel,
        grid=(n_rows // tm,),
        in_specs=in_specs,
        out_specs=pl.BlockSpec((tm, d), lambda i: (i + r0, 0)),
        out_shape=jax.ShapeDtypeStruct((N_TOK, d), F32),
        input_output_aliases={0: 0},
        compiler_params=pltpu.CompilerParams(dimension_semantics=("parallel",)),
        name="mixer_merge",
    )(x, mod3, o_attn, g_f, g_b, u, y_f, y_b, xbc, u, u, *w_args)


def _router_gates(logits, bias):
    lane = lax.broadcasted_iota(jnp.int32, logits.shape, 1)
    neg = -jnp.inf
    per_group = N_EXPERTS // N_EXPERT_GROUPS
    scores = jax.nn.sigmoid(logits)
    sel = jnp.where(lane < N_EXPERTS, scores + bias, neg)

    def top2(v):
        m1 = jnp.max(v, axis=-1, keepdims=True)
        i1 = jnp.min(jnp.where(v == m1, lane, LANE), axis=-1, keepdims=True)
        v2 = jnp.where(lane == i1, neg, v)
        m2 = jnp.max(v2, axis=-1, keepdims=True)
        i2 = jnp.min(jnp.where(v2 == m2, lane, LANE), axis=-1, keepdims=True)
        return m1, i1, m2, i2

    best_v = best_g = None
    for g in range(N_EXPERT_GROUPS):
        in_g = jnp.logical_and(lane >= g * per_group, lane < (g + 1) * per_group)
        m1, _, m2, _ = top2(jnp.where(in_g, sel, neg))
        gs = m1 + m2
        if g == 0:
            best_v, best_g = gs, jnp.zeros_like(gs, dtype=jnp.int32)
        else:
            upd = gs > best_v
            best_g = jnp.where(upd, g, best_g)
            best_v = jnp.where(upd, gs, best_v)
    in_best = jnp.logical_and(lane // per_group == best_g, lane < N_EXPERTS)
    _, i1, _, i2 = top2(jnp.where(in_best, sel, neg))
    w1 = jnp.sum(jnp.where(lane == i1, scores, 0.0), axis=-1, keepdims=True)
    w2 = jnp.sum(jnp.where(lane == i2, scores, 0.0), axis=-1, keepdims=True)
    tot = w1 + w2
    return jnp.where(lane == i1, w1 / tot, 0.0) + jnp.where(lane == i2, w2 / tot, 0.0)


def _moe_kernel(x_ref, mod_ref, nw_ref, rwh_ref, rwl_ref, rb_ref, w1_ref, w3_ref, w2_ref, o_ref,
                h_scr, gate_scr, acc_scr):
    e = pl.program_id(1)
    d = D_MODEL

    @pl.when(e == 0)
    def _():
        x = x_ref[...]
        y = x * lax.rsqrt(jnp.mean(x * x, axis=-1, keepdims=True) + EPS) * nw_ref[...]
        h = y * (1.0 + mod_ref[:, 4 * d:5 * d]) + mod_ref[:, 3 * d:4 * d]
        h_hi, h_lo = _split2(h)
        h_scr[...] = h_hi
        rwh = rwh_ref[...]
        logits = _dot(h_hi, rwh) + _dot(h_lo, rwh) + _dot(h_hi, rwl_ref[...])
        gate_scr[...] = _router_gates(logits, rb_ref[...])
        acc_scr[...] = jnp.zeros(acc_scr.shape, F32)

    hb = h_scr[...]
    hid = _silu(_dot(hb, w1_ref[...])) * _dot(hb, w3_ref[...])
    gate = gate_scr[...]
    lane = lax.broadcasted_iota(jnp.int32, gate.shape, 1)
    gcol = jnp.sum(jnp.where(lane == e, gate, 0.0), axis=-1, keepdims=True)
    acc_scr[...] += _dot((hid * gcol).astype(BF16), w2_ref[...])

    @pl.when(e == N_EXPERTS - 1)
    def _():
        o_ref[...] = x_ref[...] + mod_ref[:, 5 * d:6 * d] * acc_scr[...]


def _moe_call(x, mod3, nw, rwh, rwl, rb, w1, w3, w2, layer):
    tm = 1024
    d = D_MODEL
    full = lambda a: pl.BlockSpec(a.shape, lambda i, e: (0,) * a.ndim)
    return pl.pallas_call(
        _moe_kernel,
        grid=(N_TOK // tm, N_EXPERTS),
        in_specs=[pl.BlockSpec((tm, d), lambda i, e: (i, 0)),
                  pl.BlockSpec((None, 1, 6 * d), lambda i, e: (_mod_row(i, tm, layer), 0, 0)),
                  full(nw), full(rwh), full(rwl), full(rb),
                  pl.BlockSpec((None, d, D_EXPERT), lambda i, e: (e, 0, 0)),
                  pl.BlockSpec((None, d, D_EXPERT), lambda i, e: (e, 0, 0)),
                  pl.BlockSpec((None, D_EXPERT, d), lambda i, e: (e, 0, 0))],
        out_specs=pl.BlockSpec((tm, d), lambda i, e: (i, 0)),
        out_shape=jax.ShapeDtypeStruct((N_TOK, d), F32),
        scratch_shapes=[pltpu.VMEM((tm, d), BF16), pltpu.VMEM((tm, LANE), F32), pltpu.VMEM((tm, d), F32)],
        compiler_params=pltpu.CompilerParams(dimension_semantics=("parallel", "arbitrary")),
        name="moe_dense",
    )(x, mod3, nw, rwh, rwl, rb, w1, w3, w2)


def _pack_w_in(w):
    p = _col_slices(w)
    z = lambda n: jnp.zeros((D_MODEL, n), w.dtype)
    main = jnp.concatenate([
        p['a_cq'], p['a_ckv'], z(A_NOPE), p['a_krope'][:, _ROPE_PERM], z(U_QK - U_KR - A_NOPE - A_ROPE),
        p['b_q'], p['b_k'], p['b_v'], p['b_og'], p['c_z'], p['c_xbc'][:, :C_INNER], p['gates'],
        p['c_xbc'][:, C_INNER:]], axis=1)
    small = jnp.concatenate([p['b_gf'], p['b_gb'], p['c_dtf'], p['c_dtb'], p['a_krope'],
                             z(LANE - S_KR - A_ROPE)], axis=1)
    return main.astype(BF16), small.astype(BF16)


def _pad_heads(t, real):
    pad = jnp.zeros(t.shape[:-1] + (HEAD_PAD - real,), t.dtype)
    t = jnp.concatenate([t, pad], axis=-1)
    return t.reshape(t.shape[:-2] + (A_HEADS * HEAD_PAD,))


def _qk_gain(g, scale):
    g = jnp.concatenate([g[:A_NOPE], g[A_NOPE:][_ROPE_PERM], jnp.zeros((HEAD_PAD - A_QK,), g.dtype)])
    return (g * scale).reshape(1, HEAD_PAD)


def _lane_slot(vals, off):
    return jnp.zeros((1, LANE), F32).at[0, off:off + vals.shape[0]].set(vals.astype(F32))


def _rope_tables():
    t = jnp.arange(DEC_SEQ)
    row = (t // GRID_W).astype(F32)
    col = (t % GRID_W).astype(F32)
    n_freq = A_ROPE // 4
    inv_freq = 1.0 / (ROPE_BASE ** (jnp.arange(n_freq, dtype=F32) / n_freq))
    ang = jnp.concatenate([row[:, None] * inv_freq, col[:, None] * inv_freq], axis=-1)
    cos, sin = jnp.cos(ang), jnp.sin(ang)
    ones = jnp.ones((DEC_SEQ, A_NOPE), F32)
    tail = jnp.ones((DEC_SEQ, HEAD_PAD - A_QK), F32)
    ctab = jnp.concatenate([ones, cos, cos, tail], axis=-1)
    stab = jnp.concatenate([0.0 * ones, -sin, sin, 0.0 * tail], axis=-1)
    return ctab, stab


def kernel(x_prompt, x_sample, cache_ckv, cache_krope, state_gla, state_ssd, c, c_ctx, w_ada, b_ada, norm1, norm2, w_in, a_q_norm, a_wq, a_kv_norm, a_wkv, a_qk_qnorm, a_qk_knorm, a_wo, b_wg, b_bg, b_onorm, b_wo, c_conv_w, c_conv_b, c_dt_bias, c_A_log, c_D, c_norm, c_wo, w_out, router_w, router_bias, e_w1, e_w3, e_w2):
    d = D_MODEL
    x = jnp.concatenate([x_prompt.reshape(N_CTX, d), x_sample.reshape(N_DEC, d)], axis=0)

    cvecs = jnp.zeros((8, d), F32).at[0].set(c_ctx).at[1:1 + DEC_BATCH].set(c)
    mod3 = _mod_call(cvecs, w_ada, b_ada).reshape(DEPTH * 8, 1, 6 * d)

    rope_tabs = _rope_tables()
    gla_consts = (_gla_consts(False), _gla_consts(True))
    idx = np.arange(CHUNK)
    tri = (jnp.asarray(idx[None, :] <= idx[:, None], BF16), jnp.asarray(idx[None, :] >= idx[:, None], BF16))
    rw = jnp.zeros((d, LANE), F32).at[:, :N_EXPERTS].set(router_w)
    rwh, rwl = _split2(rw)
    rb = _lane_slot(router_bias, 0)

    ckvs, kropes, glas, ssds = [], [], [], []
    for l in range(DEPTH):
        w_main, w_small = _pack_w_in(w_in[l])
        u, s_all = _norm_proj_call(x, mod3, norm1[l].reshape(1, d), w_main, w_small, l)
        kropes.append(s_all[:N_CTX, S_KR:S_KR + A_ROPE].reshape(BATCH, SEQ, A_ROPE))

        wq = a_wq[l].reshape(A_QLORA, A_HEADS, A_QK)
        wq = _pad_heads(jnp.concatenate([wq[..., :A_NOPE], wq[..., A_NOPE:][..., _ROPE_PERM]], axis=-1), A_QK)
        wkv = a_wkv[l].reshape(A_KVLORA, A_HEADS, A_NOPE + A_V)
        aw = dict(qnw=a_q_norm[l].reshape(1, A_QLORA), wq=wq.astype(BF16),
                  gq=_qk_gain(a_qk_qnorm[l], A_QK ** -0.5),
                  kvnw=a_kv_norm[l].reshape(1, A_KVLORA),
                  wkn=_pad_heads(wkv[..., :A_NOPE], A_NOPE).astype(BF16),
                  wv=_pad_heads(wkv[..., A_NOPE:], A_V).astype(BF16),
                  gk=_qk_gain(a_qk_knorm[l], 1.0))
        q_c, k_c, v_c, ckvn = _attn_prep_call(N_CTX, 0, (u,), aw, has_q=True, norm_kv=True, rope=False, emit_ckv=True)
        o_c = _attn_call(q_c, k_c, v_c, None, n_seq=BATCH, seq_len=SEQ, hps=A_HEADS, tq=SEQ)
        ckvs.append(ckvn.reshape(BATCH, SEQ, A_KVLORA))
        q_d, k_d, v_d = _attn_prep_call(N_DEC, N_CTX, (u,), aw, has_q=True, norm_kv=True, rope=True, emit_ckv=False,
                                        tables=rope_tabs)
        kr_x = jnp.zeros((DEC_BATCH * PAST_LEN, LANE), F32).at[:, A_NOPE:A_QK].set(
            cache_krope[:, l].reshape(DEC_BATCH * PAST_LEN, A_ROPE)[:, _ROPE_PERM])
        k_x, v_x = _attn_prep_call(DEC_BATCH * PAST_LEN, 0, (cache_ckv[:, l].reshape(DEC_BATCH * PAST_LEN, A_KVLORA), kr_x),
                                   aw, has_q=False, norm_kv=False, rope=False, emit_ckv=False)
        o_d = _attn_call(q_d, k_d, v_d, (k_x, v_x), n_seq=DEC_BATCH, seq_len=DEC_SEQ, hps=2, tq=256)

        g_out = {}
        gla_l, ssd_l = [], []
        for bwd in (False, True):
            di = int(bwd)
            wg = jnp.zeros((LANE, B_HEADS * B_DK), F32).at[(S_GB if bwd else S_GF):(S_GB if bwd else S_GF) + B_GATE_RANK].set(b_wg[l, di])
            wgh, wgl = _split2(wg)
            bg = b_bg[l, di].reshape(1, B_HEADS * B_DK)
            o_gc, st_gc = _gla_call(u, s_all, wgh, wgl, bg, gla_consts[di], None,
                                    n_seq=BATCH, seq_len=SEQ, row0=0, bwd=bwd)
            st0 = jnp.swapaxes(state_gla[:, l, di], -1, -2)
            o_gd, _ = _gla_call(u, s_all, wgh, wgl, bg, gla_consts[di], st0,
                                n_seq=DEC_BATCH, seq_len=DEC_SEQ, row0=N_CTX, bwd=bwd)
            g_out[di] = (o_gc, o_gd)
            gla_l.append(jnp.swapaxes(st_gc, -1, -2))
        glas.append(jnp.stack(gla_l, axis=1))

        w8 = jnp.zeros((8, C_XBC), F32).at[:C_CONV].set(c_conv_w[l])
        cb = c_conv_b[l].reshape(1, C_XBC)
        xbc_c = _conv_call(u, w8, cb, n_seq=BATCH, seq_len=SEQ, row0=0)
        xbc_d = _conv_call(u, w8, cb, n_seq=DEC_BATCH, seq_len=DEC_SEQ, row0=N_CTX)
        y_out = {}
        for bwd in (False, True):
            di = int(bwd)
            off = S_DTB if bwd else S_DTF
            dtb = _lane_slot(c_dt_bias[l, di], off)
            aneg = _lane_slot(-jnp.exp(c_A_log[l, di].astype(F32)), off)
            y_c, st_c = _ssd_call(xbc_c, s_all, dtb, aneg, tri[di], None, n_seq=BATCH, seq_len=SEQ, row0=0, bwd=bwd)
            st0 = state_ssd[:, l, di].reshape(DEC_BATCH, C_HEADS // 2, LANE, C_STATE)
            y_d, _ = _ssd_call(xbc_d, s_all, dtb, aneg, tri[di], st0, n_seq=DEC_BATCH, seq_len=DEC_SEQ, row0=N_CTX, bwd=bwd)
            y_out[di] = (y_c, y_d)
            ssd_l.append(st_c.reshape(BATCH, C_HEADS, C_HEADDIM, C_STATE))
        ssds.append(jnp.stack(ssd_l, axis=1))

        awo = a_wo[l].reshape(A_HEADS, A_V, d)
        awo = jnp.concatenate([awo, jnp.zeros((A_HEADS, HEAD_PAD - A_V, d), awo.dtype)], axis=1)
        mw = dict(awo=awo.reshape(A_HEADS * HEAD_PAD, d).astype(BF16), onw=b_onorm[l].reshape(1, B_DV),
                  bwo=b_wo[l].astype(BF16), dexp=jnp.repeat(c_D[l], C_HEADDIM).reshape(1, C_INNER),
                  cnw=c_norm[l].reshape(1, C_INNER), cwo=c_wo[l].astype(BF16), wout=w_out[l].astype(BF16))
        x = _merge_call(x, mod3, u, o_c, g_out[0][0], g_out[1][0], y_out[0][0], y_out[1][0], xbc_c, mw,
                        n_rows=N_CTX, row0=0, layer=l)
        x = _merge_call(x, mod3, u, o_d, g_out[0][1], g_out[1][1], y_out[0][1], y_out[1][1], xbc_d, mw,
                        n_rows=N_DEC, row0=N_CTX, layer=l)

        x = _moe_call(x, mod3, norm2[l].reshape(1, d), rwh, rwl, rb,
                      e_w1[l].astype(BF16), e_w3[l].astype(BF16), e_w2[l].astype(BF16), l)

    y_prompt = x[:N_CTX].reshape(BATCH, SEQ, d)
    y_sample = x[N_CTX:].reshape(DEC_BATCH, DEC_SEQ, d)
    new_ckv = jnp.stack(ckvs, axis=1)
    new_krope = jnp.stack(kropes, axis=1)
    new_state_gla = jnp.stack(glas, axis=1)
    new_state_ssd = jnp.stack(ssds, axis=1)
    return (y_prompt, y_sample, new_ckv, new_krope, new_state_gla, new_state_ssd)
```

```python
import functools
import math

import numpy as np
import jax
import jax.numpy as jnp
from jax import lax
from jax.experimental import pallas as pl
from jax.experimental.pallas import tpu as pltpu

F32 = jnp.float32
BF16 = jnp.bfloat16

D_MODEL = 1024
BATCH = 16
SEQ = 256
DEPTH = 2
DEC_BATCH = 2
DEC_SEQ = 2048
PAST_LEN = 512
GRID_W = 64
EPS = 1e-6
ROPE_BASE = 10000.0

A_HEADS = 16
A_NOPE = 64
A_ROPE = 32
A_QK = A_NOPE + A_ROPE
A_V = 64
A_QLORA = 512
A_KVLORA = 256

B_HEADS = 4
B_DK = 128
B_DV = 256
B_GATE_RANK = 16
B_GATE_NORM = 16.0

C_HEADS = 16
C_HEADDIM = 64
C_INNER = C_HEADS * C_HEADDIM
C_GROUPS = 2
C_STATE = 128
C_XBC = C_INNER + 2 * C_GROUPS * C_STATE
C_CONV = 5

N_EXPERTS = 16
N_EXPERT_GROUPS = 4
D_EXPERT = 512

N_CTX = BATCH * SEQ
N_DEC = DEC_BATCH * DEC_SEQ
N_TOK = N_CTX + N_DEC

LANE = 128
HEAD_PAD = 128
CHUNK = 128

U_CQ = 0
U_CKV = 512
U_KR = 768
U_QK = 1024
U_V = 2048
U_OG = 3072
U_Z = 4096
U_X = 5120
U_GATES = 6144
U_BC = 9216
U_WIDTH = 9728
S_GF, S_GB, S_DTF, S_DTB, S_KR = 0, 16, 32, 48, 64

_IN_SPLITS = (
    ('a_cq', A_QLORA), ('a_ckv', A_KVLORA), ('a_krope', A_ROPE),
    ('b_q', B_HEADS * B_DK), ('b_k', B_HEADS * B_DK), ('b_v', B_HEADS * B_DV),
    ('b_og', B_HEADS * B_DV), ('b_gf', B_GATE_RANK), ('b_gb', B_GATE_RANK),
    ('c_z', C_INNER), ('c_xbc', C_XBC), ('c_dtf', C_HEADS), ('c_dtb', C_HEADS),
    ('gates', 3 * D_MODEL),
)
_ROPE_PERM = np.concatenate([np.arange(0, A_ROPE, 2), np.arange(1, A_ROPE, 2)])


def _col_slices(w):
    parts, start = {}, 0
    for name, size in _IN_SPLITS:
        parts[name] = w[:, start:start + size]
        start += size
    return parts


def _dot(a, b):
    return jnp.dot(a, b, preferred_element_type=F32)


def _dot_nt(a, b):
    return lax.dot_general(a, b, (((1,), (1,)), ((), ())), preferred_element_type=F32)


def _dot_tn(a, b):
    return lax.dot_general(a, b, (((0,), (0,)), ((), ())), preferred_element_type=F32)


def _split2(x):
    hi = x.astype(BF16)
    lo = (x - hi.astype(F32)).astype(BF16)
    return hi, lo


def _silu(x):
    return x * jax.nn.sigmoid(x)


def _softplus(x):
    return jnp.maximum(x, 0.0) + jnp.log1p(jnp.exp(-jnp.abs(x)))


def _log_sigmoid(x):
    return jnp.minimum(x, 0.0) - jnp.log1p(jnp.exp(-jnp.abs(x)))


def _mod_kernel(c_ref, w_ref, b_ref, o_ref):
    s = _silu(c_ref[...]).astype(BF16)
    o_ref[...] = _dot(s, w_ref[...].astype(BF16)) + b_ref[...]


def _mod_call(cvecs, w_ada, b_ada):
    tn = 1536
    return pl.pallas_call(
        _mod_kernel,
        grid=(DEPTH, 6 * D_MODEL // tn),
        in_specs=[pl.BlockSpec((8, D_MODEL), lambda l, j: (0, 0)),
                  pl.BlockSpec((None, D_MODEL, tn), lambda l, j: (l, 0, j)),
                  pl.BlockSpec((None, 1, tn), lambda l, j: (l, 0, j))],
        out_specs=pl.BlockSpec((None, 8, tn), lambda l, j: (l, 0, j)),
        out_shape=jax.ShapeDtypeStruct((DEPTH, 8, 6 * D_MODEL), F32),
        name="adaln_mod",
    )(cvecs, w_ada, b_ada.reshape(DEPTH, 1, 6 * D_MODEL))


def _mod_row(i, tm, layer):
    n_ctx = N_CTX // tm
    per_b = DEC_SEQ // tm
    return layer * 8 + jnp.where(i < n_ctx, 0, 1 + (i - n_ctx) // per_b)


def _norm_proj_kernel(x_ref, mod_ref, nw_ref, w_ref, ws_ref, u_ref, s_ref, h_scr):
    @pl.when(pl.program_id(1) == 0)
    def _():
        x = x_ref[...]
        m = mod_ref[...]
        y = x * lax.rsqrt(jnp.mean(x * x, axis=-1, keepdims=True) + EPS) * nw_ref[...]
        h = (y * (1.0 + m[:, D_MODEL:2 * D_MODEL]) + m[:, 0:D_MODEL]).astype(BF16)
        h_scr[...] = h
        s_ref[...] = _dot(h, ws_ref[...])

    u_ref[...] = _dot(h_scr[...], w_ref[...]).astype(BF16)


def _norm_proj_call(x, mod3, nw, w_main, w_small, layer):
    tm, tn = 1024, 512
    return pl.pallas_call(
        _norm_proj_kernel,
        grid=(N_TOK // tm, U_WIDTH // tn),
        in_specs=[pl.BlockSpec((tm, D_MODEL), lambda i, j: (i, 0)),
                  pl.BlockSpec((None, 1, 6 * D_MODEL), lambda i, j: (_mod_row(i, tm, layer), 0, 0)),
                  pl.BlockSpec((1, D_MODEL), lambda i, j: (0, 0)),
                  pl.BlockSpec((D_MODEL, tn), lambda i, j: (0, j)),
                  pl.BlockSpec((D_MODEL, LANE), lambda i, j: (0, 0))],
        out_specs=[pl.BlockSpec((tm, tn), lambda i, j: (i, j)),
                   pl.BlockSpec((tm, LANE), lambda i, j: (i, 0))],
        out_shape=[jax.ShapeDtypeStruct((N_TOK, U_WIDTH), BF16),
                   jax.ShapeDtypeStruct((N_TOK, LANE), F32)],
        scratch_shapes=[pltpu.VMEM((tm, D_MODEL), BF16)],
        compiler_params=pltpu.CompilerParams(dimension_semantics=("parallel", "arbitrary")),
        name="norm_in_proj",
    )(x, mod3, nw, w_main, w_small)


def _head_norm_rope(t, gain, cos, sin, lane):
    ss = jnp.sum(t * t, axis=-1, keepdims=True)
    t = t * lax.rsqrt(ss * (1.0 / A_QK) + EPS) * gain
    if cos is not None:
        half = A_ROPE // 2
        swapped = jnp.where(lane < A_NOPE + half, pltpu.roll(t, HEAD_PAD - half, 1), pltpu.roll(t, half, 1))
        t = t * cos + swapped * sin
    return t


def _attn_prep_kernel(*refs, has_q, norm_kv, rope, emit_ckv):
    refs = list(refs)
    cq_ref = refs.pop(0) if has_q else None
    ckv_ref, kr_ref = refs.pop(0), refs.pop(0)
    if has_q:
        qnw_ref, wq_ref, gq_ref = refs.pop(0), refs.pop(0), refs.pop(0)
    kvnw_ref, wkn_ref, wv_ref, gk_ref = refs.pop(0), refs.pop(0), refs.pop(0), refs.pop(0)
    cos = sin = None
    if rope:
        cos, sin = refs.pop(0)[...], refs.pop(0)[...]
    q_ref = refs.pop(0) if has_q else None
    k_ref, v_ref = refs.pop(0), refs.pop(0)
    ckvn_ref = refs.pop(0) if emit_ckv else None

    tm = ckv_ref.shape[0]
    lane = lax.broadcasted_iota(jnp.int32, (tm, HEAD_PAD), 1)

    ckv = ckv_ref[...].astype(F32)
    if norm_kv:
        ckv = ckv * lax.rsqrt(jnp.mean(ckv * ckv, axis=-1, keepdims=True) + EPS) * kvnw_ref[...]
    if emit_ckv:
        ckvn_ref[...] = ckv
    ckv_b = ckv.astype(BF16)
    v_ref[...] = _dot(ckv_b, wv_ref[...]).astype(BF16)
    kf = _dot(ckv_b, wkn_ref[...])
    kr = kr_ref[...].astype(F32)
    gk = gk_ref[...]
    for h in range(A_HEADS):
        sl = slice(h * HEAD_PAD, (h + 1) * HEAD_PAD)
        k_ref[:, sl] = _head_norm_rope(kf[:, sl] + kr, gk, cos, sin, lane).astype(BF16)

    if has_q:
        cq = cq_ref[...].astype(F32)
        cq = cq * lax.rsqrt(jnp.mean(cq * cq, axis=-1, keepdims=True) + EPS) * qnw_ref[...]
        qf = _dot(cq.astype(BF16), wq_ref[...])
        gq = gq_ref[...]
        for h in range(A_HEADS):
            sl = slice(h * HEAD_PAD, (h + 1) * HEAD_PAD)
            q_ref[:, sl] = _head_norm_rope(qf[:, sl], gq, cos, sin, lane).astype(BF16)


def _attn_prep_call(n_rows, row0, srcs, wts, *, has_q, norm_kv, rope, emit_ckv, tables=None):
    tm = 256
    r0 = row0 // tm
    pw = A_HEADS * HEAD_PAD
    full = lambda shape: pl.BlockSpec(shape, lambda i: (0,) * len(shape))
    args, in_specs = [], []
    if len(srcs) == 1:
        u = srcs[0]
        if has_q:
            args.append(u)
            in_specs.append(pl.BlockSpec((tm, A_QLORA), lambda i: (i + r0, U_CQ // A_QLORA)))
        args += [u, u]
        in_specs += [pl.BlockSpec((tm, A_KVLORA), lambda i: (i + r0, U_CKV // A_KVLORA)),
                     pl.BlockSpec((tm, LANE), lambda i: (i + r0, U_KR // LANE))]
    else:
        args += list(srcs)
        in_specs += [pl.BlockSpec((tm, A_KVLORA), lambda i: (i, 0)),
                     pl.BlockSpec((tm, LANE), lambda i: (i, 0))]
    if has_q:
        args += [wts['qnw'], wts['wq'], wts['gq']]
        in_specs += [full((1, A_QLORA)), full((A_QLORA, pw)), full((1, HEAD_PAD))]
    args += [wts['kvnw'], wts['wkn'], wts['wv'], wts['gk']]
    in_specs += [full((1, A_KVLORA)), full((A_KVLORA, pw)), full((A_KVLORA, pw)), full((1, HEAD_PAD))]
    if rope:
        per_seq = DEC_SEQ // tm
        args += list(tables)
        in_specs += [pl.BlockSpec((tm, HEAD_PAD), lambda i: (i % per_seq, 0))] * 2
    out_specs, out_shape = [], []
    n_out = (1 if has_q else 0) + 2
    for _ in range(n_out):
        out_specs.append(pl.BlockSpec((tm, pw), lambda i: (i, 0)))
        out_shape.append(jax.ShapeDtypeStruct((n_rows, pw), BF16))
    if emit_ckv:
        out_specs.append(pl.BlockSpec((tm, A_KVLORA), lambda i: (i, 0)))
        out_shape.append(jax.ShapeDtypeStruct((n_rows, A_KVLORA), F32))
    return pl.pallas_call(
        functools.partial(_attn_prep_kernel, has_q=has_q, norm_kv=norm_kv, rope=rope, emit_ckv=emit_ckv),
        grid=(n_rows // tm,),
        in_specs=in_specs, out_specs=out_specs, out_shape=out_shape,
        compiler_params=pltpu.CompilerParams(dimension_semantics=("parallel",)),
        name="mla_prep",
    )(*args)


def _attn_kernel(*refs, hps, has_ctx):
    if has_ctx:
        q_ref, k_ref, v_ref, kc_ref, vc_ref, o_ref = refs
    else:
        q_ref, k_ref, v_ref, o_ref = refs
    for hh in range(hps):
        sl = slice(hh * HEAD_PAD, (hh + 1) * HEAD_PAD)
        q = q_ref[:, sl]
        s = _dot_nt(q, k_ref[:, sl])
        m = jnp.max(s, axis=-1, keepdims=True)
        if has_ctx:
            s2 = _dot_nt(q, kc_ref[:, sl])
            m = jnp.maximum(m, jnp.max(s2, axis=-1, keepdims=True))
        p = jnp.exp(s - m)
        den = jnp.sum(p, axis=-1, keepdims=True)
        o = _dot(p.astype(BF16), v_ref[:, sl])
        if has_ctx:
            p2 = jnp.exp(s2 - m)
            den = den + jnp.sum(p2, axis=-1, keepdims=True)
            o = o + _dot(p2.astype(BF16), vc_ref[:, sl])
        o_ref[:, sl] = (o / den).astype(BF16)


def _attn_call(q, k, v, ctx_kv, *, n_seq, seq_len, hps, tq):
    pw = A_HEADS * HEAD_PAD
    bw = hps * HEAD_PAD
    nq = seq_len // tq
    in_specs = [pl.BlockSpec((tq, bw), lambda b, h, i: (b * nq + i, h)),
                pl.BlockSpec((seq_len, bw), lambda b, h, i: (b, h)),
                pl.BlockSpec((seq_len, bw), lambda b, h, i: (b, h))]
    args = [q, k, v]
    if ctx_kv is not None:
        in_specs += [pl.BlockSpec((PAST_LEN, bw), lambda b, h, i: (b, h))] * 2
        args += list(ctx_kv)
    return pl.pallas_call(
        functools.partial(_attn_kernel, hps=hps, has_ctx=ctx_kv is not None),
        grid=(n_seq, A_HEADS // hps, nq),
        in_specs=in_specs,
        out_specs=pl.BlockSpec((tq, bw), lambda b, h, i: (b * nq + i, h)),
        out_shape=jax.ShapeDtypeStruct((n_seq * seq_len, pw), BF16),
        compiler_params=pltpu.CompilerParams(dimension_semantics=("parallel", "parallel", "arbitrary")),
        name="mla_attention",
    )(*args)


def _conv_kernel(x_ref, w_ref, b_ref, o_ref):
    x = x_ref[...].astype(F32)
    n = x.shape[0]
    row = lax.broadcasted_iota(jnp.int32, x.shape, 0)
    half = (C_CONV - 1) // 2
    acc = x * w_ref[half:half + 1, :] + b_ref[...]
    for d in range(-half, half + 1):
        if d == 0:
            continue
        shifted = pltpu.roll(x, (-d) % n, 0)
        valid = jnp.logical_and(row + d >= 0, row + d < n)
        acc = acc + jnp.where(valid, shifted, 0.0) * w_ref[half + d:half + d + 1, :]
    o_ref[...] = _silu(acc).astype(BF16)


def _conv_call(u, w8, b, *, n_seq, seq_len, row0):
    tc = 512
    b0 = row0 // seq_len
    nx = C_INNER // tc
    ucol = lambda j: jnp.where(j < nx, U_X // tc + j, U_BC // tc + j - nx)
    return pl.pallas_call(
        _conv_kernel,
        grid=(n_seq, C_XBC // tc),
        in_specs=[pl.BlockSpec((seq_len, tc), lambda s, j: (s + b0, ucol(j))),
                  pl.BlockSpec((8, tc), lambda s, j: (0, j)),
                  pl.BlockSpec((1, tc), lambda s, j: (0, j))],
        out_specs=pl.BlockSpec((seq_len, tc), lambda s, j: (s, j)),
        out_shape=jax.ShapeDtypeStruct((n_seq * seq_len, C_XBC), BF16),
        compiler_params=pltpu.CompilerParams(dimension_semantics=("parallel", "parallel")),
        name="ssd_conv",
    )(u, w8, b)


def _ssd_consts():
    idx = np.arange(CHUNK)
    tri = np.stack([idx[None, :] <= idx[:, None], idx[None, :] >= idx[:, None]]).astype(np.float32)
    expand = np.zeros((2, LANE, C_INNER), np.float32)
    tile = np.zeros((2, LANE, C_HEADS * LANE), np.float32)
    for di, off in enumerate((S_DTF, S_DTB)):
        for h in range(C_HEADS):
            expand[di, off + h, h * C_HEADDIM:(h + 1) * C_HEADDIM] = 1.0
            tile[di, off + h, h * LANE:(h + 1) * LANE] = 1.0
    return jnp.asarray(tri, BF16), jnp.asarray(expand, BF16), jnp.asarray(tile, BF16)


def _ssd_kernel(*refs, has_init):
    refs = list(refs)
    xs = [(refs.pop(0), refs.pop(0), refs.pop(0)) for _ in range(2)]
    dtb_ref, an_ref, tri_ref, exp_ref, tile_ref = (refs.pop(0) for _ in range(5))
    st0_ref = refs.pop(0) if has_init else None
    y_refs = (refs.pop(0), refs.pop(0))
    stf_ref, st_scr = refs

    @pl.when(pl.program_id(1) == 0)
    def _():
        st_scr[...] = st0_ref[...] if has_init else jnp.zeros(st_scr.shape, F32)

    t = CHUNK
    ri = lax.broadcasted_iota(jnp.int32, (t, t), 0)
    ci = lax.broadcasted_iota(jnp.int32, (t, t), 1)
    prow = lax.broadcasted_iota(jnp.int32, (LANE, C_STATE), 0)
    gs = C_GROUPS * C_STATE
    pairs_per_group = C_HEADS // C_GROUPS // 2
    for di in range(2):
        x_ref, bc_ref, s_ref = xs[di]
        y_ref = y_refs[di]
        off = (S_DTF, S_DTB)[di]
        dt = _softplus(s_ref[...] + dtb_ref[di])
        a = dt * an_ref[di]
        a_hi, a_lo = _split2(a)
        tri = tri_ref[di]
        acum = _dot(tri, a_hi) + _dot(tri, a_lo)
        acum_t = acum.T
        dt_t = dt.T
        edge = 0 if di else t - 1
        atot = acum[edge:edge + 1, :]
        causal = (ci >= ri) if di else (ci <= ri)
        dec_all = jnp.exp(atot)
        e_exp = _dot(jnp.exp(acum).astype(BF16), exp_ref[di])
        wj_exp = _dot((jnp.exp(atot - acum) * dt).astype(BF16), exp_ref[di])
        c_hi, c_lo = _split2(acum)
        acol = _dot(c_hi, tile_ref[di]) + _dot(c_lo, tile_ref[di])
        bc = bc_ref[...]
        cb, bmat, cmat = [], [], []
        for g in range(C_GROUPS):
            bmat.append(bc[:, g * C_STATE:(g + 1) * C_STATE])
            cmat.append(bc[:, gs + g * C_STATE:gs + (g + 1) * C_STATE])
            cb.append(_dot_nt(cmat[g], bmat[g]))
        lane = lax.broadcasted_iota(jnp.int32, (t, LANE), 1)
        low = lane < C_HEADDIM
        for m in range(C_HEADS // 2):
            g = m // pairs_per_group
            ps = slice(m * LANE, (m + 1) * LANE)
            xp = x_ref[:, ps]
            y_diag = []
            for hh in range(2):
                h = 2 * m + hh
                la = off + h
                lm = jnp.exp(jnp.where(causal, acol[:, h * LANE:(h + 1) * LANE] - acum_t[la:la + 1, :], -jnp.inf))
                w = (cb[g] * lm * dt_t[la:la + 1, :]).astype(BF16)
                y_diag.append(_dot(w, xp))
            st = st_scr[di, m]
            y_off = _dot_nt(cmat[g], st.astype(BF16)) * e_exp[:, ps]
            y_ref[:, ps] = (jnp.where(low, y_diag[0], y_diag[1]) + y_off).astype(BF16)
            xw = (xp.astype(F32) * wj_exp[:, ps]).astype(BF16)
            la = off + 2 * m
            dec = jnp.where(prow < C_HEADDIM, dec_all[:, la:la + 1], dec_all[:, la + 1:la + 2])
            st_scr[di, m] = dec * st + _dot_tn(xw, bmat[g])

    @pl.when(pl.program_id(1) == pl.num_programs(1) - 1)
    def _():
        stf_ref[...] = st_scr[...]


def _ssd_call(xbc, s_all, dtb, aneg, consts, st0, *, n_seq, seq_len, row0):
    t = CHUNK
    nch = seq_len // t
    r0 = row0 // t
    npair = C_HEADS // 2
    bcw = 2 * C_GROUPS * C_STATE
    tri, expand, tile = consts
    full = lambda a: pl.BlockSpec(a.shape, lambda b, c: (0,) * a.ndim)
    in_specs, args = [], []
    for cidx in (lambda c: c, lambda c: nch - 1 - c):
        in_specs += [pl.BlockSpec((t, C_INNER), lambda b, c, cidx=cidx: (b * nch + cidx(c), 0)),
                     pl.BlockSpec((t, bcw), lambda b, c, cidx=cidx: (b * nch + cidx(c), C_INNER // bcw)),
                     pl.BlockSpec((t, LANE), lambda b, c, cidx=cidx: (r0 + b * nch + cidx(c), 0))]
        args += [xbc, xbc, s_all]
    in_specs += [full(dtb), full(aneg), full(tri), full(expand), full(tile)]
    args += [dtb, aneg, tri, expand, tile]
    st_spec = pl.BlockSpec((None, 2, npair, LANE, C_STATE), lambda b, c: (b, 0, 0, 0, 0))
    if st0 is not None:
        in_specs.append(st_spec)
        args.append(st0)
    return pl.pallas_call(
        functools.partial(_ssd_kernel, has_init=st0 is not None),
        grid=(n_seq, nch),
        in_specs=in_specs,
        out_specs=[pl.BlockSpec((t, C_INNER), lambda b, c: (b * nch + c, 0)),
                   pl.BlockSpec((t, C_INNER), lambda b, c: (b * nch + nch - 1 - c, 0)),
                   st_spec],
        out_shape=[jax.ShapeDtypeStruct((n_seq * seq_len, C_INNER), BF16),
                   jax.ShapeDtypeStruct((n_seq * seq_len, C_INNER), BF16),
                   jax.ShapeDtypeStruct((n_seq, 2, npair, LANE, C_STATE), F32)],
        scratch_shapes=[pltpu.VMEM((2, npair, LANE, C_STATE), F32)],
        compiler_params=pltpu.CompilerParams(dimension_semantics=("parallel", "arbitrary")),
        name="ssd_scan",
    )(*args)


def _gla_consts(bwd):
    t = CHUNK
    nlev = int(math.log2(t))
    idx = np.arange(t)
    mats = np.zeros((nlev + 2, t, t), np.float32)
    masks = np.zeros((nlev + 1, t, t), np.float32)
    masks[0] = np.eye(t)
    for lvl in range(nlev):
        s = 1 << lvl
        blk = idx // (2 * s)
        upper = (idx % (2 * s)) >= s
        last_low = blk * 2 * s + s - 1
        rowtok = ~upper if bwd else upper
        for i in range(t):
            r = last_low[i]
            if not bwd:
                if upper[i]:
                    mats[lvl, i, r + 1:i + 1] = 1.0
                else:
                    mats[lvl, i, i + 1:r + 1] = 1.0
            else:
                if upper[i]:
                    mats[lvl, i, r + 1:i] = 1.0
                else:
                    mats[lvl, i, i:r + 1] = 1.0
        masks[lvl + 1] = ((blk[:, None] == blk[None, :]) & rowtok[:, None] & (~rowtok)[None, :])
    incl = (idx[None, :] >= idx[:, None]) if bwd else (idx[None, :] <= idx[:, None])
    mats[nlev] = incl
    mats[nlev + 1] = 1.0 - incl
    return mats.reshape((nlev + 2) * t, t), masks


def _gla_kernel(*refs, has_init):
    refs = list(refs)
    xs = [(refs.pop(0), refs.pop(0), refs.pop(0)) for _ in range(2)]
    wgh_ref, wgl_ref, bg_ref, mall_ref, mask_ref = (refs.pop(0) for _ in range(5))
    st0_ref = refs.pop(0) if has_init else None
    o_refs = (refs.pop(0), refs.pop(0))
    stf_ref, st_scr = refs

    @pl.when(pl.program_id(1) == 0)
    def _():
        st_scr[...] = st0_ref[...] if has_init else jnp.zeros(st_scr.shape, F32)

    t = CHUNK
    nlev = mask_ref.shape[1] - 1
    hk = B_HEADS * B_DK
    for di in range(2):
        qk_ref, v_ref, s_ref = xs[di]
        o_ref = o_refs[di]
        s_hi, s_lo = _split2(s_ref[...])
        wgh = wgh_ref[di]
        logit = _dot(s_hi, wgh) + _dot(s_lo, wgh) + _dot(s_hi, wgl_ref[di]) + bg_ref[di]
        g_all = _log_sigmoid(logit) * (1.0 / B_GATE_NORM)
        e_all = jnp.exp(_dot(mall_ref[di], g_all.astype(BF16)))
        edge = 0 if di else t - 1
        for h in range(B_HEADS):
            ks = slice(h * B_DK, (h + 1) * B_DK)
            vs = slice(h * B_DV, (h + 1) * B_DV)
            q = qk_ref[:, ks].astype(F32) * (B_DK ** -0.5)
            k = qk_ref[:, hk + h * B_DK:hk + (h + 1) * B_DK]
            kf = k.astype(F32)
            v = v_ref[:, vs]
            amat = _dot_nt(q.astype(BF16), k) * mask_ref[di, 0]
            for lvl in range(nlev):
                e = e_all[lvl * t:(lvl + 1) * t, ks]
                amat = amat + _dot_nt((q * e).astype(BF16), (kf * e).astype(BF16)) * mask_ref[di, lvl + 1]
            e_in = e_all[nlev * t:(nlev + 1) * t, ks]
            e_out = e_all[(nlev + 1) * t:(nlev + 2) * t, ks]
            st = st_scr[di, h]
            o = _dot(amat.astype(BF16), v) + _dot_nt((q * e_in).astype(BF16), st.astype(BF16))
            o_ref[:, vs] = o.astype(BF16)
            st_scr[di, h] = st * e_in[edge:edge + 1, :] + _dot_tn(v, (kf * e_out).astype(BF16))

    @pl.when(pl.program_id(1) == pl.num_programs(1) - 1)
    def _():
        stf_ref[...] = st_scr[...]


def _gla_call(u, s_all, wgh, wgl, bg, mall, masks, st0, *, n_seq, seq_len, row0):
    t = CHUNK
    nch = seq_len // t
    r0 = row0 // t
    hk, hv = B_HEADS * B_DK, B_HEADS * B_DV
    full = lambda a: pl.BlockSpec(a.shape, lambda b, c: (0,) * a.ndim)
    in_specs, args = [], []
    for cidx in (lambda c: c, lambda c: nch - 1 - c):
        in_specs += [pl.BlockSpec((t, 2 * hk), lambda b, c, cidx=cidx: (r0 + b * nch + cidx(c), U_QK // (2 * hk))),
                     pl.BlockSpec((t, hv), lambda b, c, cidx=cidx: (r0 + b * nch + cidx(c), U_V // hv)),
                     pl.BlockSpec((t, LANE), lambda b, c, cidx=cidx: (r0 + b * nch + cidx(c), 0))]
        args += [u, u, s_all]
    in_specs += [full(wgh), full(wgl), full(bg), full(mall), full(masks)]
    args += [wgh, wgl, bg, mall, masks]
    st_spec = pl.BlockSpec((None, 2, B_HEADS, B_DV, B_DK), lambda b, c: (b, 0, 0, 0, 0))
    if st0 is not None:
        in_specs.append(st_spec)
        args.append(st0)
    return pl.pallas_call(
        functools.partial(_gla_kernel, has_init=st0 is not None),
        grid=(n_seq, nch),
        in_specs=in_specs,
        out_specs=[pl.BlockSpec((t, hv), lambda b, c: (b * nch + c, 0)),
                   pl.BlockSpec((t, hv), lambda b, c: (b * nch + nch - 1 - c, 0)),
                   st_spec],
        out_shape=[jax.ShapeDtypeStruct((n_seq * seq_len, hv), BF16),
                   jax.ShapeDtypeStruct((n_seq * seq_len, hv), BF16),
                   jax.ShapeDtypeStruct((n_seq, 2, B_HEADS, B_DV, B_DK), F32)],
        scratch_shapes=[pltpu.VMEM((2, B_HEADS, B_DV, B_DK), F32)],
        compiler_params=pltpu.CompilerParams(dimension_semantics=("parallel", "arbitrary")),
        name="gla_scan",
    )(*args)


def _merge_kernel(x_ref, mod_ref, oa_ref, gf_ref, gb_ref, og_ref, yf_ref, yb_ref, xc_ref, z_ref, gt_ref,
                  awo_ref, onw_ref, bwo_ref, dexp_ref, cnw_ref, cwo_ref, wout_ref, o_ref):
    o_a = _dot(oa_ref[...], awo_ref[...])

    og = gf_ref[...].astype(F32) + gb_ref[...].astype(F32)
    onw = onw_ref[...]
    parts = []
    for h in range(B_HEADS):
        th = og[:, h * B_DV:(h + 1) * B_DV]
        parts.append(th * lax.rsqrt(jnp.mean(th * th, axis=-1, keepdims=True) + EPS) * onw)
    ob_in = jnp.concatenate(parts, axis=-1) * _silu(og_ref[...].astype(F32))
    o_b = _dot(ob_in.astype(BF16), bwo_ref[...])

    y = yf_ref[...].astype(F32) + yb_ref[...].astype(F32) + dexp_ref[...] * xc_ref[...].astype(F32)
    y = y * _silu(z_ref[...].astype(F32))
    y = y * lax.rsqrt(jnp.mean(y * y, axis=-1, keepdims=True) + EPS) * cnw_ref[...]
    o_c = _dot(y.astype(BF16), cwo_ref[...])

    d = D_MODEL
    merged = (jax.nn.sigmoid(gt_ref[:, 0:d].astype(F32)) * o_a
              + jax.nn.sigmoid(gt_ref[:, d:2 * d].astype(F32)) * o_b
              + jax.nn.sigmoid(gt_ref[:, 2 * d:3 * d].astype(F32)) * o_c)
    out = _dot(merged.astype(BF16), wout_ref[...])
    o_ref[...] = x_ref[...] + mod_ref[:, 2 * d:3 * d] * out


def _merge_call(x, mod3, u, o_attn, g_f, g_b, y_f, y_b, xbc, wts, *, n_rows, row0, layer):
    tm = 256
    r0 = row0 // tm
    d = D_MODEL
    pw = A_HEADS * HEAD_PAD
    loc = lambda w: pl.BlockSpec((tm, w), lambda i: (i, 0))
    full = lambda a: pl.BlockSpec(a.shape, lambda i: (0,) * a.ndim)
    in_specs = [pl.BlockSpec((tm, d), lambda i: (i + r0, 0)),
                pl.BlockSpec((None, 1, 6 * d), lambda i: (_mod_row(i + r0, tm, layer), 0, 0)),
                loc(pw), loc(d), loc(d),
                pl.BlockSpec((tm, d), lambda i: (i + r0, U_OG // d)),
                loc(d), loc(d),
                pl.BlockSpec((tm, d), lambda i: (i, 0)),
                pl.BlockSpec((tm, d), lambda i: (i + r0, U_Z // d)),
                pl.BlockSpec((tm, 3 * d), lambda i: (i + r0, U_GATES // (3 * d)))]
    w_args = [wts[n] for n in ('awo', 'onw', 'bwo', 'dexp', 'cnw', 'cwo', 'wout')]
    in_specs += [full(a) for a in w_args]
    return pl.pallas_call(
        _merge_kernel,
        grid=(n_rows // tm,),
        in_specs=in_specs,
        out_specs=pl.BlockSpec((tm, d), lambda i: (i + r0, 0)),
        out_shape=jax.ShapeDtypeStruct((N_TOK, d), F32),
        input_output_aliases={0: 0},
        compiler_params=pltpu.CompilerParams(dimension_semantics=("parallel",)),
        name="mixer_merge",
    )(x, mod3, o_attn, g_f, g_b, u, y_f, y_b, xbc, u, u, *w_args)


def _router_gates(logits, bias):
    lane = lax.broadcasted_iota(jnp.int32, logits.shape, 1)
    neg = -jnp.inf
    per_group = N_EXPERTS // N_EXPERT_GROUPS
    scores = jax.nn.sigmoid(logits)
    sel = jnp.where(lane < N_EXPERTS, scores + bias, neg)

    def top2(v):
        m1 = jnp.max(v, axis=-1, keepdims=True)
        i1 = jnp.min(jnp.where(v == m1, lane, LANE), axis=-1, keepdims=True)
        v2 = jnp.where(lane == i1, neg, v)
        m2 = jnp.max(v2, axis=-1, keepdims=True)
        i2 = jnp.min(jnp.where(v2 == m2, lane, LANE), axis=-1, keepdims=True)
        return m1, i1, m2, i2

    best_v = best_g = None
    for g in range(N_EXPERT_GROUPS):
        in_g = jnp.logical_and(lane >= g * per_group, lane < (g + 1) * per_group)
        m1, _, m2, _ = top2(jnp.where(in_g, sel, neg))
        gs = m1 + m2
        if g == 0:
            best_v, best_g = gs, jnp.zeros_like(gs, dtype=jnp.int32)
        else:
            upd = gs > best_v
            best_g = jnp.where(upd, g, best_g)
            best_v = jnp.where(upd, gs, best_v)
    in_best = jnp.logical_and(lane // per_group == best_g, lane < N_EXPERTS)
    _, i1, _, i2 = top2(jnp.where(in_best, sel, neg))
    w1 = jnp.sum(jnp.where(lane == i1, scores, 0.0), axis=-1, keepdims=True)
    w2 = jnp.sum(jnp.where(lane == i2, scores, 0.0), axis=-1, keepdims=True)
    tot = w1 + w2
    return jnp.where(lane == i1, w1 / tot, 0.0) + jnp.where(lane == i2, w2 / tot, 0.0)


def _moe_kernel(x_ref, mod_ref, nw_ref, rwh_ref, rwl_ref, rb_ref, w1_ref, w3_ref, w2_ref, o_ref,
                h_scr, gate_scr, acc_scr):
    e = pl.program_id(1)
    d = D_MODEL

    @pl.when(e == 0)
    def _():
        x = x_ref[...]
        y = x * lax.rsqrt(jnp.mean(x * x, axis=-1, keepdims=True) + EPS) * nw_ref[...]
        h = y * (1.0 + mod_ref[:, 4 * d:5 * d]) + mod_ref[:, 3 * d:4 * d]
        h_hi, h_lo = _split2(h)
        h_scr[...] = h_hi
        rwh = rwh_ref[...]
        logits = _dot(h_hi, rwh) + _dot(h_lo, rwh) + _dot(h_hi, rwl_ref[...])
        gate_scr[...] = _router_gates(logits, rb_ref[...])
        acc_scr[...] = jnp.zeros(acc_scr.shape, F32)

    hb = h_scr[...]
    hid = _silu(_dot(hb, w1_ref[...].astype(BF16))) * _dot(hb, w3_ref[...].astype(BF16))
    gate = gate_scr[...]
    lane = lax.broadcasted_iota(jnp.int32, gate.shape, 1)
    gcol = jnp.sum(jnp.where(lane == e, gate, 0.0), axis=-1, keepdims=True)
    acc_scr[...] += _dot((hid * gcol).astype(BF16), w2_ref[...].astype(BF16))

    @pl.when(e == N_EXPERTS - 1)
    def _():
        o_ref[...] = x_ref[...] + mod_ref[:, 5 * d:6 * d] * acc_scr[...]


def _moe_call(x, mod3, nw, rwh, rwl, rb, w1, w3, w2, layer):
    tm = 1024
    d = D_MODEL
    full = lambda a: pl.BlockSpec(a.shape, lambda i, e: (0,) * a.ndim)
    return pl.pallas_call(
        _moe_kernel,
        grid=(N_TOK // tm, N_EXPERTS),
        in_specs=[pl.BlockSpec((tm, d), lambda i, e: (i, 0)),
                  pl.BlockSpec((None, 1, 6 * d), lambda i, e: (_mod_row(i, tm, layer), 0, 0)),
                  full(nw), full(rwh), full(rwl), full(rb),
                  pl.BlockSpec((None, None, d, D_EXPERT), lambda i, e: (layer, e, 0, 0)),
                  pl.BlockSpec((None, None, d, D_EXPERT), lambda i, e: (layer, e, 0, 0)),
                  pl.BlockSpec((None, None, D_EXPERT, d), lambda i, e: (layer, e, 0, 0))],
        out_specs=pl.BlockSpec((tm, d), lambda i, e: (i, 0)),
        out_shape=jax.ShapeDtypeStruct((N_TOK, d), F32),
        scratch_shapes=[pltpu.VMEM((tm, d), BF16), pltpu.VMEM((tm, LANE), F32), pltpu.VMEM((tm, d), F32)],
        compiler_params=pltpu.CompilerParams(dimension_semantics=("parallel", "arbitrary")),
        name="moe_dense",
    )(x, mod3, nw, rwh, rwl, rb, w1, w3, w2)


def _pack_w_in(w):
    p = _col_slices(w)
    z = lambda n: jnp.zeros((D_MODEL, n), w.dtype)
    main = jnp.concatenate([
        p['a_cq'], p['a_ckv'], z(A_NOPE), p['a_krope'][:, _ROPE_PERM], z(U_QK - U_KR - A_NOPE - A_ROPE),
        p['b_q'], p['b_k'], p['b_v'], p['b_og'], p['c_z'], p['c_xbc'][:, :C_INNER], p['gates'],
        p['c_xbc'][:, C_INNER:]], axis=1)
    small = jnp.concatenate([p['b_gf'], p['b_gb'], p['c_dtf'], p['c_dtb'], p['a_krope'],
                             z(LANE - S_KR - A_ROPE)], axis=1)
    return main.astype(BF16), small.astype(BF16)


def _pad_heads(t, real):
    pad = jnp.zeros(t.shape[:-1] + (HEAD_PAD - real,), t.dtype)
    t = jnp.concatenate([t, pad], axis=-1)
    return t.reshape(t.shape[:-2] + (A_HEADS * HEAD_PAD,))


def _qk_gain(g, scale):
    g = jnp.concatenate([g[:A_NOPE], g[A_NOPE:][_ROPE_PERM], jnp.zeros((HEAD_PAD - A_QK,), g.dtype)])
    return (g * scale).reshape(1, HEAD_PAD)


def _lane_slot(vals, off):
    return jnp.zeros((1, LANE), F32).at[0, off:off + vals.shape[0]].set(vals.astype(F32))


def _rope_tables():
    t = jnp.arange(DEC_SEQ)
    row = (t // GRID_W).astype(F32)
    col = (t % GRID_W).astype(F32)
    n_freq = A_ROPE // 4
    inv_freq = 1.0 / (ROPE_BASE ** (jnp.arange(n_freq, dtype=F32) / n_freq))
    ang = jnp.concatenate([row[:, None] * inv_freq, col[:, None] * inv_freq], axis=-1)
    cos, sin = jnp.cos(ang), jnp.sin(ang)
    ones = jnp.ones((DEC_SEQ, A_NOPE), F32)
    tail = jnp.ones((DEC_SEQ, HEAD_PAD - A_QK), F32)
    ctab = jnp.concatenate([ones, cos, cos, tail], axis=-1)
    stab = jnp.concatenate([0.0 * ones, -sin, sin, 0.0 * tail], axis=-1)
    return ctab, stab


def kernel(x_prompt, x_sample, cache_ckv, cache_krope, state_gla, state_ssd, c, c_ctx, w_ada, b_ada, norm1, norm2, w_in, a_q_norm, a_wq, a_kv_norm, a_wkv, a_qk_qnorm, a_qk_knorm, a_wo, b_wg, b_bg, b_onorm, b_wo, c_conv_w, c_conv_b, c_dt_bias, c_A_log, c_D, c_norm, c_wo, w_out, router_w, router_bias, e_w1, e_w3, e_w2):
    d = D_MODEL
    x = jnp.concatenate([x_prompt.reshape(N_CTX, d), x_sample.reshape(N_DEC, d)], axis=0)

    cvecs = jnp.zeros((8, d), F32).at[0].set(c_ctx).at[1:1 + DEC_BATCH].set(c)
    mod3 = _mod_call(cvecs, w_ada, b_ada).reshape(DEPTH * 8, 1, 6 * d)

    rope_tabs = _rope_tables()
    gla_np = (_gla_consts(False), _gla_consts(True))
    gla_mall = jnp.asarray(np.stack([gla_np[0][0], gla_np[1][0]]), BF16)
    gla_masks = jnp.asarray(np.stack([gla_np[0][1], gla_np[1][1]]))
    ssd_consts = _ssd_consts()
    rw = jnp.zeros((d, LANE), F32).at[:, :N_EXPERTS].set(router_w)
    rwh, rwl = _split2(rw)
    rb = _lane_slot(router_bias, 0)

    ckvs, kropes, glas, ssds = [], [], [], []
    for l in range(DEPTH):
        w_main, w_small = _pack_w_in(w_in[l])
        u, s_all = _norm_proj_call(x, mod3, norm1[l].reshape(1, d), w_main, w_small, l)
        kropes.append(s_all[:N_CTX, S_KR:S_KR + A_ROPE].reshape(BATCH, SEQ, A_ROPE))

        wq = a_wq[l].reshape(A_QLORA, A_HEADS, A_QK)
        wq = _pad_heads(jnp.concatenate([wq[..., :A_NOPE], wq[..., A_NOPE:][..., _ROPE_PERM]], axis=-1), A_QK)
        wkv = a_wkv[l].reshape(A_KVLORA, A_HEADS, A_NOPE + A_V)
        aw = dict(qnw=a_q_norm[l].reshape(1, A_QLORA), wq=wq.astype(BF16),
                  gq=_qk_gain(a_qk_qnorm[l], A_QK ** -0.5),
                  kvnw=a_kv_norm[l].reshape(1, A_KVLORA),
                  wkn=_pad_heads(wkv[..., :A_NOPE], A_NOPE).astype(BF16),
                  wv=_pad_heads(wkv[..., A_NOPE:], A_V).astype(BF16),
                  gk=_qk_gain(a_qk_knorm[l], 1.0))
        q_c, k_c, v_c, ckvn = _attn_prep_call(N_CTX, 0, (u,), aw, has_q=True, norm_kv=True, rope=False, emit_ckv=True)
        o_c = _attn_call(q_c, k_c, v_c, None, n_seq=BATCH, seq_len=SEQ, hps=A_HEADS, tq=SEQ)
        ckvs.append(ckvn.reshape(BATCH, SEQ, A_KVLORA))
        q_d, k_d, v_d = _attn_prep_call(N_DEC, N_CTX, (u,), aw, has_q=True, norm_kv=True, rope=True, emit_ckv=False,
                                        tables=rope_tabs)
        kr_x = jnp.zeros((DEC_BATCH * PAST_LEN, LANE), F32).at[:, A_NOPE:A_QK].set(
            cache_krope[:, l].reshape(DEC_BATCH * PAST_LEN, A_ROPE)[:, _ROPE_PERM])
        k_x, v_x = _attn_prep_call(DEC_BATCH * PAST_LEN, 0, (cache_ckv[:, l].reshape(DEC_BATCH * PAST_LEN, A_KVLORA), kr_x),
                                   aw, has_q=False, norm_kv=False, rope=False, emit_ckv=False)
        o_d = _attn_call(q_d, k_d, v_d, (k_x, v_x), n_seq=DEC_BATCH, seq_len=DEC_SEQ, hps=2, tq=256)

        hk = B_HEADS * B_DK
        wg = (jnp.zeros((2, LANE, hk), F32).at[0, S_GF:S_GF + B_GATE_RANK].set(b_wg[l, 0])
              .at[1, S_GB:S_GB + B_GATE_RANK].set(b_wg[l, 1]))
        wgh, wgl = _split2(wg)
        bg = b_bg[l].reshape(2, 1, hk)
        g_fc, g_bc, st_gc = _gla_call(u, s_all, wgh, wgl, bg, gla_mall, gla_masks, None,
                                      n_seq=BATCH, seq_len=SEQ, row0=0)
        g_fd, g_bd, _ = _gla_call(u, s_all, wgh, wgl, bg, gla_mall, gla_masks, jnp.swapaxes(state_gla[:, l], -1, -2),
                                  n_seq=DEC_BATCH, seq_len=DEC_SEQ, row0=N_CTX)
        glas.append(jnp.swapaxes(st_gc, -1, -2))

        w8 = jnp.zeros((8, C_XBC), F32).at[:C_CONV].set(c_conv_w[l])
        cb = c_conv_b[l].reshape(1, C_XBC)
        xbc_c = _conv_call(u, w8, cb, n_seq=BATCH, seq_len=SEQ, row0=0)
        xbc_d = _conv_call(u, w8, cb, n_seq=DEC_BATCH, seq_len=DEC_SEQ, row0=N_CTX)
        dtb = jnp.stack([_lane_slot(c_dt_bias[l, 0], S_DTF), _lane_slot(c_dt_bias[l, 1], S_DTB)])
        a_neg = -jnp.exp(c_A_log[l].astype(F32))
        aneg = jnp.stack([_lane_slot(a_neg[0], S_DTF), _lane_slot(a_neg[1], S_DTB)])
        y_fc, y_bc, st_c = _ssd_call(xbc_c, s_all, dtb, aneg, ssd_consts, None, n_seq=BATCH, seq_len=SEQ, row0=0)
        st0 = state_ssd[:, l].reshape(DEC_BATCH, 2, C_HEADS // 2, LANE, C_STATE)
        y_fd, y_bd, _ = _ssd_call(xbc_d, s_all, dtb, aneg, ssd_consts, st0, n_seq=DEC_BATCH, seq_len=DEC_SEQ, row0=N_CTX)
        ssds.append(st_c.reshape(BATCH, 2, C_HEADS, C_HEADDIM, C_STATE))

        awo = a_wo[l].reshape(A_HEADS, A_V, d)
        awo = jnp.concatenate([awo, jnp.zeros((A_HEADS, HEAD_PAD - A_V, d), awo.dtype)], axis=1)
        mw = dict(awo=awo.reshape(A_HEADS * HEAD_PAD, d).astype(BF16), onw=b_onorm[l].reshape(1, B_DV),
                  bwo=b_wo[l].astype(BF16), dexp=jnp.repeat(c_D[l], C_HEADDIM).reshape(1, C_INNER),
                  cnw=c_norm[l].reshape(1, C_INNER), cwo=c_wo[l].astype(BF16), wout=w_out[l].astype(BF16))
        x = _merge_call(x, mod3, u, o_c, g_fc, g_bc, y_fc, y_bc, xbc_c, mw, n_rows=N_CTX, row0=0, layer=l)
        x = _merge_call(x, mod3, u, o_d, g_fd, g_bd, y_fd, y_bd, xbc_d, mw, n_rows=N_DEC, row0=N_CTX, layer=l)

        x = _moe_call(x, mod3, norm2[l].reshape(1, d), rwh, rwl, rb,
                      e_w1, e_w3, e_w2, l)

    y_prompt = x[:N_CTX].reshape(BATCH, SEQ, d)
    y_sample = x[N_CTX:].reshape(DEC_BATCH, DEC_SEQ, d)
    new_ckv = jnp.stack(ckvs, axis=1)
    new_krope = jnp.stack(kropes, axis=1)
    new_state_gla = jnp.stack(glas, axis=1)
    new_state_ssd = jnp.stack(ssds, axis=1)
    return (y_prompt, y_sample, new_ckv, new_krope, new_state_gla, new_state_ssd)
```

```python
import functools
import math

import numpy as np
import jax
import jax.numpy as jnp
from jax import lax
from jax.experimental import pallas as pl
from jax.experimental.pallas import tpu as pltpu

F32 = jnp.float32
BF16 = jnp.bfloat16

D_MODEL = 1024
BATCH = 16
SEQ = 256
DEPTH = 2
DEC_BATCH = 2
DEC_SEQ = 2048
PAST_LEN = 512
GRID_W = 64
EPS = 1e-6
ROPE_BASE = 10000.0

A_HEADS = 16
A_NOPE = 64
A_ROPE = 32
A_QK = A_NOPE + A_ROPE
A_V = 64
A_QLORA = 512
A_KVLORA = 256

B_HEADS = 4
B_DK = 128
B_DV = 256
B_GATE_RANK = 16
B_GATE_NORM = 16.0

C_HEADS = 16
C_HEADDIM = 64
C_INNER = C_HEADS * C_HEADDIM
C_GROUPS = 2
C_STATE = 128
C_XBC = C_INNER + 2 * C_GROUPS * C_STATE
C_CONV = 5

N_EXPERTS = 16
N_EXPERT_GROUPS = 4
D_EXPERT = 512

N_CTX = BATCH * SEQ
N_DEC = DEC_BATCH * DEC_SEQ
N_TOK = N_CTX + N_DEC

LANE = 128
HEAD_PAD = 128
CHUNK = 128

U_CQ = 0
U_CKV = 512
U_KR = 768
U_QK = 1024
U_V = 2048
U_OG = 3072
U_Z = 4096
U_X = 5120
U_GATES = 6144
U_BC = 9216
U_WIDTH = 9728
S_GF, S_GB, S_DTF, S_DTB, S_KR = 0, 16, 32, 48, 64

_IN_SPLITS = (
    ('a_cq', A_QLORA), ('a_ckv', A_KVLORA), ('a_krope', A_ROPE),
    ('b_q', B_HEADS * B_DK), ('b_k', B_HEADS * B_DK), ('b_v', B_HEADS * B_DV),
    ('b_og', B_HEADS * B_DV), ('b_gf', B_GATE_RANK), ('b_gb', B_GATE_RANK),
    ('c_z', C_INNER), ('c_xbc', C_XBC), ('c_dtf', C_HEADS), ('c_dtb', C_HEADS),
    ('gates', 3 * D_MODEL),
)
_ROPE_PERM = np.concatenate([np.arange(0, A_ROPE, 2), np.arange(1, A_ROPE, 2)])


def _col_slices(w):
    parts, start = {}, 0
    for name, size in _IN_SPLITS:
        parts[name] = w[:, start:start + size]
        start += size
    return parts


def _dot(a, b):
    return jnp.dot(a, b, preferred_element_type=F32)


def _dot_nt(a, b):
    return lax.dot_general(a, b, (((1,), (1,)), ((), ())), preferred_element_type=F32)


def _dot_tn(a, b):
    return lax.dot_general(a, b, (((0,), (0,)), ((), ())), preferred_element_type=F32)


def _split2(x):
    hi = x.astype(BF16)
    lo = (x - hi.astype(F32)).astype(BF16)
    return hi, lo


def _silu(x):
    return x * jax.nn.sigmoid(x)


def _softplus(x):
    return jnp.maximum(x, 0.0) + jnp.log1p(jnp.exp(-jnp.abs(x)))


def _log_sigmoid(x):
    return jnp.minimum(x, 0.0) - jnp.log1p(jnp.exp(-jnp.abs(x)))


def _mod_kernel(c_ref, w_ref, b_ref, o_ref):
    s = _silu(c_ref[...]).astype(BF16)
    o_ref[...] = _dot(s, w_ref[...].astype(BF16)) + b_ref[...]


def _mod_call(cvecs, w_ada, b_ada):
    tn = 1536
    return pl.pallas_call(
        _mod_kernel,
        grid=(DEPTH, 6 * D_MODEL // tn),
        in_specs=[pl.BlockSpec((8, D_MODEL), lambda l, j: (0, 0)),
                  pl.BlockSpec((None, D_MODEL, tn), lambda l, j: (l, 0, j)),
                  pl.BlockSpec((None, 1, tn), lambda l, j: (l, 0, j))],
        out_specs=pl.BlockSpec((None, 8, tn), lambda l, j: (l, 0, j)),
        out_shape=jax.ShapeDtypeStruct((DEPTH, 8, 6 * D_MODEL), F32),
        name="adaln_mod",
    )(cvecs, w_ada, b_ada.reshape(DEPTH, 1, 6 * D_MODEL))


def _mod_row(i, tm, layer):
    n_ctx = N_CTX // tm
    per_b = DEC_SEQ // tm
    return layer * 8 + jnp.where(i < n_ctx, 0, 1 + (i - n_ctx) // per_b)


def _norm_proj_kernel(x_ref, mod_ref, nw_ref, w_ref, ws_ref, u_ref, s_ref, h_scr):
    @pl.when(pl.program_id(1) == 0)
    def _():
        x = x_ref[...]
        m = mod_ref[...]
        y = x * lax.rsqrt(jnp.mean(x * x, axis=-1, keepdims=True) + EPS) * nw_ref[...]
        h = (y * (1.0 + m[:, D_MODEL:2 * D_MODEL]) + m[:, 0:D_MODEL]).astype(BF16)
        h_scr[...] = h
        s_ref[...] = _dot(h, ws_ref[...])

    u_ref[...] = _dot(h_scr[...], w_ref[...]).astype(BF16)


def _norm_proj_call(x, mod3, nw, w_main, w_small, layer):
    tm, tn = 2048, 512
    return pl.pallas_call(
        _norm_proj_kernel,
        grid=(N_TOK // tm, U_WIDTH // tn),
        in_specs=[pl.BlockSpec((tm, D_MODEL), lambda i, j: (i, 0)),
                  pl.BlockSpec((None, 1, 6 * D_MODEL), lambda i, j: (_mod_row(i, tm, layer), 0, 0)),
                  pl.BlockSpec((1, D_MODEL), lambda i, j: (0, 0)),
                  pl.BlockSpec((D_MODEL, tn), lambda i, j: (0, j)),
                  pl.BlockSpec((D_MODEL, LANE), lambda i, j: (0, 0))],
        out_specs=[pl.BlockSpec((tm, tn), lambda i, j: (i, j)),
                   pl.BlockSpec((tm, LANE), lambda i, j: (i, 0))],
        out_shape=[jax.ShapeDtypeStruct((N_TOK, U_WIDTH), BF16),
                   jax.ShapeDtypeStruct((N_TOK, LANE), F32)],
        scratch_shapes=[pltpu.VMEM((tm, D_MODEL), BF16)],
        compiler_params=pltpu.CompilerParams(dimension_semantics=("parallel", "arbitrary")),
        name="norm_in_proj",
    )(x, mod3, nw, w_main, w_small)


def _head_norm_rope(t, gain, cos, sin, lane):
    ss = jnp.sum(t * t, axis=-1, keepdims=True)
    t = t * lax.rsqrt(ss * (1.0 / A_QK) + EPS) * gain
    if cos is not None:
        half = A_ROPE // 2
        swapped = jnp.where(lane < A_NOPE + half, pltpu.roll(t, HEAD_PAD - half, 1), pltpu.roll(t, half, 1))
        t = t * cos + swapped * sin
    return t


def _attn_prep_kernel(*refs, has_q, norm_kv, rope, emit_ckv):
    refs = list(refs)
    cq_ref = refs.pop(0) if has_q else None
    ckv_ref, kr_ref = refs.pop(0), refs.pop(0)
    if has_q:
        qnw_ref, wq_ref, gq_ref = refs.pop(0), refs.pop(0), refs.pop(0)
    kvnw_ref, wkn_ref, wv_ref, gk_ref = refs.pop(0), refs.pop(0), refs.pop(0), refs.pop(0)
    cos = sin = None
    if rope:
        cos, sin = refs.pop(0)[...], refs.pop(0)[...]
    q_ref = refs.pop(0) if has_q else None
    k_ref, v_ref = refs.pop(0), refs.pop(0)
    ckvn_ref = refs.pop(0) if emit_ckv else None

    tm = ckv_ref.shape[0]
    lane = lax.broadcasted_iota(jnp.int32, (tm, HEAD_PAD), 1)

    ckv = ckv_ref[...].astype(F32)
    if norm_kv:
        ckv = ckv * lax.rsqrt(jnp.mean(ckv * ckv, axis=-1, keepdims=True) + EPS) * kvnw_ref[...]
    if emit_ckv:
        ckvn_ref[...] = ckv
    ckv_b = ckv.astype(BF16)
    v_ref[...] = _dot(ckv_b, wv_ref[...]).astype(BF16)
    kf = _dot(ckv_b, wkn_ref[...])
    kr = kr_ref[...].astype(F32)
    gk = gk_ref[...]
    for h in range(A_HEADS):
        sl = slice(h * HEAD_PAD, (h + 1) * HEAD_PAD)
        k_ref[:, sl] = _head_norm_rope(kf[:, sl] + kr, gk, cos, sin, lane).astype(BF16)

    if has_q:
        cq = cq_ref[...].astype(F32)
        cq = cq * lax.rsqrt(jnp.mean(cq * cq, axis=-1, keepdims=True) + EPS) * qnw_ref[...]
        qf = _dot(cq.astype(BF16), wq_ref[...])
        gq = gq_ref[...]
        for h in range(A_HEADS):
            sl = slice(h * HEAD_PAD, (h + 1) * HEAD_PAD)
            q_ref[:, sl] = _head_norm_rope(qf[:, sl], gq, cos, sin, lane).astype(BF16)


def _attn_prep_call(n_rows, row0, srcs, wts, *, has_q, norm_kv, rope, emit_ckv, tables=None):
    tm = 256
    r0 = row0 // tm
    pw = A_HEADS * HEAD_PAD
    full = lambda shape: pl.BlockSpec(shape, lambda i: (0,) * len(shape))
    args, in_specs = [], []
    if len(srcs) == 1:
        u = srcs[0]
        if has_q:
            args.append(u)
            in_specs.append(pl.BlockSpec((tm, A_QLORA), lambda i: (i + r0, U_CQ // A_QLORA)))
        args += [u, u]
        in_specs += [pl.BlockSpec((tm, A_KVLORA), lambda i: (i + r0, U_CKV // A_KVLORA)),
                     pl.BlockSpec((tm, LANE), lambda i: (i + r0, U_KR // LANE))]
    else:
        args += list(srcs)
        in_specs += [pl.BlockSpec((tm, A_KVLORA), lambda i: (i, 0)),
                     pl.BlockSpec((tm, LANE), lambda i: (i, 0))]
    if has_q:
        args += [wts['qnw'], wts['wq'], wts['gq']]
        in_specs += [full((1, A_QLORA)), full((A_QLORA, pw)), full((1, HEAD_PAD))]
    args += [wts['kvnw'], wts['wkn'], wts['wv'], wts['gk']]
    in_specs += [full((1, A_KVLORA)), full((A_KVLORA, pw)), full((A_KVLORA, pw)), full((1, HEAD_PAD))]
    if rope:
        per_seq = DEC_SEQ // tm
        args += list(tables)
        in_specs += [pl.BlockSpec((tm, HEAD_PAD), lambda i: (i % per_seq, 0))] * 2
    out_specs, out_shape = [], []
    for _ in range((1 if has_q else 0) + 2):
        out_specs.append(pl.BlockSpec((tm, pw), lambda i: (i, 0)))
        out_shape.append(jax.ShapeDtypeStruct((n_rows, pw), BF16))
    if emit_ckv:
        out_specs.append(pl.BlockSpec((tm, A_KVLORA), lambda i: (i, 0)))
        out_shape.append(jax.ShapeDtypeStruct((n_rows, A_KVLORA), F32))
    return pl.pallas_call(
        functools.partial(_attn_prep_kernel, has_q=has_q, norm_kv=norm_kv, rope=rope, emit_ckv=emit_ckv),
        grid=(n_rows // tm,),
        in_specs=in_specs, out_specs=out_specs, out_shape=out_shape,
        compiler_params=pltpu.CompilerParams(dimension_semantics=("parallel",)),
        name="mla_prep",
    )(*args)


def _attn_kernel(*refs, hps, has_ctx):
    if has_ctx:
        q_ref, k_ref, v_ref, kc_ref, vc_ref, o_ref = refs
    else:
        q_ref, k_ref, v_ref, o_ref = refs

    def scores(hh):
        sl = slice(hh * HEAD_PAD, (hh + 1) * HEAD_PAD)
        q = q_ref[:, sl]
        return _dot_nt(q, k_ref[:, sl]), (_dot_nt(q, kc_ref[:, sl]) if has_ctx else None)

    nxt = scores(0)
    for hh in range(hps):
        sl = slice(hh * HEAD_PAD, (hh + 1) * HEAD_PAD)
        s, s2 = nxt
        if hh + 1 < hps:
            nxt = scores(hh + 1)
        m = jnp.max(s, axis=-1, keepdims=True)
        if has_ctx:
            m = jnp.maximum(m, jnp.max(s2, axis=-1, keepdims=True))
        p = jnp.exp(s - m)
        den = jnp.sum(p, axis=-1, keepdims=True)
        o = _dot(p.astype(BF16), v_ref[:, sl])
        if has_ctx:
            p2 = jnp.exp(s2 - m)
            den = den + jnp.sum(p2, axis=-1, keepdims=True)
            o = o + _dot(p2.astype(BF16), vc_ref[:, sl])
        o_ref[:, sl] = (o / den).astype(BF16)


def _attn_call(q, k, v, ctx_kv, *, n_seq, seq_len, hps, tq):
    pw = A_HEADS * HEAD_PAD
    bw = hps * HEAD_PAD
    nq = seq_len // tq
    in_specs = [pl.BlockSpec((tq, bw), lambda b, h, i: (b * nq + i, h)),
                pl.BlockSpec((seq_len, bw), lambda b, h, i: (b, h)),
                pl.BlockSpec((seq_len, bw), lambda b, h, i: (b, h))]
    args = [q, k, v]
    if ctx_kv is not None:
        in_specs += [pl.BlockSpec((PAST_LEN, bw), lambda b, h, i: (b, h))] * 2
        args += list(ctx_kv)
    return pl.pallas_call(
        functools.partial(_attn_kernel, hps=hps, has_ctx=ctx_kv is not None),
        grid=(n_seq, A_HEADS // hps, nq),
        in_specs=in_specs,
        out_specs=pl.BlockSpec((tq, bw), lambda b, h, i: (b * nq + i, h)),
        out_shape=jax.ShapeDtypeStruct((n_seq * seq_len, pw), BF16),
        compiler_params=pltpu.CompilerParams(dimension_semantics=("parallel", "parallel", "arbitrary")),
        name="mla_attention",
    )(*args)


def _conv_kernel(x_ref, w_ref, b_ref, o_ref):
    x = x_ref[...].astype(F32)
    n = x.shape[0]
    row = lax.broadcasted_iota(jnp.int32, x.shape, 0)
    half = (C_CONV - 1) // 2
    acc = x * w_ref[half:half + 1, :] + b_ref[...]
    for d in range(-half, half + 1):
        if d == 0:
            continue
        shifted = pltpu.roll(x, (-d) % n, 0)
        valid = jnp.logical_and(row + d >= 0, row + d < n)
        acc = acc + jnp.where(valid, shifted, 0.0) * w_ref[half + d:half + d + 1, :]
    o_ref[...] = _silu(acc).astype(BF16)


def _conv_call(u, w8, b, *, n_seq, seq_len, row0):
    tc = 512
    b0 = row0 // seq_len
    nx = C_INNER // tc
    ucol = lambda j: jnp.where(j < nx, U_X // tc + j, U_BC // tc + j - nx)
    return pl.pallas_call(
        _conv_kernel,
        grid=(n_seq, C_XBC // tc),
        in_specs=[pl.BlockSpec((seq_len, tc), lambda s, j: (s + b0, ucol(j))),
                  pl.BlockSpec((8, tc), lambda s, j: (0, j)),
                  pl.BlockSpec((1, tc), lambda s, j: (0, j))],
        out_specs=pl.BlockSpec((seq_len, tc), lambda s, j: (s, j)),
        out_shape=jax.ShapeDtypeStruct((n_seq * seq_len, C_XBC), BF16),
        compiler_params=pltpu.CompilerParams(dimension_semantics=("parallel", "parallel")),
        name="ssd_conv",
    )(u, w8, b)


def _ssd_consts():
    idx = np.arange(CHUNK)
    tri = np.stack([idx[None, :] <= idx[:, None], idx[None, :] >= idx[:, None]]).astype(np.float32)
    expand = np.zeros((2, LANE, C_INNER), np.float32)
    tile = np.zeros((2, LANE, C_HEADS * LANE), np.float32)
    for di, off in enumerate((S_DTF, S_DTB)):
        for h in range(C_HEADS):
            expand[di, off + h, h * C_HEADDIM:(h + 1) * C_HEADDIM] = 1.0
            tile[di, off + h, h * LANE:(h + 1) * LANE] = 1.0
    return jnp.asarray(tri, BF16), jnp.asarray(expand, BF16), jnp.asarray(tile, BF16)


def _ssd_kernel(*refs, has_init):
    refs = list(refs)
    xs = [(refs.pop(0), refs.pop(0), refs.pop(0)) for _ in range(2)]
    dtb_ref, an_ref, tri_ref, exp_ref, tile_ref = (refs.pop(0) for _ in range(5))
    st0_ref = refs.pop(0) if has_init else None
    y_refs = (refs.pop(0), refs.pop(0))
    stf_ref, st_scr = refs

    @pl.when(pl.program_id(1) == 0)
    def _():
        st_scr[...] = st0_ref[...] if has_init else jnp.zeros(st_scr.shape, F32)

    t = CHUNK
    ri = lax.broadcasted_iota(jnp.int32, (t, t), 0)
    ci = lax.broadcasted_iota(jnp.int32, (t, t), 1)
    prow = lax.broadcasted_iota(jnp.int32, (LANE, C_STATE), 0)
    gs = C_GROUPS * C_STATE
    pairs_per_group = C_HEADS // C_GROUPS // 2
    for di in range(2):
        x_ref, bc_ref, s_ref = xs[di]
        y_ref = y_refs[di]
        off = (S_DTF, S_DTB)[di]
        dt = _softplus(s_ref[...] + dtb_ref[di])
        a = dt * an_ref[di]
        a_hi, a_lo = _split2(a)
        tri = tri_ref[di]
        acum = _dot(tri, a_hi) + _dot(tri, a_lo)
        acum_t = acum.T
        dt_t = dt.T
        edge = 0 if di else t - 1
        atot = acum[edge:edge + 1, :]
        causal = (ci >= ri) if di else (ci <= ri)
        dec_all = jnp.exp(atot)
        e_exp = _dot(jnp.exp(acum).astype(BF16), exp_ref[di])
        wj_exp = _dot((jnp.exp(atot - acum) * dt).astype(BF16), exp_ref[di])
        c_hi, c_lo = _split2(acum)
        acol = _dot(c_hi, tile_ref[di]) + _dot(c_lo, tile_ref[di])
        bc = bc_ref[...]
        cb, bmat, cmat = [], [], []
        for g in range(C_GROUPS):
            bmat.append(bc[:, g * C_STATE:(g + 1) * C_STATE])
            cmat.append(bc[:, gs + g * C_STATE:gs + (g + 1) * C_STATE])
            cb.append(_dot_nt(cmat[g], bmat[g]))
        lane = lax.broadcasted_iota(jnp.int32, (t, LANE), 1)
        low = lane < C_HEADDIM
        for m in range(C_HEADS // 2):
            g = m // pairs_per_group
            ps = slice(m * LANE, (m + 1) * LANE)
            xp = x_ref[:, ps]
            y_diag = []
            for hh in range(2):
                h = 2 * m + hh
                la = off + h
                lm = jnp.exp(jnp.where(causal, acol[:, h * LANE:(h + 1) * LANE] - acum_t[la:la + 1, :], -jnp.inf))
                w = (cb[g] * lm * dt_t[la:la + 1, :]).astype(BF16)
                y_diag.append(_dot(w, xp))
            st = st_scr[di, m]
            y_off = _dot_nt(cmat[g], st.astype(BF16)) * e_exp[:, ps]
            y_ref[:, ps] = (jnp.where(low, y_diag[0], y_diag[1]) + y_off).astype(BF16)
            xw = (xp.astype(F32) * wj_exp[:, ps]).astype(BF16)
            la = off + 2 * m
            dec = jnp.where(prow < C_HEADDIM, dec_all[:, la:la + 1], dec_all[:, la + 1:la + 2])
            st_scr[di, m] = dec * st + _dot_tn(xw, bmat[g])

    @pl.when(pl.program_id(1) == pl.num_programs(1) - 1)
    def _():
        stf_ref[...] = st_scr[...]


def _ssd_call(xbc, s_all, dtb, aneg, consts, st0, *, n_seq, seq_len, row0):
    t = CHUNK
    nch = seq_len // t
    r0 = row0 // t
    npair = C_HEADS // 2
    bcw = 2 * C_GROUPS * C_STATE
    tri, expand, tile = consts
    full = lambda a: pl.BlockSpec(a.shape, lambda b, c: (0,) * a.ndim)
    in_specs, args = [], []
    for cidx in (lambda c: c, lambda c: nch - 1 - c):
        in_specs += [pl.BlockSpec((t, C_INNER), lambda b, c, cidx=cidx: (b * nch + cidx(c), 0)),
                     pl.BlockSpec((t, bcw), lambda b, c, cidx=cidx: (b * nch + cidx(c), C_INNER // bcw)),
                     pl.BlockSpec((t, LANE), lambda b, c, cidx=cidx: (r0 + b * nch + cidx(c), 0))]
        args += [xbc, xbc, s_all]
    in_specs += [full(dtb), full(aneg), full(tri), full(expand), full(tile)]
    args += [dtb, aneg, tri, expand, tile]
    st_spec = pl.BlockSpec((None, 2, npair, LANE, C_STATE), lambda b, c: (b, 0, 0, 0, 0))
    if st0 is not None:
        in_specs.append(st_spec)
        args.append(st0)
    return pl.pallas_call(
        functools.partial(_ssd_kernel, has_init=st0 is not None),
        grid=(n_seq, nch),
        in_specs=in_specs,
        out_specs=[pl.BlockSpec((t, C_INNER), lambda b, c: (b * nch + c, 0)),
                   pl.BlockSpec((t, C_INNER), lambda b, c: (b * nch + nch - 1 - c, 0)),
                   st_spec],
        out_shape=[jax.ShapeDtypeStruct((n_seq * seq_len, C_INNER), BF16),
                   jax.ShapeDtypeStruct((n_seq * seq_len, C_INNER), BF16),
                   jax.ShapeDtypeStruct((n_seq, 2, npair, LANE, C_STATE), F32)],
        scratch_shapes=[pltpu.VMEM((2, npair, LANE, C_STATE), F32)],
        compiler_params=pltpu.CompilerParams(dimension_semantics=("parallel", "arbitrary")),
        name="ssd_scan",
    )(*args)


def _gla_consts(bwd):
    t = CHUNK
    nlev = int(math.log2(t))
    idx = np.arange(t)
    mats = np.zeros((nlev + 2, t, t), np.float32)
    masks = np.zeros((nlev + 1, t, t), np.float32)
    masks[0] = np.eye(t)
    for lvl in range(nlev):
        s = 1 << lvl
        blk = idx // (2 * s)
        upper = (idx % (2 * s)) >= s
        last_low = blk * 2 * s + s - 1
        rowtok = ~upper if bwd else upper
        for i in range(t):
            r = last_low[i]
            if not bwd:
                if upper[i]:
                    mats[lvl, i, r + 1:i + 1] = 1.0
                else:
                    mats[lvl, i, i + 1:r + 1] = 1.0
            else:
                if upper[i]:
                    mats[lvl, i, r + 1:i] = 1.0
                else:
                    mats[lvl, i, i:r + 1] = 1.0
        masks[lvl + 1] = ((blk[:, None] == blk[None, :]) & rowtok[:, None] & (~rowtok)[None, :])
    incl = (idx[None, :] >= idx[:, None]) if bwd else (idx[None, :] <= idx[:, None])
    mats[nlev] = incl
    mats[nlev + 1] = 1.0 - incl
    return mats.reshape((nlev + 2) * t, t), masks


def _gla_kernel(*refs, has_init):
    refs = list(refs)
    xs = [(refs.pop(0), refs.pop(0), refs.pop(0)) for _ in range(2)]
    wgh_ref, wgl_ref, bg_ref, mall_ref, mask_ref = (refs.pop(0) for _ in range(5))
    st0_ref = refs.pop(0) if has_init else None
    o_refs = (refs.pop(0), refs.pop(0))
    stf_ref, st_scr = refs

    @pl.when(pl.program_id(1) == 0)
    def _():
        st_scr[...] = st0_ref[...] if has_init else jnp.zeros(st_scr.shape, F32)

    t = CHUNK
    nlev = mask_ref.shape[1] - 1
    hk = B_HEADS * B_DK
    e_alls = []
    for di in range(2):
        s_hi, s_lo = _split2(xs[di][2][...])
        wgh = wgh_ref[di]
        logit = _dot(s_hi, wgh) + _dot(s_lo, wgh) + _dot(s_hi, wgl_ref[di]) + bg_ref[di]
        g_all = _log_sigmoid(logit) * (1.0 / B_GATE_NORM)
        e_alls.append(jnp.exp(_dot(mall_ref[di], g_all.astype(BF16))))
    for h in range(B_HEADS):
        for di in range(2):
            qk_ref, v_ref, _ = xs[di]
            o_ref = o_refs[di]
            e_all = e_alls[di]
            edge = 0 if di else t - 1
            ks = slice(h * B_DK, (h + 1) * B_DK)
            vs = slice(h * B_DV, (h + 1) * B_DV)
            q = qk_ref[:, ks].astype(F32) * (B_DK ** -0.5)
            k = qk_ref[:, hk + h * B_DK:hk + (h + 1) * B_DK]
            kf = k.astype(F32)
            v = v_ref[:, vs]
            amat = _dot_nt(q.astype(BF16), k) * mask_ref[di, 0]
            for lvl in range(nlev):
                e = e_all[lvl * t:(lvl + 1) * t, ks]
                amat = amat + _dot_nt((q * e).astype(BF16), (kf * e).astype(BF16)) * mask_ref[di, lvl + 1]
            e_in = e_all[nlev * t:(nlev + 1) * t, ks]
            e_out = e_all[(nlev + 1) * t:(nlev + 2) * t, ks]
            st = st_scr[di, h]
            o = _dot(amat.astype(BF16), v) + _dot_nt((q * e_in).astype(BF16), st.astype(BF16))
            o_ref[:, vs] = o.astype(BF16)
            st_scr[di, h] = st * e_in[edge:edge + 1, :] + _dot_tn(v, (kf * e_out).astype(BF16))

    @pl.when(pl.program_id(1) == pl.num_programs(1) - 1)
    def _():
        stf_ref[...] = st_scr[...]


def _gla_call(u, s_all, wgh, wgl, bg, mall, masks, st0, *, n_seq, seq_len, row0):
    t = CHUNK
    nch = seq_len // t
    r0 = row0 // t
    hk, hv = B_HEADS * B_DK, B_HEADS * B_DV
    full = lambda a: pl.BlockSpec(a.shape, lambda b, c: (0,) * a.ndim)
    in_specs, args = [], []
    for cidx in (lambda c: c, lambda c: nch - 1 - c):
        in_specs += [pl.BlockSpec((t, 2 * hk), lambda b, c, cidx=cidx: (r0 + b * nch + cidx(c), U_QK // (2 * hk))),
                     pl.BlockSpec((t, hv), lambda b, c, cidx=cidx: (r0 + b * nch + cidx(c), U_V // hv)),
                     pl.BlockSpec((t, LANE), lambda b, c, cidx=cidx: (r0 + b * nch + cidx(c), 0))]
        args += [u, u, s_all]
    in_specs += [full(wgh), full(wgl), full(bg), full(mall), full(masks)]
    args += [wgh, wgl, bg, mall, masks]
    st_spec = pl.BlockSpec((None, 2, B_HEADS, B_DV, B_DK), lambda b, c: (b, 0, 0, 0, 0))
    if st0 is not None:
        in_specs.append(st_spec)
        args.append(st0)
    return pl.pallas_call(
        functools.partial(_gla_kernel, has_init=st0 is not None),
        grid=(n_seq, nch),
        in_specs=in_specs,
        out_specs=[pl.BlockSpec((t, hv), lambda b, c: (b * nch + c, 0)),
                   pl.BlockSpec((t, hv), lambda b, c: (b * nch + nch - 1 - c, 0)),
                   st_spec],
        out_shape=[jax.ShapeDtypeStruct((n_seq * seq_len, hv), BF16),
                   jax.ShapeDtypeStruct((n_seq * seq_len, hv), BF16),
                   jax.ShapeDtypeStruct((n_seq, 2, B_HEADS, B_DV, B_DK), F32)],
        scratch_shapes=[pltpu.VMEM((2, B_HEADS, B_DV, B_DK), F32)],
        compiler_params=pltpu.CompilerParams(dimension_semantics=("parallel", "arbitrary")),
        name="gla_scan",
    )(*args)


def _merge_kernel(x_ref, mod_ref, oa_ref, gf_ref, gb_ref, og_ref, yf_ref, yb_ref, xc_ref, z_ref, gt_ref,
                  awo_ref, onw_ref, bwo_ref, dexp_ref, cnw_ref, cwo_ref, wout_ref, o_ref):
    o_a = _dot(oa_ref[...], awo_ref[...])

    og = gf_ref[...].astype(F32) + gb_ref[...].astype(F32)
    onw = onw_ref[...]
    parts = []
    for h in range(B_HEADS):
        th = og[:, h * B_DV:(h + 1) * B_DV]
        parts.append(th * lax.rsqrt(jnp.mean(th * th, axis=-1, keepdims=True) + EPS) * onw)
    ob_in = jnp.concatenate(parts, axis=-1) * _silu(og_ref[...].astype(F32))
    o_b = _dot(ob_in.astype(BF16), bwo_ref[...])

    y = yf_ref[...].astype(F32) + yb_ref[...].astype(F32) + dexp_ref[...] * xc_ref[...].astype(F32)
    y = y * _silu(z_ref[...].astype(F32))
    y = y * lax.rsqrt(jnp.mean(y * y, axis=-1, keepdims=True) + EPS) * cnw_ref[...]
    o_c = _dot(y.astype(BF16), cwo_ref[...])

    d = D_MODEL
    merged = (jax.nn.sigmoid(gt_ref[:, 0:d].astype(F32)) * o_a
              + jax.nn.sigmoid(gt_ref[:, d:2 * d].astype(F32)) * o_b
              + jax.nn.sigmoid(gt_ref[:, 2 * d:3 * d].astype(F32)) * o_c)
    out = _dot(merged.astype(BF16), wout_ref[...])
    o_ref[...] = x_ref[...] + mod_ref[:, 2 * d:3 * d] * out


def _merge_call(x, mod3, u, o_attn, g_f, g_b, y_f, y_b, xbc, wts, *, n_rows, row0, layer):
    tm = 512
    r0 = row0 // tm
    d = D_MODEL
    loc = lambda w: pl.BlockSpec((tm, w), lambda i: (i, 0))
    full = lambda a: pl.BlockSpec(a.shape, lambda i: (0,) * a.ndim)
    in_specs = [pl.BlockSpec((tm, d), lambda i: (i + r0, 0)),
                pl.BlockSpec((None, 1, 6 * d), lambda i: (_mod_row(i + r0, tm, layer), 0, 0)),
                loc(A_HEADS * HEAD_PAD), loc(d), loc(d),
                pl.BlockSpec((tm, d), lambda i: (i + r0, U_OG // d)),
                loc(d), loc(d),
                pl.BlockSpec((tm, d), lambda i: (i, 0)),
                pl.BlockSpec((tm, d), lambda i: (i + r0, U_Z // d)),
                pl.BlockSpec((tm, 3 * d), lambda i: (i + r0, U_GATES // (3 * d)))]
    w_args = [wts[n] for n in ('awo', 'onw', 'bwo', 'dexp', 'cnw', 'cwo', 'wout')]
    in_specs += [full(a) for a in w_args]
    return pl.pallas_call(
        _merge_kernel,
        grid=(n_rows // tm,),
        in_specs=in_specs,
        out_specs=pl.BlockSpec((tm, d), lambda i: (i + r0, 0)),
        out_shape=jax.ShapeDtypeStruct((N_TOK, d), F32),
        input_output_aliases={0: 0},
        compiler_params=pltpu.CompilerParams(dimension_semantics=("parallel",)),
        name="mixer_merge",
    )(x, mod3, o_attn, g_f, g_b, u, y_f, y_b, xbc, u, u, *w_args)


def _router_gates(logits, bias):
    lane = lax.broadcasted_iota(jnp.int32, logits.shape, 1)
    neg = -jnp.inf
    per_group = N_EXPERTS // N_EXPERT_GROUPS
    scores = jax.nn.sigmoid(logits)
    sel = jnp.where(lane < N_EXPERTS, scores + bias, neg)

    def top2(v):
        m1 = jnp.max(v, axis=-1, keepdims=True)
        i1 = jnp.min(jnp.where(v == m1, lane, LANE), axis=-1, keepdims=True)
        v2 = jnp.where(lane == i1, neg, v)
        m2 = jnp.max(v2, axis=-1, keepdims=True)
        i2 = jnp.min(jnp.where(v2 == m2, lane, LANE), axis=-1, keepdims=True)
        return m1, i1, m2, i2

    best_v = best_g = None
    for g in range(N_EXPERT_GROUPS):
        in_g = jnp.logical_and(lane >= g * per_group, lane < (g + 1) * per_group)
        m1, _, m2, _ = top2(jnp.where(in_g, sel, neg))
        gs = m1 + m2
        if g == 0:
            best_v, best_g = gs, jnp.zeros_like(gs, dtype=jnp.int32)
        else:
            upd = gs > best_v
            best_g = jnp.where(upd, g, best_g)
            best_v = jnp.where(upd, gs, best_v)
    in_best = jnp.logical_and(lane // per_group == best_g, lane < N_EXPERTS)
    _, i1, _, i2 = top2(jnp.where(in_best, sel, neg))
    w1 = jnp.sum(jnp.where(lane == i1, scores, 0.0), axis=-1, keepdims=True)
    w2 = jnp.sum(jnp.where(lane == i2, scores, 0.0), axis=-1, keepdims=True)
    tot = w1 + w2
    return jnp.where(lane == i1, w1 / tot, 0.0) + jnp.where(lane == i2, w2 / tot, 0.0)


def _moe_kernel(x_ref, mod_ref, nw_ref, rwh_ref, rwl_ref, rb_ref, w1_ref, w3_ref, w2_ref, o_ref,
                h_scr, gate_scr, acc_scr):
    e = pl.program_id(1)
    d = D_MODEL

    @pl.when(e == 0)
    def _():
        x = x_ref[...]
        y = x * lax.rsqrt(jnp.mean(x * x, axis=-1, keepdims=True) + EPS) * nw_ref[...]
        h = y * (1.0 + mod_ref[:, 4 * d:5 * d]) + mod_ref[:, 3 * d:4 * d]
        h_hi, h_lo = _split2(h)
        h_scr[...] = h_hi
        rwh = rwh_ref[...]
        logits = _dot(h_hi, rwh) + _dot(h_lo, rwh) + _dot(h_hi, rwl_ref[...])
        gate_scr[...] = _router_gates(logits, rb_ref[...])
        acc_scr[...] = jnp.zeros(acc_scr.shape, F32)

    hb = h_scr[...]
    hid = _silu(_dot(hb, w1_ref[...].astype(BF16))) * _dot(hb, w3_ref[...].astype(BF16))
    gate = gate_scr[...]
    lane = lax.broadcasted_iota(jnp.int32, gate.shape, 1)
    gcol = jnp.sum(jnp.where(lane == e, gate, 0.0), axis=-1, keepdims=True)
    acc_scr[...] += _dot((hid * gcol).astype(BF16), w2_ref[...].astype(BF16))

    @pl.when(e == N_EXPERTS - 1)
    def _():
        o_ref[...] = x_ref[...] + mod_ref[:, 5 * d:6 * d] * acc_scr[...]


def _moe_call(x, mod3, nw, rwh, rwl, rb, w1, w3, w2, layer):
    tm = 1024
    d = D_MODEL
    full = lambda a: pl.BlockSpec(a.shape, lambda i, e: (0,) * a.ndim)
    return pl.pallas_call(
        _moe_kernel,
        grid=(N_TOK // tm, N_EXPERTS),
        in_specs=[pl.BlockSpec((tm, d), lambda i, e: (i, 0)),
                  pl.BlockSpec((None, 1, 6 * d), lambda i, e: (_mod_row(i, tm, layer), 0, 0)),
                  full(nw), full(rwh), full(rwl), full(rb),
                  pl.BlockSpec((None, None, d, D_EXPERT), lambda i, e: (layer, e, 0, 0)),
                  pl.BlockSpec((None, None, d, D_EXPERT), lambda i, e: (layer, e, 0, 0)),
                  pl.BlockSpec((None, None, D_EXPERT, d), lambda i, e: (layer, e, 0, 0))],
        out_specs=pl.BlockSpec((tm, d), lambda i, e: (i, 0)),
        out_shape=jax.ShapeDtypeStruct((N_TOK, d), F32),
        scratch_shapes=[pltpu.VMEM((tm, d), BF16), pltpu.VMEM((tm, LANE), F32), pltpu.VMEM((tm, d), F32)],
        compiler_params=pltpu.CompilerParams(dimension_semantics=("parallel", "arbitrary")),
        name="moe_dense",
    )(x, mod3, nw, rwh, rwl, rb, w1, w3, w2)


def _pack_w_in(w):
    p = _col_slices(w)
    z = lambda n: jnp.zeros((D_MODEL, n), w.dtype)
    main = jnp.concatenate([
        p['a_cq'], p['a_ckv'], z(A_NOPE), p['a_krope'][:, _ROPE_PERM], z(U_QK - U_KR - A_NOPE - A_ROPE),
        p['b_q'], p['b_k'], p['b_v'], p['b_og'], p['c_z'], p['c_xbc'][:, :C_INNER], p['gates'],
        p['c_xbc'][:, C_INNER:]], axis=1)
    small = jnp.concatenate([p['b_gf'], p['b_gb'], p['c_dtf'], p['c_dtb'], p['a_krope'],
                             z(LANE - S_KR - A_ROPE)], axis=1)
    return main.astype(BF16), small.astype(BF16)


def _pad_heads(t, real):
    pad = jnp.zeros(t.shape[:-1] + (HEAD_PAD - real,), t.dtype)
    t = jnp.concatenate([t, pad], axis=-1)
    return t.reshape(t.shape[:-2] + (A_HEADS * HEAD_PAD,))


def _qk_gain(g, scale):
    g = jnp.concatenate([g[:A_NOPE], g[A_NOPE:][_ROPE_PERM], jnp.zeros((HEAD_PAD - A_QK,), g.dtype)])
    return (g * scale).reshape(1, HEAD_PAD)


def _lane_slot(vals, off):
    return jnp.zeros((1, LANE), F32).at[0, off:off + vals.shape[0]].set(vals.astype(F32))


def _rope_tables():
    t = jnp.arange(DEC_SEQ)
    row = (t // GRID_W).astype(F32)
    col = (t % GRID_W).astype(F32)
    n_freq = A_ROPE // 4
    inv_freq = 1.0 / (ROPE_BASE ** (jnp.arange(n_freq, dtype=F32) / n_freq))
    ang = jnp.concatenate([row[:, None] * inv_freq, col[:, None] * inv_freq], axis=-1)
    cos, sin = jnp.cos(ang), jnp.sin(ang)
    ones = jnp.ones((DEC_SEQ, A_NOPE), F32)
    tail = jnp.ones((DEC_SEQ, HEAD_PAD - A_QK), F32)
    ctab = jnp.concatenate([ones, cos, cos, tail], axis=-1)
    stab = jnp.concatenate([0.0 * ones, -sin, sin, 0.0 * tail], axis=-1)
    return ctab, stab


def kernel(x_prompt, x_sample, cache_ckv, cache_krope, state_gla, state_ssd, c, c_ctx, w_ada, b_ada, norm1, norm2, w_in, a_q_norm, a_wq, a_kv_norm, a_wkv, a_qk_qnorm, a_qk_knorm, a_wo, b_wg, b_bg, b_onorm, b_wo, c_conv_w, c_conv_b, c_dt_bias, c_A_log, c_D, c_norm, c_wo, w_out, router_w, router_bias, e_w1, e_w3, e_w2):
    d = D_MODEL
    x = jnp.concatenate([x_prompt.reshape(N_CTX, d), x_sample.reshape(N_DEC, d)], axis=0)

    cvecs = jnp.zeros((8, d), F32).at[0].set(c_ctx).at[1:1 + DEC_BATCH].set(c)
    mod3 = _mod_call(cvecs, w_ada, b_ada).reshape(DEPTH * 8, 1, 6 * d)

    rope_tabs = _rope_tables()
    gla_np = (_gla_consts(False), _gla_consts(True))
    gla_mall = jnp.asarray(np.stack([gla_np[0][0], gla_np[1][0]]), BF16)
    gla_masks = jnp.asarray(np.stack([gla_np[0][1], gla_np[1][1]]))
    ssd_consts = _ssd_consts()
    rw = jnp.zeros((d, LANE), F32).at[:, :N_EXPERTS].set(router_w)
    rwh, rwl = _split2(rw)
    rb = _lane_slot(router_bias, 0)

    ckvs, kropes, glas, ssds = [], [], [], []
    for l in range(DEPTH):
        w_main, w_small = _pack_w_in(w_in[l])
        u, s_all = _norm_proj_call(x, mod3, norm1[l].reshape(1, d), w_main, w_small, l)
        kropes.append(s_all[:N_CTX, S_KR:S_KR + A_ROPE].reshape(BATCH, SEQ, A_ROPE))

        wq = a_wq[l].reshape(A_QLORA, A_HEADS, A_QK)
        wq = _pad_heads(jnp.concatenate([wq[..., :A_NOPE], wq[..., A_NOPE:][..., _ROPE_PERM]], axis=-1), A_QK)
        wkv = a_wkv[l].reshape(A_KVLORA, A_HEADS, A_NOPE + A_V)
        aw = dict(qnw=a_q_norm[l].reshape(1, A_QLORA), wq=wq.astype(BF16),
                  gq=_qk_gain(a_qk_qnorm[l], A_QK ** -0.5),
                  kvnw=a_kv_norm[l].reshape(1, A_KVLORA),
                  wkn=_pad_heads(wkv[..., :A_NOPE], A_NOPE).astype(BF16),
                  wv=_pad_heads(wkv[..., A_NOPE:], A_V).astype(BF16),
                  gk=_qk_gain(a_qk_knorm[l], 1.0))
        q_c, k_c, v_c, ckvn = _attn_prep_call(N_CTX, 0, (u,), aw, has_q=True, norm_kv=True, rope=False, emit_ckv=True)
        o_c = _attn_call(q_c, k_c, v_c, None, n_seq=BATCH, seq_len=SEQ, hps=A_HEADS, tq=SEQ)
        ckvs.append(ckvn.reshape(BATCH, SEQ, A_KVLORA))
        q_d, k_d, v_d = _attn_prep_call(N_DEC, N_CTX, (u,), aw, has_q=True, norm_kv=True, rope=True, emit_ckv=False,
                                        tables=rope_tabs)
        kr_x = jnp.zeros((DEC_BATCH * PAST_LEN, LANE), F32).at[:, A_NOPE:A_QK].set(
            cache_krope[:, l].reshape(DEC_BATCH * PAST_LEN, A_ROPE)[:, _ROPE_PERM])
        k_x, v_x = _attn_prep_call(DEC_BATCH * PAST_LEN, 0, (cache_ckv[:, l].reshape(DEC_BATCH * PAST_LEN, A_KVLORA), kr_x),
                                   aw, has_q=False, norm_kv=False, rope=False, emit_ckv=False)
        o_d = _attn_call(q_d, k_d, v_d, (k_x, v_x), n_seq=DEC_BATCH, seq_len=DEC_SEQ, hps=2, tq=512)

        hk = B_HEADS * B_DK
        wg = (jnp.zeros((2, LANE, hk), F32).at[0, S_GF:S_GF + B_GATE_RANK].set(b_wg[l, 0])
              .at[1, S_GB:S_GB + B_GATE_RANK].set(b_wg[l, 1]))
        wgh, wgl = _split2(wg)
        bg = b_bg[l].reshape(2, 1, hk)
        g_fc, g_bc, st_gc = _gla_call(u, s_all, wgh, wgl, bg, gla_mall, gla_masks, None,
                                      n_seq=BATCH, seq_len=SEQ, row0=0)
        g_fd, g_bd, _ = _gla_call(u, s_all, wgh, wgl, bg, gla_mall, gla_masks, jnp.swapaxes(state_gla[:, l], -1, -2),
                                  n_seq=DEC_BATCH, seq_len=DEC_SEQ, row0=N_CTX)
        glas.append(jnp.swapaxes(st_gc, -1, -2))

        w8 = jnp.zeros((8, C_XBC), F32).at[:C_CONV].set(c_conv_w[l])
        cb = c_conv_b[l].reshape(1, C_XBC)
        xbc_c = _conv_call(u, w8, cb, n_seq=BATCH, seq_len=SEQ, row0=0)
        xbc_d = _conv_call(u, w8, cb, n_seq=DEC_BATCH, seq_len=DEC_SEQ, row0=N_CTX)
        dtb = jnp.stack([_lane_slot(c_dt_bias[l, 0], S_DTF), _lane_slot(c_dt_bias[l, 1], S_DTB)])
        a_neg = -jnp.exp(c_A_log[l].astype(F32))
        aneg = jnp.stack([_lane_slot(a_neg[0], S_DTF), _lane_slot(a_neg[1], S_DTB)])
        y_fc, y_bc, st_c = _ssd_call(xbc_c, s_all, dtb, aneg, ssd_consts, None, n_seq=BATCH, seq_len=SEQ, row0=0)
        st0 = state_ssd[:, l].reshape(DEC_BATCH, 2, C_HEADS // 2, LANE, C_STATE)
        y_fd, y_bd, _ = _ssd_call(xbc_d, s_all, dtb, aneg, ssd_consts, st0, n_seq=DEC_BATCH, seq_len=DEC_SEQ, row0=N_CTX)
        ssds.append(st_c.reshape(BATCH, 2, C_HEADS, C_HEADDIM, C_STATE))

        awo = a_wo[l].reshape(A_HEADS, A_V, d)
        awo = jnp.concatenate([awo, jnp.zeros((A_HEADS, HEAD_PAD - A_V, d), awo.dtype)], axis=1)
        mw = dict(awo=awo.reshape(A_HEADS * HEAD_PAD, d).astype(BF16), onw=b_onorm[l].reshape(1, B_DV),
                  bwo=b_wo[l].astype(BF16), dexp=jnp.repeat(c_D[l], C_HEADDIM).reshape(1, C_INNER),
                  cnw=c_norm[l].reshape(1, C_INNER), cwo=c_wo[l].astype(BF16), wout=w_out[l].astype(BF16))
        x = _merge_call(x, mod3, u, o_c, g_fc, g_bc, y_fc, y_bc, xbc_c, mw, n_rows=N_CTX, row0=0, layer=l)
        x = _merge_call(x, mod3, u, o_d, g_fd, g_bd, y_fd, y_bd, xbc_d, mw, n_rows=N_DEC, row0=N_CTX, layer=l)

        x = _moe_call(x, mod3, norm2[l].reshape(1, d), rwh, rwl, rb,
                      e_w1, e_w3, e_w2, l)

    y_prompt = x[:N_CTX].reshape(BATCH, SEQ, d)
    y_sample = x[N_CTX:].reshape(DEC_BATCH, DEC_SEQ, d)
    new_ckv = jnp.stack(ckvs, axis=1)
    new_krope = jnp.stack(kropes, axis=1)
    new_state_gla = jnp.stack(glas, axis=1)
    new_state_ssd = jnp.stack(ssds, axis=1)
    return (y_prompt, y_sample, new_ckv, new_krope, new_state_gla, new_state_ssd)
```

```python
import functools
import math

import numpy as np
import jax
import jax.numpy as jnp
from jax import lax
from jax.experimental import pallas as pl
from jax.experimental.pallas import tpu as pltpu

F32 = jnp.float32
BF16 = jnp.bfloat16

D_MODEL = 1024
BATCH = 16
SEQ = 256
DEPTH = 2
DEC_BATCH = 2
DEC_SEQ = 2048
PAST_LEN = 512
GRID_W = 64
EPS = 1e-6
ROPE_BASE = 10000.0

A_HEADS = 16
A_NOPE = 64
A_ROPE = 32
A_QK = A_NOPE + A_ROPE
A_V = 64
A_QLORA = 512
A_KVLORA = 256

B_HEADS = 4
B_DK = 128
B_DV = 256
B_GATE_RANK = 16
B_GATE_NORM = 16.0

C_HEADS = 16
C_HEADDIM = 64
C_INNER = C_HEADS * C_HEADDIM
C_GROUPS = 2
C_STATE = 128
C_XBC = C_INNER + 2 * C_GROUPS * C_STATE
C_CONV = 5

N_EXPERTS = 16
N_EXPERT_GROUPS = 4
D_EXPERT = 512

N_CTX = BATCH * SEQ
N_DEC = DEC_BATCH * DEC_SEQ
N_TOK = N_CTX + N_DEC

LANE = 128
HEAD_PAD = 128
CHUNK = 128

U_CQ = 0
U_CKV = 512
U_KR = 768
U_QK = 1024
U_V = 2048
U_OG = 3072
U_Z = 4096
U_X = 5120
U_GATES = 6144
U_BC = 9216
U_WIDTH = 9728
S_GF, S_GB, S_DTF, S_DTB, S_KR = 0, 16, 32, 48, 64

_IN_SPLITS = (
    ('a_cq', A_QLORA), ('a_ckv', A_KVLORA), ('a_krope', A_ROPE),
    ('b_q', B_HEADS * B_DK), ('b_k', B_HEADS * B_DK), ('b_v', B_HEADS * B_DV),
    ('b_og', B_HEADS * B_DV), ('b_gf', B_GATE_RANK), ('b_gb', B_GATE_RANK),
    ('c_z', C_INNER), ('c_xbc', C_XBC), ('c_dtf', C_HEADS), ('c_dtb', C_HEADS),
    ('gates', 3 * D_MODEL),
)
_ROPE_PERM = np.concatenate([np.arange(0, A_ROPE, 2), np.arange(1, A_ROPE, 2)])


def _col_slices(w):
    parts, start = {}, 0
    for name, size in _IN_SPLITS:
        parts[name] = w[:, start:start + size]
        start += size
    return parts


def _dot(a, b):
    return jnp.dot(a, b, preferred_element_type=F32)


def _dot_nt(a, b):
    return lax.dot_general(a, b, (((1,), (1,)), ((), ())), preferred_element_type=F32)


def _dot_tn(a, b):
    return lax.dot_general(a, b, (((0,), (0,)), ((), ())), preferred_element_type=F32)


def _split2(x):
    hi = x.astype(BF16)
    lo = (x - hi.astype(F32)).astype(BF16)
    return hi, lo


def _silu(x):
    return x * jax.nn.sigmoid(x)


def _softplus(x):
    return jnp.maximum(x, 0.0) + jnp.log1p(jnp.exp(-jnp.abs(x)))


def _log_sigmoid(x):
    return jnp.minimum(x, 0.0) - jnp.log1p(jnp.exp(-jnp.abs(x)))


def _mod_kernel(c_ref, w_ref, b_ref, o_ref):
    s = _silu(c_ref[...]).astype(BF16)
    o_ref[...] = _dot(s, w_ref[...].astype(BF16)) + b_ref[...]


def _mod_call(cvecs, w_ada, b_ada):
    tn = 1536
    return pl.pallas_call(
        _mod_kernel,
        grid=(DEPTH, 6 * D_MODEL // tn),
        in_specs=[pl.BlockSpec((8, D_MODEL), lambda l, j: (0, 0)),
                  pl.BlockSpec((None, D_MODEL, tn), lambda l, j: (l, 0, j)),
                  pl.BlockSpec((None, 1, tn), lambda l, j: (l, 0, j))],
        out_specs=pl.BlockSpec((None, 8, tn), lambda l, j: (l, 0, j)),
        out_shape=jax.ShapeDtypeStruct((DEPTH, 8, 6 * D_MODEL), F32),
        name="adaln_mod",
    )(cvecs, w_ada, b_ada.reshape(DEPTH, 1, 6 * D_MODEL))


def _mod_row(i, tm, layer):
    n_ctx = N_CTX // tm
    per_b = DEC_SEQ // tm
    return layer * 8 + jnp.where(i < n_ctx, 0, 1 + (i - n_ctx) // per_b)


def _norm_proj_kernel(x_ref, mod_ref, nw_ref, w_ref, ws_ref, u_ref, s_ref, h_scr):
    @pl.when(pl.program_id(1) == 0)
    def _():
        x = x_ref[...]
        m = mod_ref[...]
        y = x * lax.rsqrt(jnp.mean(x * x, axis=-1, keepdims=True) + EPS) * nw_ref[...]
        h = (y * (1.0 + m[:, D_MODEL:2 * D_MODEL]) + m[:, 0:D_MODEL]).astype(BF16)
        h_scr[...] = h
        s_ref[...] = _dot(h, ws_ref[...])

    u_ref[...] = _dot(h_scr[...], w_ref[...]).astype(BF16)


def _norm_proj_call(x, mod3, nw, w_main, w_small, layer):
    tm, tn = 2048, 512
    return pl.pallas_call(
        _norm_proj_kernel,
        grid=(N_TOK // tm, U_WIDTH // tn),
        in_specs=[pl.BlockSpec((tm, D_MODEL), lambda i, j: (i, 0)),
                  pl.BlockSpec((None, 1, 6 * D_MODEL), lambda i, j: (_mod_row(i, tm, layer), 0, 0)),
                  pl.BlockSpec((1, D_MODEL), lambda i, j: (0, 0)),
                  pl.BlockSpec((D_MODEL, tn), lambda i, j: (0, j)),
                  pl.BlockSpec((D_MODEL, LANE), lambda i, j: (0, 0))],
        out_specs=[pl.BlockSpec((tm, tn), lambda i, j: (i, j)),
                   pl.BlockSpec((tm, LANE), lambda i, j: (i, 0))],
        out_shape=[jax.ShapeDtypeStruct((N_TOK, U_WIDTH), BF16),
                   jax.ShapeDtypeStruct((N_TOK, LANE), F32)],
        scratch_shapes=[pltpu.VMEM((tm, D_MODEL), BF16)],
        compiler_params=pltpu.CompilerParams(dimension_semantics=("parallel", "arbitrary")),
        name="norm_in_proj",
    )(x, mod3, nw, w_main, w_small)


def _head_norm_rope(t, gain, cos, sin, lane):
    ss = jnp.sum(t * t, axis=-1, keepdims=True)
    t = t * lax.rsqrt(ss * (1.0 / A_QK) + EPS) * gain
    if cos is not None:
        half = A_ROPE // 2
        swapped = jnp.where(lane < A_NOPE + half, pltpu.roll(t, HEAD_PAD - half, 1), pltpu.roll(t, half, 1))
        t = t * cos + swapped * sin
    return t


def _attn_prep_kernel(*refs, has_q, norm_kv, rope, emit_ckv):
    refs = list(refs)
    cq_ref = refs.pop(0) if has_q else None
    ckv_ref, kr_ref = refs.pop(0), refs.pop(0)
    if has_q:
        qnw_ref, wq_ref, gq_ref = refs.pop(0), refs.pop(0), refs.pop(0)
    kvnw_ref, wkn_ref, wv_ref, gk_ref = refs.pop(0), refs.pop(0), refs.pop(0), refs.pop(0)
    cos = sin = None
    if rope:
        cos, sin = refs.pop(0)[...], refs.pop(0)[...]
    q_ref = refs.pop(0) if has_q else None
    k_ref, v_ref = refs.pop(0), refs.pop(0)
    ckvn_ref = refs.pop(0) if emit_ckv else None

    tm = ckv_ref.shape[0]
    lane = lax.broadcasted_iota(jnp.int32, (tm, HEAD_PAD), 1)

    ckv = ckv_ref[...].astype(F32)
    if norm_kv:
        ckv = ckv * lax.rsqrt(jnp.mean(ckv * ckv, axis=-1, keepdims=True) + EPS) * kvnw_ref[...]
    if emit_ckv:
        ckvn_ref[...] = ckv
    ckv_b = ckv.astype(BF16)
    v_ref[...] = _dot(ckv_b, wv_ref[...]).astype(BF16)
    kf = _dot(ckv_b, wkn_ref[...])
    kr = kr_ref[...].astype(F32)
    gk = gk_ref[...]
    for h in range(A_HEADS):
        sl = slice(h * HEAD_PAD, (h + 1) * HEAD_PAD)
        k_ref[:, sl] = _head_norm_rope(kf[:, sl] + kr, gk, cos, sin, lane).astype(BF16)

    if has_q:
        cq = cq_ref[...].astype(F32)
        cq = cq * lax.rsqrt(jnp.mean(cq * cq, axis=-1, keepdims=True) + EPS) * qnw_ref[...]
        qf = _dot(cq.astype(BF16), wq_ref[...])
        gq = gq_ref[...]
        for h in range(A_HEADS):
            sl = slice(h * HEAD_PAD, (h + 1) * HEAD_PAD)
            q_ref[:, sl] = _head_norm_rope(qf[:, sl], gq, cos, sin, lane).astype(BF16)


def _attn_prep_call(n_rows, row0, srcs, wts, *, has_q, norm_kv, rope, emit_ckv, tables=None):
    tm = 256
    r0 = row0 // tm
    pw = A_HEADS * HEAD_PAD
    full = lambda shape: pl.BlockSpec(shape, lambda i: (0,) * len(shape))
    args, in_specs = [], []
    if len(srcs) == 1:
        u = srcs[0]
        if has_q:
            args.append(u)
            in_specs.append(pl.BlockSpec((tm, A_QLORA), lambda i: (i + r0, U_CQ // A_QLORA)))
        args += [u, u]
        in_specs += [pl.BlockSpec((tm, A_KVLORA), lambda i: (i + r0, U_CKV // A_KVLORA)),
                     pl.BlockSpec((tm, LANE), lambda i: (i + r0, U_KR // LANE))]
    else:
        args += list(srcs)
        in_specs += [pl.BlockSpec((tm, A_KVLORA), lambda i: (i, 0)),
                     pl.BlockSpec((tm, LANE), lambda i: (i, 0))]
    if has_q:
        args += [wts['qnw'], wts['wq'], wts['gq']]
        in_specs += [full((1, A_QLORA)), full((A_QLORA, pw)), full((1, HEAD_PAD))]
    args += [wts['kvnw'], wts['wkn'], wts['wv'], wts['gk']]
    in_specs += [full((1, A_KVLORA)), full((A_KVLORA, pw)), full((A_KVLORA, pw)), full((1, HEAD_PAD))]
    if rope:
        per_seq = DEC_SEQ // tm
        args += list(tables)
        in_specs += [pl.BlockSpec((tm, HEAD_PAD), lambda i: (i % per_seq, 0))] * 2
    out_specs, out_shape = [], []
    for _ in range((1 if has_q else 0) + 2):
        out_specs.append(pl.BlockSpec((tm, pw), lambda i: (i, 0)))
        out_shape.append(jax.ShapeDtypeStruct((n_rows, pw), BF16))
    if emit_ckv:
        out_specs.append(pl.BlockSpec((tm, A_KVLORA), lambda i: (i, 0)))
        out_shape.append(jax.ShapeDtypeStruct((n_rows, A_KVLORA), F32))
    return pl.pallas_call(
        functools.partial(_attn_prep_kernel, has_q=has_q, norm_kv=norm_kv, rope=rope, emit_ckv=emit_ckv),
        grid=(n_rows // tm,),
        in_specs=in_specs, out_specs=out_specs, out_shape=out_shape,
        compiler_params=pltpu.CompilerParams(dimension_semantics=("parallel",)),
        name="mla_prep",
    )(*args)


def _attn_kernel(*refs, hps, has_ctx):
    if has_ctx:
        q_ref, k_ref, v_ref, kc_ref, vc_ref, o_ref = refs
    else:
        q_ref, k_ref, v_ref, o_ref = refs

    def scores(hh):
        sl = slice(hh * HEAD_PAD, (hh + 1) * HEAD_PAD)
        q = q_ref[:, sl]
        return _dot_nt(q, k_ref[:, sl]), (_dot_nt(q, kc_ref[:, sl]) if has_ctx else None)

    nxt = scores(0)
    for hh in range(hps):
        sl = slice(hh * HEAD_PAD, (hh + 1) * HEAD_PAD)
        s, s2 = nxt
        if hh + 1 < hps:
            nxt = scores(hh + 1)
        m = jnp.max(s, axis=-1, keepdims=True)
        if has_ctx:
            m = jnp.maximum(m, jnp.max(s2, axis=-1, keepdims=True))
        p = jnp.exp(s - m)
        den = jnp.sum(p, axis=-1, keepdims=True)
        o = _dot(p.astype(BF16), v_ref[:, sl])
        if has_ctx:
            p2 = jnp.exp(s2 - m)
            den = den + jnp.sum(p2, axis=-1, keepdims=True)
            o = o + _dot(p2.astype(BF16), vc_ref[:, sl])
        o_ref[:, sl] = (o / den).astype(BF16)


def _attn_call(q, k, v, ctx_kv, *, n_seq, seq_len, hps, tq):
    pw = A_HEADS * HEAD_PAD
    bw = hps * HEAD_PAD
    nq = seq_len // tq
    in_specs = [pl.BlockSpec((tq, bw), lambda b, h, i: (b * nq + i, h)),
                pl.BlockSpec((seq_len, bw), lambda b, h, i: (b, h)),
                pl.BlockSpec((seq_len, bw), lambda b, h, i: (b, h))]
    args = [q, k, v]
    if ctx_kv is not None:
        in_specs += [pl.BlockSpec((PAST_LEN, bw), lambda b, h, i: (b, h))] * 2
        args += list(ctx_kv)
    return pl.pallas_call(
        functools.partial(_attn_kernel, hps=hps, has_ctx=ctx_kv is not None),
        grid=(n_seq, A_HEADS // hps, nq),
        in_specs=in_specs,
        out_specs=pl.BlockSpec((tq, bw), lambda b, h, i: (b * nq + i, h)),
        out_shape=jax.ShapeDtypeStruct((n_seq * seq_len, pw), BF16),
        compiler_params=pltpu.CompilerParams(dimension_semantics=("parallel", "parallel", "arbitrary")),
        name="mla_attention",
    )(*args)


def _conv_kernel(x_ref, w_ref, b_ref, o_ref):
    x = x_ref[...].astype(F32)
    n = x.shape[0]
    row = lax.broadcasted_iota(jnp.int32, x.shape, 0)
    half = (C_CONV - 1) // 2
    acc = x * w_ref[half:half + 1, :] + b_ref[...]
    for d in range(-half, half + 1):
        if d == 0:
            continue
        shifted = pltpu.roll(x, (-d) % n, 0)
        valid = jnp.logical_and(row + d >= 0, row + d < n)
        acc = acc + jnp.where(valid, shifted, 0.0) * w_ref[half + d:half + d + 1, :]
    o_ref[...] = _silu(acc).astype(BF16)


def _conv_call(u, w8, b, *, n_seq, seq_len, row0):
    tc = 512
    b0 = row0 // seq_len
    nx = C_INNER // tc
    ucol = lambda j: jnp.where(j < nx, U_X // tc + j, U_BC // tc + j - nx)
    return pl.pallas_call(
        _conv_kernel,
        grid=(n_seq, C_XBC // tc),
        in_specs=[pl.BlockSpec((seq_len, tc), lambda s, j: (s + b0, ucol(j))),
                  pl.BlockSpec((8, tc), lambda s, j: (0, j)),
                  pl.BlockSpec((1, tc), lambda s, j: (0, j))],
        out_specs=pl.BlockSpec((seq_len, tc), lambda s, j: (s, j)),
        out_shape=jax.ShapeDtypeStruct((n_seq * seq_len, C_XBC), BF16),
        compiler_params=pltpu.CompilerParams(dimension_semantics=("parallel", "parallel")),
        name="ssd_conv",
    )(u, w8, b)


def _ssd_consts():
    idx = np.arange(CHUNK)
    tri = np.stack([idx[None, :] <= idx[:, None], idx[None, :] >= idx[:, None]]).astype(np.float32)
    expand = np.zeros((2, LANE, C_INNER), np.float32)
    tile = np.zeros((2, LANE, C_HEADS * LANE), np.float32)
    for di, off in enumerate((S_DTF, S_DTB)):
        for h in range(C_HEADS):
            expand[di, off + h, h * C_HEADDIM:(h + 1) * C_HEADDIM] = 1.0
            tile[di, off + h, h * LANE:(h + 1) * LANE] = 1.0
    return jnp.asarray(tri, BF16), jnp.asarray(expand, BF16), jnp.asarray(tile, BF16)


def _ssd_kernel(*refs, has_init):
    refs = list(refs)
    xs = [(refs.pop(0), refs.pop(0), refs.pop(0)) for _ in range(2)]
    dtb_ref, an_ref, tri_ref, exp_ref, tile_ref = (refs.pop(0) for _ in range(5))
    st0_ref = refs.pop(0) if has_init else None
    y_refs = (refs.pop(0), refs.pop(0))
    stf_ref, st_scr = refs

    @pl.when(pl.program_id(1) == 0)
    def _():
        st_scr[...] = st0_ref[...] if has_init else jnp.zeros(st_scr.shape, F32)

    t = CHUNK
    ri = lax.broadcasted_iota(jnp.int32, (t, t), 0)
    ci = lax.broadcasted_iota(jnp.int32, (t, t), 1)
    prow = lax.broadcasted_iota(jnp.int32, (LANE, C_STATE), 0)
    gs = C_GROUPS * C_STATE
    pairs_per_group = C_HEADS // C_GROUPS // 2
    for di in range(2):
        x_ref, bc_ref, s_ref = xs[di]
        y_ref = y_refs[di]
        off = (S_DTF, S_DTB)[di]
        dt = _softplus(s_ref[...] + dtb_ref[di])
        a = dt * an_ref[di]
        a_hi, a_lo = _split2(a)
        tri = tri_ref[di]
        acum = _dot(tri, a_hi) + _dot(tri, a_lo)
        acum_t = acum.T
        dt_t = dt.T
        edge = 0 if di else t - 1
        atot = acum[edge:edge + 1, :]
        causal = (ci >= ri) if di else (ci <= ri)
        dec_all = jnp.exp(atot)
        e_exp = _dot(jnp.exp(acum).astype(BF16), exp_ref[di])
        wj_exp = _dot((jnp.exp(atot - acum) * dt).astype(BF16), exp_ref[di])
        c_hi, c_lo = _split2(acum)
        acol = _dot(c_hi, tile_ref[di]) + _dot(c_lo, tile_ref[di])
        bc = bc_ref[...]
        cb, bmat, cmat = [], [], []
        for g in range(C_GROUPS):
            bmat.append(bc[:, g * C_STATE:(g + 1) * C_STATE])
            cmat.append(bc[:, gs + g * C_STATE:gs + (g + 1) * C_STATE])
            cb.append(_dot_nt(cmat[g], bmat[g]))
        lane = lax.broadcasted_iota(jnp.int32, (t, LANE), 1)
        low = lane < C_HEADDIM
        for m in range(C_HEADS // 2):
            g = m // pairs_per_group
            ps = slice(m * LANE, (m + 1) * LANE)
            xp = x_ref[:, ps]
            y_diag = []
            for hh in range(2):
                h = 2 * m + hh
                la = off + h
                lm = jnp.exp(jnp.where(causal, acol[:, h * LANE:(h + 1) * LANE] - acum_t[la:la + 1, :], -jnp.inf))
                w = (cb[g] * lm * dt_t[la:la + 1, :]).astype(BF16)
                y_diag.append(_dot(w, xp))
            st = st_scr[di, m]
            y_off = _dot_nt(cmat[g], st.astype(BF16)) * e_exp[:, ps]
            y_ref[:, ps] = (jnp.where(low, y_diag[0], y_diag[1]) + y_off).astype(BF16)
            xw = (xp.astype(F32) * wj_exp[:, ps]).astype(BF16)
            la = off + 2 * m
            dec = jnp.where(prow < C_HEADDIM, dec_all[:, la:la + 1], dec_all[:, la + 1:la + 2])
            st_scr[di, m] = dec * st + _dot_tn(xw, bmat[g])

    @pl.when(pl.program_id(1) == pl.num_programs(1) - 1)
    def _():
        stf_ref[...] = st_scr[...]


def _ssd_call(xbc, s_all, dtb, aneg, consts, st0, *, n_seq, seq_len, row0):
    t = CHUNK
    nch = seq_len // t
    r0 = row0 // t
    npair = C_HEADS // 2
    bcw = 2 * C_GROUPS * C_STATE
    tri, expand, tile = consts
    full = lambda a: pl.BlockSpec(a.shape, lambda b, c: (0,) * a.ndim)
    in_specs, args = [], []
    for cidx in (lambda c: c, lambda c: nch - 1 - c):
        in_specs += [pl.BlockSpec((t, C_INNER), lambda b, c, cidx=cidx: (b * nch + cidx(c), 0)),
                     pl.BlockSpec((t, bcw), lambda b, c, cidx=cidx: (b * nch + cidx(c), C_INNER // bcw)),
                     pl.BlockSpec((t, LANE), lambda b, c, cidx=cidx: (r0 + b * nch + cidx(c), 0))]
        args += [xbc, xbc, s_all]
    in_specs += [full(dtb), full(aneg), full(tri), full(expand), full(tile)]
    args += [dtb, aneg, tri, expand, tile]
    st_spec = pl.BlockSpec((None, 2, npair, LANE, C_STATE), lambda b, c: (b, 0, 0, 0, 0))
    if st0 is not None:
        in_specs.append(st_spec)
        args.append(st0)
    return pl.pallas_call(
        functools.partial(_ssd_kernel, has_init=st0 is not None),
        grid=(n_seq, nch),
        in_specs=in_specs,
        out_specs=[pl.BlockSpec((t, C_INNER), lambda b, c: (b * nch + c, 0)),
                   pl.BlockSpec((t, C_INNER), lambda b, c: (b * nch + nch - 1 - c, 0)),
                   st_spec],
        out_shape=[jax.ShapeDtypeStruct((n_seq * seq_len, C_INNER), BF16),
                   jax.ShapeDtypeStruct((n_seq * seq_len, C_INNER), BF16),
                   jax.ShapeDtypeStruct((n_seq, 2, npair, LANE, C_STATE), F32)],
        scratch_shapes=[pltpu.VMEM((2, npair, LANE, C_STATE), F32)],
        compiler_params=pltpu.CompilerParams(dimension_semantics=("parallel", "arbitrary")),
        name="ssd_scan",
    )(*args)


def _gla_consts(bwd):
    t = CHUNK
    nlev = int(math.log2(t))
    idx = np.arange(t)
    mats = np.zeros((nlev + 2, t, t), np.float32)
    masks = np.zeros((nlev + 1, t, t), np.float32)
    masks[0] = np.eye(t)
    for lvl in range(nlev):
        s = 1 << lvl
        blk = idx // (2 * s)
        upper = (idx % (2 * s)) >= s
        last_low = blk * 2 * s + s - 1
        rowtok = ~upper if bwd else upper
        for i in range(t):
            r = last_low[i]
            if not bwd:
                if upper[i]:
                    mats[lvl, i, r + 1:i + 1] = 1.0
                else:
                    mats[lvl, i, i + 1:r + 1] = 1.0
            else:
                if upper[i]:
                    mats[lvl, i, r + 1:i] = 1.0
                else:
                    mats[lvl, i, i:r + 1] = 1.0
        masks[lvl + 1] = ((blk[:, None] == blk[None, :]) & rowtok[:, None] & (~rowtok)[None, :])
    incl = (idx[None, :] >= idx[:, None]) if bwd else (idx[None, :] <= idx[:, None])
    mats[nlev] = incl
    mats[nlev + 1] = 1.0 - incl
    return mats.reshape((nlev + 2) * t, t), masks


def _gla_kernel(*refs, has_init):
    refs = list(refs)
    xs = [(refs.pop(0), refs.pop(0), refs.pop(0)) for _ in range(2)]
    wgh_ref, wgl_ref, bg_ref, mall_ref, mask_ref = (refs.pop(0) for _ in range(5))
    st0_ref = refs.pop(0) if has_init else None
    o_refs = (refs.pop(0), refs.pop(0))
    stf_ref, st_scr = refs

    @pl.when(pl.program_id(1) == 0)
    def _():
        st_scr[...] = st0_ref[...] if has_init else jnp.zeros(st_scr.shape, F32)

    t = CHUNK
    nlev = mask_ref.shape[1] - 1
    hk = B_HEADS * B_DK
    e_alls = []
    for di in range(2):
        s_hi, s_lo = _split2(xs[di][2][...])
        wgh = wgh_ref[di]
        logit = _dot(s_hi, wgh) + _dot(s_lo, wgh) + _dot(s_hi, wgl_ref[di]) + bg_ref[di]
        g_all = _log_sigmoid(logit) * (1.0 / B_GATE_NORM)
        e_alls.append(jnp.exp(_dot(mall_ref[di], g_all.astype(BF16))))
    for h in range(B_HEADS):
        for di in range(2):
            qk_ref, v_ref, _ = xs[di]
            o_ref = o_refs[di]
            e_all = e_alls[di]
            edge = 0 if di else t - 1
            ks = slice(h * B_DK, (h + 1) * B_DK)
            vs = slice(h * B_DV, (h + 1) * B_DV)
            q = qk_ref[:, ks].astype(F32) * (B_DK ** -0.5)
            k = qk_ref[:, hk + h * B_DK:hk + (h + 1) * B_DK]
            kf = k.astype(F32)
            v = v_ref[:, vs]
            amat = _dot_nt(q.astype(BF16), k) * mask_ref[di, 0]
            for lvl in range(nlev):
                e = e_all[lvl * t:(lvl + 1) * t, ks]
                amat = amat + _dot_nt((q * e).astype(BF16), (kf * e).astype(BF16)) * mask_ref[di, lvl + 1]
            e_in = e_all[nlev * t:(nlev + 1) * t, ks]
            e_out = e_all[(nlev + 1) * t:(nlev + 2) * t, ks]
            st = st_scr[di, h]
            o = _dot(amat.astype(BF16), v) + _dot_nt((q * e_in).astype(BF16), st.astype(BF16))
            o_ref[:, vs] = o.astype(BF16)
            st_scr[di, h] = st * e_in[edge:edge + 1, :] + _dot_tn(v, (kf * e_out).astype(BF16))

    @pl.when(pl.program_id(1) == pl.num_programs(1) - 1)
    def _():
        stf_ref[...] = st_scr[...]


def _gla_call(u, s_all, wgh, wgl, bg, mall, masks, st0, *, n_seq, seq_len, row0):
    t = CHUNK
    nch = seq_len // t
    r0 = row0 // t
    hk, hv = B_HEADS * B_DK, B_HEADS * B_DV
    full = lambda a: pl.BlockSpec(a.shape, lambda b, c: (0,) * a.ndim)
    in_specs, args = [], []
    for cidx in (lambda c: c, lambda c: nch - 1 - c):
        in_specs += [pl.BlockSpec((t, 2 * hk), lambda b, c, cidx=cidx: (r0 + b * nch + cidx(c), U_QK // (2 * hk))),
                     pl.BlockSpec((t, hv), lambda b, c, cidx=cidx: (r0 + b * nch + cidx(c), U_V // hv)),
                     pl.BlockSpec((t, LANE), lambda b, c, cidx=cidx: (r0 + b * nch + cidx(c), 0))]
        args += [u, u, s_all]
    in_specs += [full(wgh), full(wgl), full(bg), full(mall), full(masks)]
    args += [wgh, wgl, bg, mall, masks]
    st_spec = pl.BlockSpec((None, 2, B_HEADS, B_DV, B_DK), lambda b, c: (b, 0, 0, 0, 0))
    if st0 is not None:
        in_specs.append(st_spec)
        args.append(st0)
    return pl.pallas_call(
        functools.partial(_gla_kernel, has_init=st0 is not None),
        grid=(n_seq, nch),
        in_specs=in_specs,
        out_specs=[pl.BlockSpec((t, hv), lambda b, c: (b * nch + c, 0)),
                   pl.BlockSpec((t, hv), lambda b, c: (b * nch + nch - 1 - c, 0)),
                   st_spec],
        out_shape=[jax.ShapeDtypeStruct((n_seq * seq_len, hv), BF16),
                   jax.ShapeDtypeStruct((n_seq * seq_len, hv), BF16),
                   jax.ShapeDtypeStruct((n_seq, 2, B_HEADS, B_DV, B_DK), F32)],
        scratch_shapes=[pltpu.VMEM((2, B_HEADS, B_DV, B_DK), F32)],
        compiler_params=pltpu.CompilerParams(dimension_semantics=("parallel", "arbitrary")),
        name="gla_scan",
    )(*args)


def _merge_kernel(x_ref, mod_ref, oa_ref, gf_ref, gb_ref, og_ref, yf_ref, yb_ref, xc_ref, z_ref, gt_ref,
                  awo_ref, onw_ref, bwo_ref, dexp_ref, cnw_ref, cwo_ref, wout_ref, o_ref):
    o_a = _dot(oa_ref[...], awo_ref[...])

    og = gf_ref[...].astype(F32) + gb_ref[...].astype(F32)
    onw = onw_ref[...]
    parts = []
    for h in range(B_HEADS):
        th = og[:, h * B_DV:(h + 1) * B_DV]
        parts.append(th * lax.rsqrt(jnp.mean(th * th, axis=-1, keepdims=True) + EPS) * onw)
    ob_in = jnp.concatenate(parts, axis=-1) * _silu(og_ref[...].astype(F32))
    o_b = _dot(ob_in.astype(BF16), bwo_ref[...])

    y = yf_ref[...].astype(F32) + yb_ref[...].astype(F32) + dexp_ref[...] * xc_ref[...].astype(F32)
    y = y * _silu(z_ref[...].astype(F32))
    y = y * lax.rsqrt(jnp.mean(y * y, axis=-1, keepdims=True) + EPS) * cnw_ref[...]
    o_c = _dot(y.astype(BF16), cwo_ref[...])

    d = D_MODEL
    merged = (jax.nn.sigmoid(gt_ref[:, 0:d].astype(F32)) * o_a
              + jax.nn.sigmoid(gt_ref[:, d:2 * d].astype(F32)) * o_b
              + jax.nn.sigmoid(gt_ref[:, 2 * d:3 * d].astype(F32)) * o_c)
    out = _dot(merged.astype(BF16), wout_ref[...])
    o_ref[...] = x_ref[...] + mod_ref[:, 2 * d:3 * d] * out


def _merge_call(x, mod3, u, o_attn, g_f, g_b, y_f, y_b, xbc, wts, *, n_rows, row0, layer):
    tm = 512
    r0 = row0 // tm
    d = D_MODEL
    loc = lambda w: pl.BlockSpec((tm, w), lambda i: (i, 0))
    full = lambda a: pl.BlockSpec(a.shape, lambda i: (0,) * a.ndim)
    in_specs = [pl.BlockSpec((tm, d), lambda i: (i + r0, 0)),
                pl.BlockSpec((None, 1, 6 * d), lambda i: (_mod_row(i + r0, tm, layer), 0, 0)),
                loc(A_HEADS * HEAD_PAD), loc(d), loc(d),
                pl.BlockSpec((tm, d), lambda i: (i + r0, U_OG // d)),
                loc(d), loc(d),
                pl.BlockSpec((tm, d), lambda i: (i, 0)),
                pl.BlockSpec((tm, d), lambda i: (i + r0, U_Z // d)),
                pl.BlockSpec((tm, 3 * d), lambda i: (i + r0, U_GATES // (3 * d)))]
    w_args = [wts[n] for n in ('awo', 'onw', 'bwo', 'dexp', 'cnw', 'cwo', 'wout')]
    in_specs += [full(a) for a in w_args]
    return pl.pallas_call(
        _merge_kernel,
        grid=(n_rows // tm,),
        in_specs=in_specs,
        out_specs=pl.BlockSpec((tm, d), lambda i: (i + r0, 0)),
        out_shape=jax.ShapeDtypeStruct((N_TOK, d), F32),
        input_output_aliases={0: 0},
        compiler_params=pltpu.CompilerParams(dimension_semantics=("parallel",)),
        name="mixer_merge",
    )(x, mod3, o_attn, g_f, g_b, u, y_f, y_b, xbc, u, u, *w_args)


MOE_TILE = 256
MOE_TOKEN_TILE = 512
MOE_MAX_TILES = 2 * N_TOK // MOE_TILE + N_EXPERTS
MOE_ROWS = MOE_MAX_TILES * MOE_TILE
R_E1, R_E2, R_W1, R_W2, R_P1, R_P2 = 0, 1, 2, 3, 4, 5


def _router_top2(logits, bias):
    lane = lax.broadcasted_iota(jnp.int32, logits.shape, 1)
    neg = -jnp.inf
    per_group = N_EXPERTS // N_EXPERT_GROUPS
    scores = jax.nn.sigmoid(logits)
    sel = jnp.where(lane < N_EXPERTS, scores + bias, neg)

    def top2(v):
        m1 = jnp.max(v, axis=-1, keepdims=True)
        i1 = jnp.min(jnp.where(v == m1, lane, LANE), axis=-1, keepdims=True)
        v2 = jnp.where(lane == i1, neg, v)
        m2 = jnp.max(v2, axis=-1, keepdims=True)
        i2 = jnp.min(jnp.where(v2 == m2, lane, LANE), axis=-1, keepdims=True)
        return m1, i1, m2, i2

    best_v = best_g = None
    for g in range(N_EXPERT_GROUPS):
        in_g = jnp.logical_and(lane >= g * per_group, lane < (g + 1) * per_group)
        m1, _, m2, _ = top2(jnp.where(in_g, sel, neg))
        gs = m1 + m2
        if g == 0:
            best_v, best_g = gs, jnp.zeros_like(gs, dtype=jnp.int32)
        else:
            upd = gs > best_v
            best_g = jnp.where(upd, g, best_g)
            best_v = jnp.where(upd, gs, best_v)
    in_best = jnp.logical_and(lane // per_group == best_g, lane < N_EXPERTS)
    _, i1, _, i2 = top2(jnp.where(in_best, sel, neg))
    w1 = jnp.sum(jnp.where(lane == i1, scores, 0.0), axis=-1, keepdims=True)
    w2 = jnp.sum(jnp.where(lane == i2, scores, 0.0), axis=-1, keepdims=True)
    tot = w1 + w2
    return i1, i2, w1 / tot, w2 / tot


def _route_kernel(x_ref, mod_ref, nw_ref, rwh_ref, rwl_ref, rb_ref, tri_ref, h_ref, route_ref, cnt_ref, base_scr):
    d = D_MODEL

    @pl.when(pl.program_id(0) == 0)
    def _():
        base_scr[...] = jnp.zeros(base_scr.shape, F32)

    x = x_ref[...]
    y = x * lax.rsqrt(jnp.mean(x * x, axis=-1, keepdims=True) + EPS) * nw_ref[...]
    h = y * (1.0 + mod_ref[:, 4 * d:5 * d]) + mod_ref[:, 3 * d:4 * d]
    h_ref[...] = h
    h_hi, h_lo = _split2(h)
    rwh = rwh_ref[...]
    logits = _dot(h_hi, rwh) + _dot(h_lo, rwh) + _dot(h_hi, rwl_ref[...])
    i1, i2, w1, w2 = _router_top2(logits, rb_ref[...])
    lane = lax.broadcasted_iota(jnp.int32, logits.shape, 1)
    oh1, oh2 = lane == i1, lane == i2
    hit = jnp.where(jnp.logical_or(oh1, oh2), 1.0, 0.0)
    rank = _dot(tri_ref[...], hit.astype(BF16)) + base_scr[...]
    p1 = jnp.sum(jnp.where(oh1, rank, 0.0), axis=-1, keepdims=True)
    p2 = jnp.sum(jnp.where(oh2, rank, 0.0), axis=-1, keepdims=True)
    base_scr[...] += jnp.sum(hit, axis=0, keepdims=True)
    rec = jnp.zeros(logits.shape, F32)
    for ln, val in ((R_E1, i1.astype(F32)), (R_E2, i2.astype(F32)), (R_W1, w1), (R_W2, w2), (R_P1, p1), (R_P2, p2)):
        rec = jnp.where(lane == ln, val, rec)
    route_ref[...] = rec
    cnt_ref[...] = base_scr[...]


def _route_call(x, mod3, nw, rwh, rwl, rb, tri, layer):
    tm = tri.shape[0]
    d = D_MODEL
    full = lambda a: pl.BlockSpec(a.shape, lambda i: (0,) * a.ndim)
    return pl.pallas_call(
        _route_kernel,
        grid=(N_TOK // tm,),
        in_specs=[pl.BlockSpec((tm, d), lambda i: (i, 0)),
                  pl.BlockSpec((None, 1, 6 * d), lambda i: (_mod_row(i, tm, layer), 0, 0)),
                  full(nw), full(rwh), full(rwl), full(rb), full(tri)],
        out_specs=[pl.BlockSpec((tm, d), lambda i: (i, 0)),
                   pl.BlockSpec((tm, LANE), lambda i: (i, 0)),
                   pl.BlockSpec((1, LANE), lambda i: (0, 0))],
        out_shape=[jax.ShapeDtypeStruct((N_TOK, d), F32),
                   jax.ShapeDtypeStruct((N_TOK, LANE), F32),
                   jax.ShapeDtypeStruct((1, LANE), F32)],
        scratch_shapes=[pltpu.VMEM((1, LANE), F32)],
        compiler_params=pltpu.CompilerParams(dimension_semantics=("arbitrary",)),
        name="moe_route",
    )(x, mod3, nw, rwh, rwl, rb, tri)


def _row_copy(src, src_row, dst, dst_row, sem):
    return pltpu.make_async_copy(src.at[pl.ds(src_row, 1), :], dst.at[pl.ds(dst_row, 1), :], sem)


def _dispatch_kernel(dst_ref, h_ref, xs_in_ref, xs_ref, sem):
    del xs_in_ref
    tm = h_ref.shape[0]

    def issue(r, carry):
        _row_copy(h_ref, r, xs_ref, dst_ref[0, r], sem).start()
        _row_copy(h_ref, r, xs_ref, dst_ref[0, tm + r], sem).start()
        return carry

    def drain(r, carry):
        _row_copy(h_ref, 0, xs_ref, 0, sem).wait()
        _row_copy(h_ref, 0, xs_ref, 0, sem).wait()
        return carry

    lax.fori_loop(0, tm, issue, 0, unroll=8)
    lax.fori_loop(0, tm, drain, 0, unroll=8)


def _dispatch_call(dst, h, xs0):
    tm = dst.shape[-1] // 2
    d = D_MODEL
    return pl.pallas_call(
        _dispatch_kernel,
        grid=(N_TOK // tm,),
        in_specs=[pl.BlockSpec((None, 1, 2 * tm), lambda i: (i, 0, 0), memory_space=pltpu.SMEM),
                  pl.BlockSpec((tm, d), lambda i: (i, 0)),
                  pl.BlockSpec(memory_space=pl.ANY)],
        out_specs=pl.BlockSpec(memory_space=pl.ANY),
        out_shape=jax.ShapeDtypeStruct((MOE_ROWS, d), F32),
        scratch_shapes=[pltpu.SemaphoreType.DMA(())],
        input_output_aliases={2: 0},
        compiler_params=pltpu.CompilerParams(dimension_semantics=("arbitrary",)),
        name="moe_dispatch",
    )(dst, h, xs0)


def _expert_kernel(te_ref, nt_ref, xs_ref, w1_ref, w3_ref, w2_ref, y_ref, w1b, w3b, w2b):
    t = pl.program_id(0)
    changed = jnp.logical_or(t == 0, te_ref[t] != te_ref[jnp.maximum(t - 1, 0)])

    @pl.when(changed)
    def _():
        w1b[...] = w1_ref[...].astype(BF16)
        w3b[...] = w3_ref[...].astype(BF16)
        w2b[...] = w2_ref[...].astype(BF16)

    @pl.when(t < nt_ref[0])
    def _():
        xb = xs_ref[...].astype(BF16)
        hid = _silu(_dot(xb, w1b[...])) * _dot(xb, w3b[...])
        y_ref[...] = _dot(hid.astype(BF16), w2b[...])

    @pl.when(t >= nt_ref[0])
    def _():
        y_ref[...] = jnp.zeros(y_ref.shape, F32)


def _expert_call(tile_expert, n_tiles, xs, w1, w3, w2, layer):
    d = D_MODEL
    grid_spec = pltpu.PrefetchScalarGridSpec(
        num_scalar_prefetch=2,
        grid=(MOE_MAX_TILES,),
        in_specs=[pl.BlockSpec((MOE_TILE, d), lambda t, te, nt: (jnp.minimum(t, nt[0] - 1), 0)),
                  pl.BlockSpec((None, None, d, D_EXPERT), lambda t, te, nt: (layer, te[t], 0, 0)),
                  pl.BlockSpec((None, None, d, D_EXPERT), lambda t, te, nt: (layer, te[t], 0, 0)),
                  pl.BlockSpec((None, None, D_EXPERT, d), lambda t, te, nt: (layer, te[t], 0, 0))],
        out_specs=pl.BlockSpec((MOE_TILE, d), lambda t, te, nt: (t, 0)),
        scratch_shapes=[pltpu.VMEM((d, D_EXPERT), BF16), pltpu.VMEM((d, D_EXPERT), BF16),
                        pltpu.VMEM((D_EXPERT, d), BF16)])
    return pl.pallas_call(
        _expert_kernel,
        grid_spec=grid_spec,
        out_shape=jax.ShapeDtypeStruct((MOE_ROWS, d), F32),
        compiler_params=pltpu.CompilerParams(dimension_semantics=("arbitrary",)),
        name="moe_experts",
    )(tile_expert, n_tiles, xs, w1, w3, w2)


def _combine_kernel(dst_ref, x_ref, mod_ref, route_ref, y_ref, o_ref, b1, b2, sem):
    tm = x_ref.shape[0]
    d = D_MODEL

    def issue(r, carry):
        _row_copy(y_ref, dst_ref[0, r], b1, r, sem).start()
        _row_copy(y_ref, dst_ref[0, tm + r], b2, r, sem).start()
        return carry

    def drain(r, carry):
        _row_copy(y_ref, 0, b1, 0, sem).wait()
        _row_copy(y_ref, 0, b2, 0, sem).wait()
        return carry

    lax.fori_loop(0, tm, issue, 0, unroll=8)
    lax.fori_loop(0, tm, drain, 0, unroll=8)
    rec = route_ref[...]
    moe = rec[:, R_W1:R_W1 + 1] * b1[...] + rec[:, R_W2:R_W2 + 1] * b2[...]
    o_ref[...] = x_ref[...] + mod_ref[:, 5 * d:6 * d] * moe


def _combine_call(dst, x, mod3, route, y, layer):
    tm = dst.shape[-1] // 2
    d = D_MODEL
    return pl.pallas_call(
        _combine_kernel,
        grid=(N_TOK // tm,),
        in_specs=[pl.BlockSpec((None, 1, 2 * tm), lambda i: (i, 0, 0), memory_space=pltpu.SMEM),
                  pl.BlockSpec((tm, d), lambda i: (i, 0)),
                  pl.BlockSpec((None, 1, 6 * d), lambda i: (_mod_row(i, tm, layer), 0, 0)),
                  pl.BlockSpec((tm, LANE), lambda i: (i, 0)),
                  pl.BlockSpec(memory_space=pl.ANY)],
        out_specs=pl.BlockSpec((tm, d), lambda i: (i, 0)),
        out_shape=jax.ShapeDtypeStruct((N_TOK, d), F32),
        scratch_shapes=[pltpu.VMEM((tm, d), F32), pltpu.VMEM((tm, d), F32), pltpu.SemaphoreType.DMA(())],
        compiler_params=pltpu.CompilerParams(dimension_semantics=("arbitrary",)),
        name="moe_combine",
    )(dst, x, mod3, route, y)


def _moe_routed(x, mod3, nw, rwh, rwl, rb, tri, w1, w3, w2, layer):
    h, route, cnt = _route_call(x, mod3, nw, rwh, rwl, rb, tri, layer)
    counts = cnt[0, :N_EXPERTS].astype(jnp.int32)
    ntile = (counts + MOE_TILE - 1) // MOE_TILE
    tend = jnp.cumsum(ntile)
    row0 = ((tend - ntile) * MOE_TILE).astype(F32)
    n_tiles = tend[-1:]
    tids = jnp.arange(MOE_MAX_TILES)
    tile_expert = jnp.minimum(jnp.sum(tend[None, :] <= jnp.minimum(tids, n_tiles[0] - 1)[:, None], axis=1),
                              N_EXPERTS - 1).astype(jnp.int32)
    eids = jnp.arange(N_EXPERTS, dtype=F32)

    def dest(e, p):
        return (jnp.sum(jnp.where(e[:, None] == eids[None, :], row0[None, :], 0.0), axis=1) + p).astype(jnp.int32)

    tm = MOE_TOKEN_TILE
    dst = jnp.concatenate([dest(route[:, R_E1], route[:, R_P1]).reshape(N_TOK // tm, 1, tm),
                           dest(route[:, R_E2], route[:, R_P2]).reshape(N_TOK // tm, 1, tm)], axis=-1)
    xs = _dispatch_call(dst, h, jnp.zeros((MOE_ROWS, D_MODEL), F32))
    y = _expert_call(tile_expert, n_tiles.astype(jnp.int32), xs, w1, w3, w2, layer)
    return _combine_call(dst, x, mod3, route, y, layer)


def _pack_w_in(w):
    p = _col_slices(w)
    z = lambda n: jnp.zeros((D_MODEL, n), w.dtype)
    main = jnp.concatenate([
        p['a_cq'], p['a_ckv'], z(A_NOPE), p['a_krope'][:, _ROPE_PERM], z(U_QK - U_KR - A_NOPE - A_ROPE),
        p['b_q'], p['b_k'], p['b_v'], p['b_og'], p['c_z'], p['c_xbc'][:, :C_INNER], p['gates'],
        p['c_xbc'][:, C_INNER:]], axis=1)
    small = jnp.concatenate([p['b_gf'], p['b_gb'], p['c_dtf'], p['c_dtb'], p['a_krope'],
                             z(LANE - S_KR - A_ROPE)], axis=1)
    return main.astype(BF16), small.astype(BF16)


def _pad_heads(t, real):
    pad = jnp.zeros(t.shape[:-1] + (HEAD_PAD - real,), t.dtype)
    t = jnp.concatenate([t, pad], axis=-1)
    return t.reshape(t.shape[:-2] + (A_HEADS * HEAD_PAD,))


def _qk_gain(g, scale):
    g = jnp.concatenate([g[:A_NOPE], g[A_NOPE:][_ROPE_PERM], jnp.zeros((HEAD_PAD - A_QK,), g.dtype)])
    return (g * scale).reshape(1, HEAD_PAD)


def _lane_slot(vals, off):
    return jnp.zeros((1, LANE), F32).at[0, off:off + vals.shape[0]].set(vals.astype(F32))


def _rope_tables():
    t = jnp.arange(DEC_SEQ)
    row = (t // GRID_W).astype(F32)
    col = (t % GRID_W).astype(F32)
    n_freq = A_ROPE // 4
    inv_freq = 1.0 / (ROPE_BASE ** (jnp.arange(n_freq, dtype=F32) / n_freq))
    ang = jnp.concatenate([row[:, None] * inv_freq, col[:, None] * inv_freq], axis=-1)
    cos, sin = jnp.cos(ang), jnp.sin(ang)
    ones = jnp.ones((DEC_SEQ, A_NOPE), F32)
    tail = jnp.ones((DEC_SEQ, HEAD_PAD - A_QK), F32)
    ctab = jnp.concatenate([ones, cos, cos, tail], axis=-1)
    stab = jnp.concatenate([0.0 * ones, -sin, sin, 0.0 * tail], axis=-1)
    return ctab, stab


def kernel(x_prompt, x_sample, cache_ckv, cache_krope, state_gla, state_ssd, c, c_ctx, w_ada, b_ada, norm1, norm2, w_in, a_q_norm, a_wq, a_kv_norm, a_wkv, a_qk_qnorm, a_qk_knorm, a_wo, b_wg, b_bg, b_onorm, b_wo, c_conv_w, c_conv_b, c_dt_bias, c_A_log, c_D, c_norm, c_wo, w_out, router_w, router_bias, e_w1, e_w3, e_w2):
    d = D_MODEL
    x = jnp.concatenate([x_prompt.reshape(N_CTX, d), x_sample.reshape(N_DEC, d)], axis=0)

    cvecs = jnp.zeros((8, d), F32).at[0].set(c_ctx).at[1:1 + DEC_BATCH].set(c)
    mod3 = _mod_call(cvecs, w_ada, b_ada).reshape(DEPTH * 8, 1, 6 * d)

    rope_tabs = _rope_tables()
    gla_np = (_gla_consts(False), _gla_consts(True))
    gla_mall = jnp.asarray(np.stack([gla_np[0][0], gla_np[1][0]]), BF16)
    gla_masks = jnp.asarray(np.stack([gla_np[0][1], gla_np[1][1]]))
    ssd_consts = _ssd_consts()
    rw = jnp.zeros((d, LANE), F32).at[:, :N_EXPERTS].set(router_w)
    rwh, rwl = _split2(rw)
    rb = _lane_slot(router_bias, 0)
    ridx = np.arange(1024)
    route_tri = jnp.asarray(ridx[None, :] < ridx[:, None], BF16)

    ckvs, kropes, glas, ssds = [], [], [], []
    for l in range(DEPTH):
        w_main, w_small = _pack_w_in(w_in[l])
        u, s_all = _norm_proj_call(x, mod3, norm1[l].reshape(1, d), w_main, w_small, l)
        kropes.append(s_all[:N_CTX, S_KR:S_KR + A_ROPE].reshape(BATCH, SEQ, A_ROPE))

        wq = a_wq[l].reshape(A_QLORA, A_HEADS, A_QK)
        wq = _pad_heads(jnp.concatenate([wq[..., :A_NOPE], wq[..., A_NOPE:][..., _ROPE_PERM]], axis=-1), A_QK)
        wkv = a_wkv[l].reshape(A_KVLORA, A_HEADS, A_NOPE + A_V)
        aw = dict(qnw=a_q_norm[l].reshape(1, A_QLORA), wq=wq.astype(BF16),
                  gq=_qk_gain(a_qk_qnorm[l], A_QK ** -0.5),
                  kvnw=a_kv_norm[l].reshape(1, A_KVLORA),
                  wkn=_pad_heads(wkv[..., :A_NOPE], A_NOPE).astype(BF16),
                  wv=_pad_heads(wkv[..., A_NOPE:], A_V).astype(BF16),
                  gk=_qk_gain(a_qk_knorm[l], 1.0))
        q_c, k_c, v_c, ckvn = _attn_prep_call(N_CTX, 0, (u,), aw, has_q=True, norm_kv=True, rope=False, emit_ckv=True)
        o_c = _attn_call(q_c, k_c, v_c, None, n_seq=BATCH, seq_len=SEQ, hps=A_HEADS, tq=SEQ)
        ckvs.append(ckvn.reshape(BATCH, SEQ, A_KVLORA))
        q_d, k_d, v_d = _attn_prep_call(N_DEC, N_CTX, (u,), aw, has_q=True, norm_kv=True, rope=True, emit_ckv=False,
                                        tables=rope_tabs)
        kr_x = jnp.zeros((DEC_BATCH * PAST_LEN, LANE), F32).at[:, A_NOPE:A_QK].set(
            cache_krope[:, l].reshape(DEC_BATCH * PAST_LEN, A_ROPE)[:, _ROPE_PERM])
        k_x, v_x = _attn_prep_call(DEC_BATCH * PAST_LEN, 0, (cache_ckv[:, l].reshape(DEC_BATCH * PAST_LEN, A_KVLORA), kr_x),
                                   aw, has_q=False, norm_kv=False, rope=False, emit_ckv=False)
        o_d = _attn_call(q_d, k_d, v_d, (k_x, v_x), n_seq=DEC_BATCH, seq_len=DEC_SEQ, hps=2, tq=512)

        hk = B_HEADS * B_DK
        wg = (jnp.zeros((2, LANE, hk), F32).at[0, S_GF:S_GF + B_GATE_RANK].set(b_wg[l, 0])
              .at[1, S_GB:S_GB + B_GATE_RANK].set(b_wg[l, 1]))
        wgh, wgl = _split2(wg)
        bg = b_bg[l].reshape(2, 1, hk)
        g_fc, g_bc, st_gc = _gla_call(u, s_all, wgh, wgl, bg, gla_mall, gla_masks, None,
                                      n_seq=BATCH, seq_len=SEQ, row0=0)
        g_fd, g_bd, _ = _gla_call(u, s_all, wgh, wgl, bg, gla_mall, gla_masks, jnp.swapaxes(state_gla[:, l], -1, -2),
                                  n_seq=DEC_BATCH, seq_len=DEC_SEQ, row0=N_CTX)
        glas.append(jnp.swapaxes(st_gc, -1, -2))

        w8 = jnp.zeros((8, C_XBC), F32).at[:C_CONV].set(c_conv_w[l])
        cb = c_conv_b[l].reshape(1, C_XBC)
        xbc_c = _conv_call(u, w8, cb, n_seq=BATCH, seq_len=SEQ, row0=0)
        xbc_d = _conv_call(u, w8, cb, n_seq=DEC_BATCH, seq_len=DEC_SEQ, row0=N_CTX)
        dtb = jnp.stack([_lane_slot(c_dt_bias[l, 0], S_DTF), _lane_slot(c_dt_bias[l, 1], S_DTB)])
        a_neg = -jnp.exp(c_A_log[l].astype(F32))
        aneg = jnp.stack([_lane_slot(a_neg[0], S_DTF), _lane_slot(a_neg[1], S_DTB)])
        y_fc, y_bc, st_c = _ssd_call(xbc_c, s_all, dtb, aneg, ssd_consts, None, n_seq=BATCH, seq_len=SEQ, row0=0)
        st0 = state_ssd[:, l].reshape(DEC_BATCH, 2, C_HEADS // 2, LANE, C_STATE)
        y_fd, y_bd, _ = _ssd_call(xbc_d, s_all, dtb, aneg, ssd_consts, st0, n_seq=DEC_BATCH, seq_len=DEC_SEQ, row0=N_CTX)
        ssds.append(st_c.reshape(BATCH, 2, C_HEADS, C_HEADDIM, C_STATE))

        awo = a_wo[l].reshape(A_HEADS, A_V, d)
        awo = jnp.concatenate([awo, jnp.zeros((A_HEADS, HEAD_PAD - A_V, d), awo.dtype)], axis=1)
        mw = dict(awo=awo.reshape(A_HEADS * HEAD_PAD, d).astype(BF16), onw=b_onorm[l].reshape(1, B_DV),
                  bwo=b_wo[l].astype(BF16), dexp=jnp.repeat(c_D[l], C_HEADDIM).reshape(1, C_INNER),
                  cnw=c_norm[l].reshape(1, C_INNER), cwo=c_wo[l].astype(BF16), wout=w_out[l].astype(BF16))
        x = _merge_call(x, mod3, u, o_c, g_fc, g_bc, y_fc, y_bc, xbc_c, mw, n_rows=N_CTX, row0=0, layer=l)
        x = _merge_call(x, mod3, u, o_d, g_fd, g_bd, y_fd, y_bd, xbc_d, mw, n_rows=N_DEC, row0=N_CTX, layer=l)

        x = _moe_routed(x, mod3, norm2[l].reshape(1, d), rwh, rwl, rb, route_tri, e_w1, e_w3, e_w2, l)

    y_prompt = x[:N_CTX].reshape(BATCH, SEQ, d)
    y_sample = x[N_CTX:].reshape(DEC_BATCH, DEC_SEQ, d)
    new_ckv = jnp.stack(ckvs, axis=1)
    new_krope = jnp.stack(kropes, axis=1)
    new_state_gla = jnp.stack(glas, axis=1)
    new_state_ssd = jnp.stack(ssds, axis=1)
    return (y_prompt, y_sample, new_ckv, new_krope, new_state_gla, new_state_ssd)
```

```python
import functools
import math

import numpy as np
import jax
import jax.numpy as jnp
from jax import lax
from jax.experimental import pallas as pl
from jax.experimental.pallas import tpu as pltpu

F32 = jnp.float32
BF16 = jnp.bfloat16

D_MODEL = 1024
BATCH = 16
SEQ = 256
DEPTH = 2
DEC_BATCH = 2
DEC_SEQ = 2048
PAST_LEN = 512
GRID_W = 64
EPS = 1e-6
ROPE_BASE = 10000.0

A_HEADS = 16
A_NOPE = 64
A_ROPE = 32
A_QK = A_NOPE + A_ROPE
A_V = 64
A_QLORA = 512
A_KVLORA = 256

B_HEADS = 4
B_DK = 128
B_DV = 256
B_GATE_RANK = 16
B_GATE_NORM = 16.0

C_HEADS = 16
C_HEADDIM = 64
C_INNER = C_HEADS * C_HEADDIM
C_GROUPS = 2
C_STATE = 128
C_XBC = C_INNER + 2 * C_GROUPS * C_STATE
C_CONV = 5

N_EXPERTS = 16
N_EXPERT_GROUPS = 4
D_EXPERT = 512

N_CTX = BATCH * SEQ
N_DEC = DEC_BATCH * DEC_SEQ
N_TOK = N_CTX + N_DEC

LANE = 128
HEAD_PAD = 128
CHUNK = 128

U_CQ = 0
U_CKV = 512
U_KR = 768
U_QK = 1024
U_V = 2048
U_OG = 3072
U_Z = 4096
U_X = 5120
U_GATES = 6144
U_BC = 9216
U_WIDTH = 9728
S_GF, S_GB, S_DTF, S_DTB, S_KR = 0, 16, 32, 48, 64

_IN_SPLITS = (
    ('a_cq', A_QLORA), ('a_ckv', A_KVLORA), ('a_krope', A_ROPE),
    ('b_q', B_HEADS * B_DK), ('b_k', B_HEADS * B_DK), ('b_v', B_HEADS * B_DV),
    ('b_og', B_HEADS * B_DV), ('b_gf', B_GATE_RANK), ('b_gb', B_GATE_RANK),
    ('c_z', C_INNER), ('c_xbc', C_XBC), ('c_dtf', C_HEADS), ('c_dtb', C_HEADS),
    ('gates', 3 * D_MODEL),
)
_N_LOW = HEAD_PAD // 2 - A_ROPE // 2
_SLOT_SRC = np.concatenate([A_NOPE + np.arange(0, A_ROPE, 2), np.arange(0, _N_LOW),
                            A_NOPE + np.arange(1, A_ROPE, 2), np.arange(_N_LOW, A_NOPE)])


def _col_slices(w):
    parts, start = {}, 0
    for name, size in _IN_SPLITS:
        parts[name] = w[:, start:start + size]
        start += size
    return parts


def _dot(a, b):
    return jnp.dot(a, b, preferred_element_type=F32)


def _dot_nt(a, b):
    return lax.dot_general(a, b, (((1,), (1,)), ((), ())), preferred_element_type=F32)


def _dot_tn(a, b):
    return lax.dot_general(a, b, (((0,), (0,)), ((), ())), preferred_element_type=F32)


def _split2(x):
    hi = x.astype(BF16)
    lo = (x - hi.astype(F32)).astype(BF16)
    return hi, lo


def _silu(x):
    return x * jax.nn.sigmoid(x)


def _softplus(x):
    return jnp.maximum(x, 0.0) + jnp.log1p(jnp.exp(-jnp.abs(x)))


def _log_sigmoid(x):
    return jnp.minimum(x, 0.0) - jnp.log1p(jnp.exp(-jnp.abs(x)))


def _mod_kernel(c_ref, w_ref, b_ref, o_ref):
    s = _silu(c_ref[...]).astype(BF16)
    o_ref[...] = _dot(s, w_ref[...].astype(BF16)) + b_ref[...]


def _mod_call(cvecs, w_ada, b_ada):
    tn = 1536
    return pl.pallas_call(
        _mod_kernel,
        grid=(DEPTH, 6 * D_MODEL // tn),
        in_specs=[pl.BlockSpec((8, D_MODEL), lambda l, j: (0, 0)),
                  pl.BlockSpec((None, D_MODEL, tn), lambda l, j: (l, 0, j)),
                  pl.BlockSpec((None, 1, tn), lambda l, j: (l, 0, j))],
        out_specs=pl.BlockSpec((None, 8, tn), lambda l, j: (l, 0, j)),
        out_shape=jax.ShapeDtypeStruct((DEPTH, 8, 6 * D_MODEL), F32),
        name="adaln_mod",
    )(cvecs, w_ada, b_ada.reshape(DEPTH, 1, 6 * D_MODEL))


def _mod_row(i, tm, layer):
    n_ctx = N_CTX // tm
    per_b = DEC_SEQ // tm
    return layer * 8 + jnp.where(i < n_ctx, 0, 1 + (i - n_ctx) // per_b)


def _norm_proj_kernel(x_ref, mod_ref, nw_ref, w_ref, ws_ref, u_ref, s_ref, h_scr):
    @pl.when(pl.program_id(1) == 0)
    def _():
        x = x_ref[...]
        m = mod_ref[...]
        y = x * lax.rsqrt(jnp.mean(x * x, axis=-1, keepdims=True) + EPS) * nw_ref[...]
        h = (y * (1.0 + m[:, D_MODEL:2 * D_MODEL]) + m[:, 0:D_MODEL]).astype(BF16)
        h_scr[...] = h
        s_ref[...] = _dot(h, ws_ref[...])

    u_ref[...] = _dot(h_scr[...], w_ref[...]).astype(BF16)


def _norm_proj_call(x, mod3, nw, w_main, w_small, layer):
    tm, tn = 2048, 512
    return pl.pallas_call(
        _norm_proj_kernel,
        grid=(N_TOK // tm, U_WIDTH // tn),
        in_specs=[pl.BlockSpec((tm, D_MODEL), lambda i, j: (i, 0)),
                  pl.BlockSpec((None, 1, 6 * D_MODEL), lambda i, j: (_mod_row(i, tm, layer), 0, 0)),
                  pl.BlockSpec((1, D_MODEL), lambda i, j: (0, 0)),
                  pl.BlockSpec((D_MODEL, tn), lambda i, j: (0, j)),
                  pl.BlockSpec((D_MODEL, LANE), lambda i, j: (0, 0))],
        out_specs=[pl.BlockSpec((tm, tn), lambda i, j: (i, j)),
                   pl.BlockSpec((tm, LANE), lambda i, j: (i, 0))],
        out_shape=[jax.ShapeDtypeStruct((N_TOK, U_WIDTH), BF16),
                   jax.ShapeDtypeStruct((N_TOK, LANE), F32)],
        scratch_shapes=[pltpu.VMEM((tm, D_MODEL), BF16)],
        compiler_params=pltpu.CompilerParams(dimension_semantics=("parallel", "arbitrary")),
        name="norm_in_proj",
    )(x, mod3, nw, w_main, w_small)


def _rope(t, cos, sin):
    return t * cos + pltpu.roll(t, HEAD_PAD // 2, 1) * sin


def _head_scale(ss):
    return lax.rsqrt(ss * (1.0 / A_QK) + EPS)


def _attn_prep_kernel(*refs, has_q, norm_kv, rope, emit_ckv):
    refs = list(refs)
    cq_ref = refs.pop(0) if has_q else None
    ckv_ref, kr_ref = refs.pop(0), refs.pop(0)
    if has_q:
        qnw_ref, wq_ref, gq_ref = refs.pop(0), refs.pop(0), refs.pop(0)
    kvnw_ref, wkn_ref, wv_ref, gk_ref = refs.pop(0), refs.pop(0), refs.pop(0), refs.pop(0)
    cos = sin = None
    if rope:
        cos, sin = refs.pop(0)[...], refs.pop(0)[...]
    q_ref = refs.pop(0) if has_q else None
    k_ref, v_ref = refs.pop(0), refs.pop(0)
    ckvn_ref = refs.pop(0) if emit_ckv else None

    ckv = ckv_ref[...].astype(F32)
    if norm_kv:
        ckv = ckv * lax.rsqrt(jnp.mean(ckv * ckv, axis=-1, keepdims=True) + EPS) * kvnw_ref[...]
    if emit_ckv:
        ckvn_ref[...] = ckv
    ckv_b = ckv.astype(BF16)
    v_ref[...] = _dot(ckv_b, wv_ref[...]).astype(BF16)
    kf = _dot(ckv_b, wkn_ref[...])
    kr = kr_ref[...].astype(F32)
    gk = gk_ref[...]
    ss_kr = jnp.sum(kr * kr, axis=-1, keepdims=True)
    krg = kr * gk
    if rope:
        krg = _rope(krg, cos, sin)
    for h in range(A_HEADS):
        sl = slice(h * HEAD_PAD, (h + 1) * HEAD_PAD)
        kn = kf[:, sl]
        scale = _head_scale(jnp.sum(kn * kn, axis=-1, keepdims=True) + ss_kr)
        k_ref[:, sl] = ((kn * gk + krg) * scale).astype(BF16)

    if has_q:
        cq = cq_ref[...].astype(F32)
        cq = cq * lax.rsqrt(jnp.mean(cq * cq, axis=-1, keepdims=True) + EPS) * qnw_ref[...]
        qf = _dot(cq.astype(BF16), wq_ref[...])
        gq = gq_ref[...]
        for h in range(A_HEADS):
            sl = slice(h * HEAD_PAD, (h + 1) * HEAD_PAD)
            t = qf[:, sl]
            t = t * _head_scale(jnp.sum(t * t, axis=-1, keepdims=True)) * gq
            q_ref[:, sl] = (_rope(t, cos, sin) if rope else t).astype(BF16)


def _attn_prep_call(n_rows, row0, srcs, wts, *, has_q, norm_kv, rope, emit_ckv, tables=None):
    tm = 256
    r0 = row0 // tm
    pw = A_HEADS * HEAD_PAD
    full = lambda shape: pl.BlockSpec(shape, lambda i: (0,) * len(shape))
    args, in_specs = [], []
    if len(srcs) == 1:
        u = srcs[0]
        if has_q:
            args.append(u)
            in_specs.append(pl.BlockSpec((tm, A_QLORA), lambda i: (i + r0, U_CQ // A_QLORA)))
        args += [u, u]
        in_specs += [pl.BlockSpec((tm, A_KVLORA), lambda i: (i + r0, U_CKV // A_KVLORA)),
                     pl.BlockSpec((tm, LANE), lambda i: (i + r0, U_KR // LANE))]
    else:
        args += list(srcs)
        in_specs += [pl.BlockSpec((tm, A_KVLORA), lambda i: (i, 0)),
                     pl.BlockSpec((tm, LANE), lambda i: (i, 0))]
    if has_q:
        args += [wts['qnw'], wts['wq'], wts['gq']]
        in_specs += [full((1, A_QLORA)), full((A_QLORA, pw)), full((1, HEAD_PAD))]
    args += [wts['kvnw'], wts['wkn'], wts['wv'], wts['gk']]
    in_specs += [full((1, A_KVLORA)), full((A_KVLORA, pw)), full((A_KVLORA, pw)), full((1, HEAD_PAD))]
    if rope:
        per_seq = DEC_SEQ // tm
        args += list(tables)
        in_specs += [pl.BlockSpec((tm, HEAD_PAD), lambda i: (i % per_seq, 0))] * 2
    out_specs, out_shape = [], []
    for _ in range((1 if has_q else 0) + 2):
        out_specs.append(pl.BlockSpec((tm, pw), lambda i: (i, 0)))
        out_shape.append(jax.ShapeDtypeStruct((n_rows, pw), BF16))
    if emit_ckv:
        out_specs.append(pl.BlockSpec((tm, A_KVLORA), lambda i: (i, 0)))
        out_shape.append(jax.ShapeDtypeStruct((n_rows, A_KVLORA), F32))
    return pl.pallas_call(
        functools.partial(_attn_prep_kernel, has_q=has_q, norm_kv=norm_kv, rope=rope, emit_ckv=emit_ckv),
        grid=(n_rows // tm,),
        in_specs=in_specs, out_specs=out_specs, out_shape=out_shape,
        compiler_params=pltpu.CompilerParams(dimension_semantics=("parallel",)),
        name="mla_prep",
    )(*args)


def _attn_kernel(*refs, hps, has_ctx):
    if has_ctx:
        q_ref, k_ref, v_ref, kc_ref, vc_ref, o_ref = refs
    else:
        q_ref, k_ref, v_ref, o_ref = refs

    def scores(hh):
        sl = slice(hh * HEAD_PAD, (hh + 1) * HEAD_PAD)
        q = q_ref[:, sl]
        return _dot_nt(q, k_ref[:, sl]), (_dot_nt(q, kc_ref[:, sl]) if has_ctx else None)

    nxt = scores(0)
    for hh in range(hps):
        sl = slice(hh * HEAD_PAD, (hh + 1) * HEAD_PAD)
        s, s2 = nxt
        if hh + 1 < hps:
            nxt = scores(hh + 1)
        m = jnp.max(s, axis=-1, keepdims=True)
        if has_ctx:
            m = jnp.maximum(m, jnp.max(s2, axis=-1, keepdims=True))
        p = jnp.exp(s - m)
        den = jnp.sum(p, axis=-1, keepdims=True)
        o = _dot(p.astype(BF16), v_ref[:, sl])
        if has_ctx:
            p2 = jnp.exp(s2 - m)
            den = den + jnp.sum(p2, axis=-1, keepdims=True)
            o = o + _dot(p2.astype(BF16), vc_ref[:, sl])
        o_ref[:, sl] = (o / den).astype(BF16)


def _attn_call(q, k, v, ctx_kv, *, n_seq, seq_len, hps, tq):
    pw = A_HEADS * HEAD_PAD
    bw = hps * HEAD_PAD
    nq = seq_len // tq
    in_specs = [pl.BlockSpec((tq, bw), lambda b, h, i: (b * nq + i, h)),
                pl.BlockSpec((seq_len, bw), lambda b, h, i: (b, h)),
                pl.BlockSpec((seq_len, bw), lambda b, h, i: (b, h))]
    args = [q, k, v]
    if ctx_kv is not None:
        in_specs += [pl.BlockSpec((PAST_LEN, bw), lambda b, h, i: (b, h))] * 2
        args += list(ctx_kv)
    return pl.pallas_call(
        functools.partial(_attn_kernel, hps=hps, has_ctx=ctx_kv is not None),
        grid=(n_seq, A_HEADS // hps, nq),
        in_specs=in_specs,
        out_specs=pl.BlockSpec((tq, bw), lambda b, h, i: (b * nq + i, h)),
        out_shape=jax.ShapeDtypeStruct((n_seq * seq_len, pw), BF16),
        compiler_params=pltpu.CompilerParams(dimension_semantics=("parallel", "parallel", "arbitrary")),
        name="mla_attention",
    )(*args)


def _conv_kernel(x_ref, w_ref, b_ref, o_ref):
    x = x_ref[...].astype(F32)
    n = x.shape[0]
    row = lax.broadcasted_iota(jnp.int32, x.shape, 0)
    half = (C_CONV - 1) // 2
    acc = x * w_ref[half:half + 1, :] + b_ref[...]
    for d in range(-half, half + 1):
        if d == 0:
            continue
        shifted = pltpu.roll(x, (-d) % n, 0)
        valid = jnp.logical_and(row + d >= 0, row + d < n)
        acc = acc + jnp.where(valid, shifted, 0.0) * w_ref[half + d:half + d + 1, :]
    o_ref[...] = _silu(acc).astype(BF16)


def _conv_call(u, w8, b, *, n_seq, seq_len, row0):
    tc = 512
    b0 = row0 // seq_len
    nx = C_INNER // tc
    ucol = lambda j: jnp.where(j < nx, U_X // tc + j, U_BC // tc + j - nx)
    return pl.pallas_call(
        _conv_kernel,
        grid=(n_seq, C_XBC // tc),
        in_specs=[pl.BlockSpec((seq_len, tc), lambda s, j: (s + b0, ucol(j))),
                  pl.BlockSpec((8, tc), lambda s, j: (0, j)),
                  pl.BlockSpec((1, tc), lambda s, j: (0, j))],
        out_specs=pl.BlockSpec((seq_len, tc), lambda s, j: (s, j)),
        out_shape=jax.ShapeDtypeStruct((n_seq * seq_len, C_XBC), BF16),
        compiler_params=pltpu.CompilerParams(dimension_semantics=("parallel", "parallel")),
        name="ssd_conv",
    )(u, w8, b)


def _ssd_consts():
    idx = np.arange(CHUNK)
    tri = np.stack([idx[None, :] <= idx[:, None], idx[None, :] >= idx[:, None]]).astype(np.float32)
    expand = np.zeros((2, LANE, C_INNER), np.float32)
    tile = np.zeros((2, LANE, C_HEADS * LANE), np.float32)
    for di, off in enumerate((S_DTF, S_DTB)):
        for h in range(C_HEADS):
            expand[di, off + h, h * C_HEADDIM:(h + 1) * C_HEADDIM] = 1.0
            tile[di, off + h, h * LANE:(h + 1) * LANE] = 1.0
    return jnp.asarray(tri, BF16), jnp.asarray(expand, BF16), jnp.asarray(tile, BF16)


def _ssd_kernel(*refs, has_init):
    refs = list(refs)
    xs = [(refs.pop(0), refs.pop(0), refs.pop(0)) for _ in range(2)]
    dtb_ref, an_ref, tri_ref, exp_ref, tile_ref = (refs.pop(0) for _ in range(5))
    st0_ref = refs.pop(0) if has_init else None
    y_refs = (refs.pop(0), refs.pop(0))
    stf_ref, st_scr = refs

    @pl.when(pl.program_id(1) == 0)
    def _():
        st_scr[...] = st0_ref[...] if has_init else jnp.zeros(st_scr.shape, F32)

    t = CHUNK
    ri = lax.broadcasted_iota(jnp.int32, (t, t), 0)
    ci = lax.broadcasted_iota(jnp.int32, (t, t), 1)
    prow = lax.broadcasted_iota(jnp.int32, (LANE, C_STATE), 0)
    gs = C_GROUPS * C_STATE
    pairs_per_group = C_HEADS // C_GROUPS // 2
    for di in range(2):
        x_ref, bc_ref, s_ref = xs[di]
        y_ref = y_refs[di]
        off = (S_DTF, S_DTB)[di]
        dt = _softplus(s_ref[...] + dtb_ref[di])
        a = dt * an_ref[di]
        a_hi, a_lo = _split2(a)
        tri = tri_ref[di]
        acum = _dot(tri, a_hi) + _dot(tri, a_lo)
        acum_t = acum.T
        dt_t = dt.T
        edge = 0 if di else t - 1
        atot = acum[edge:edge + 1, :]
        causal = (ci >= ri) if di else (ci <= ri)
        dec_all = jnp.exp(atot)
        e_exp = _dot(jnp.exp(acum).astype(BF16), exp_ref[di])
        wj_exp = _dot((jnp.exp(atot - acum) * dt).astype(BF16), exp_ref[di])
        c_hi, c_lo = _split2(acum)
        acol = _dot(c_hi, tile_ref[di]) + _dot(c_lo, tile_ref[di])
        bc = bc_ref[...]
        cb, bmat, cmat = [], [], []
        for g in range(C_GROUPS):
            bmat.append(bc[:, g * C_STATE:(g + 1) * C_STATE])
            cmat.append(bc[:, gs + g * C_STATE:gs + (g + 1) * C_STATE])
            cb.append(_dot_nt(cmat[g], bmat[g]))
        lane = lax.broadcasted_iota(jnp.int32, (t, LANE), 1)
        low = lane < C_HEADDIM
        for m in range(C_HEADS // 2):
            g = m // pairs_per_group
            ps = slice(m * LANE, (m + 1) * LANE)
            xp = x_ref[:, ps]
            y_diag = []
            for hh in range(2):
                h = 2 * m + hh
                la = off + h
                lm = jnp.exp(jnp.where(causal, acol[:, h * LANE:(h + 1) * LANE] - acum_t[la:la + 1, :], -jnp.inf))
                w = (cb[g] * lm * dt_t[la:la + 1, :]).astype(BF16)
                y_diag.append(_dot(w, xp))
            st = st_scr[di, m]
            y_off = _dot_nt(cmat[g], st.astype(BF16)) * e_exp[:, ps]
            y_ref[:, ps] = (jnp.where(low, y_diag[0], y_diag[1]) + y_off).astype(BF16)
            xw = (xp.astype(F32) * wj_exp[:, ps]).astype(BF16)
            la = off + 2 * m
            dec = jnp.where(prow < C_HEADDIM, dec_all[:, la:la + 1], dec_all[:, la + 1:la + 2])
            st_scr[di, m] = dec * st + _dot_tn(xw, bmat[g])

    @pl.when(pl.program_id(1) == pl.num_programs(1) - 1)
    def _():
        stf_ref[...] = st_scr[...]


def _ssd_call(xbc, s_all, dtb, aneg, consts, st0, *, n_seq, seq_len, row0):
    t = CHUNK
    nch = seq_len // t
    r0 = row0 // t
    npair = C_HEADS // 2
    bcw = 2 * C_GROUPS * C_STATE
    tri, expand, tile = consts
    full = lambda a: pl.BlockSpec(a.shape, lambda b, c: (0,) * a.ndim)
    in_specs, args = [], []
    for cidx in (lambda c: c, lambda c: nch - 1 - c):
        in_specs += [pl.BlockSpec((t, C_INNER), lambda b, c, cidx=cidx: (b * nch + cidx(c), 0)),
                     pl.BlockSpec((t, bcw), lambda b, c, cidx=cidx: (b * nch + cidx(c), C_INNER // bcw)),
                     pl.BlockSpec((t, LANE), lambda b, c, cidx=cidx: (r0 + b * nch + cidx(c), 0))]
        args += [xbc, xbc, s_all]
    in_specs += [full(dtb), full(aneg), full(tri), full(expand), full(tile)]
    args += [dtb, aneg, tri, expand, tile]
    st_spec = pl.BlockSpec((None, 2, npair, LANE, C_STATE), lambda b, c: (b, 0, 0, 0, 0))
    if st0 is not None:
        in_specs.append(st_spec)
        args.append(st0)
    return pl.pallas_call(
        functools.partial(_ssd_kernel, has_init=st0 is not None),
        grid=(n_seq, nch),
        in_specs=in_specs,
        out_specs=[pl.BlockSpec((t, C_INNER), lambda b, c: (b * nch + c, 0)),
                   pl.BlockSpec((t, C_INNER), lambda b, c: (b * nch + nch - 1 - c, 0)),
                   st_spec],
        out_shape=[jax.ShapeDtypeStruct((n_seq * seq_len, C_INNER), BF16),
                   jax.ShapeDtypeStruct((n_seq * seq_len, C_INNER), BF16),
                   jax.ShapeDtypeStruct((n_seq, 2, npair, LANE, C_STATE), F32)],
        scratch_shapes=[pltpu.VMEM((2, npair, LANE, C_STATE), F32)],
        compiler_params=pltpu.CompilerParams(dimension_semantics=("parallel", "arbitrary")),
        name="ssd_scan",
    )(*args)


def _gla_consts(bwd):
    t = CHUNK
    nlev = int(math.log2(t))
    idx = np.arange(t)
    mats = np.zeros((nlev + 2, t, t), np.float32)
    masks = np.zeros((nlev + 1, t, t), np.float32)
    masks[0] = np.eye(t)
    for lvl in range(nlev):
        s = 1 << lvl
        blk = idx // (2 * s)
        upper = (idx % (2 * s)) >= s
        last_low = blk * 2 * s + s - 1
        rowtok = ~upper if bwd else upper
        for i in range(t):
            r = last_low[i]
            if not bwd:
                if upper[i]:
                    mats[lvl, i, r + 1:i + 1] = 1.0
                else:
                    mats[lvl, i, i + 1:r + 1] = 1.0
            else:
                if upper[i]:
                    mats[lvl, i, r + 1:i] = 1.0
                else:
                    mats[lvl, i, i:r + 1] = 1.0
        masks[lvl + 1] = ((blk[:, None] == blk[None, :]) & rowtok[:, None] & (~rowtok)[None, :])
    incl = (idx[None, :] >= idx[:, None]) if bwd else (idx[None, :] <= idx[:, None])
    mats[nlev] = incl
    mats[nlev + 1] = 1.0 - incl
    return mats.reshape((nlev + 2) * t, t), masks


def _gla_kernel(*refs, has_init):
    refs = list(refs)
    xs = [(refs.pop(0), refs.pop(0), refs.pop(0)) for _ in range(2)]
    wgh_ref, wgl_ref, bg_ref, mall_ref, mask_ref = (refs.pop(0) for _ in range(5))
    st0_ref = refs.pop(0) if has_init else None
    o_refs = (refs.pop(0), refs.pop(0))
    stf_ref, st_scr = refs

    @pl.when(pl.program_id(1) == 0)
    def _():
        st_scr[...] = st0_ref[...] if has_init else jnp.zeros(st_scr.shape, F32)

    t = CHUNK
    nlev = mask_ref.shape[1] - 1
    hk = B_HEADS * B_DK
    e_alls = []
    for di in range(2):
        s_hi, s_lo = _split2(xs[di][2][...])
        wgh = wgh_ref[di]
        logit = _dot(s_hi, wgh) + _dot(s_lo, wgh) + _dot(s_hi, wgl_ref[di]) + bg_ref[di]
        g_all = _log_sigmoid(logit) * (1.0 / B_GATE_NORM)
        e_alls.append(jnp.exp(_dot(mall_ref[di], g_all.astype(BF16))))
    for h in range(B_HEADS):
        for di in range(2):
            qk_ref, v_ref, _ = xs[di]
            o_ref = o_refs[di]
            e_all = e_alls[di]
            edge = 0 if di else t - 1
            ks = slice(h * B_DK, (h + 1) * B_DK)
            vs = slice(h * B_DV, (h + 1) * B_DV)
            q = qk_ref[:, ks].astype(F32) * (B_DK ** -0.5)
            k = qk_ref[:, hk + h * B_DK:hk + (h + 1) * B_DK]
            kf = k.astype(F32)
            v = v_ref[:, vs]
            amat = _dot_nt(q.astype(BF16), k) * mask_ref[di, 0]
            for lvl in range(nlev):
                e = e_all[lvl * t:(lvl + 1) * t, ks]
                amat = amat + _dot_nt((q * e).astype(BF16), (kf * e).astype(BF16)) * mask_ref[di, lvl + 1]
            e_in = e_all[nlev * t:(nlev + 1) * t, ks]
            e_out = e_all[(nlev + 1) * t:(nlev + 2) * t, ks]
            st = st_scr[di, h]
            o = _dot(amat.astype(BF16), v) + _dot_nt((q * e_in).astype(BF16), st.astype(BF16))
            o_ref[:, vs] = o.astype(BF16)
            st_scr[di, h] = st * e_in[edge:edge + 1, :] + _dot_tn(v, (kf * e_out).astype(BF16))

    @pl.when(pl.program_id(1) == pl.num_programs(1) - 1)
    def _():
        stf_ref[...] = st_scr[...]


def _gla_call(u, s_all, wgh, wgl, bg, mall, masks, st0, *, n_seq, seq_len, row0):
    t = CHUNK
    nch = seq_len // t
    r0 = row0 // t
    hk, hv = B_HEADS * B_DK, B_HEADS * B_DV
    full = lambda a: pl.BlockSpec(a.shape, lambda b, c: (0,) * a.ndim)
    in_specs, args = [], []
    for cidx in (lambda c: c, lambda c: nch - 1 - c):
        in_specs += [pl.BlockSpec((t, 2 * hk), lambda b, c, cidx=cidx: (r0 + b * nch + cidx(c), U_QK // (2 * hk))),
                     pl.BlockSpec((t, hv), lambda b, c, cidx=cidx: (r0 + b * nch + cidx(c), U_V // hv)),
                     pl.BlockSpec((t, LANE), lambda b, c, cidx=cidx: (r0 + b * nch + cidx(c), 0))]
        args += [u, u, s_all]
    in_specs += [full(wgh), full(wgl), full(bg), full(mall), full(masks)]
    args += [wgh, wgl, bg, mall, masks]
    st_spec = pl.BlockSpec((None, 2, B_HEADS, B_DV, B_DK), lambda b, c: (b, 0, 0, 0, 0))
    if st0 is not None:
        in_specs.append(st_spec)
        args.append(st0)
    return pl.pallas_call(
        functools.partial(_gla_kernel, has_init=st0 is not None),
        grid=(n_seq, nch),
        in_specs=in_specs,
        out_specs=[pl.BlockSpec((t, hv), lambda b, c: (b * nch + c, 0)),
                   pl.BlockSpec((t, hv), lambda b, c: (b * nch + nch - 1 - c, 0)),
                   st_spec],
        out_shape=[jax.ShapeDtypeStruct((n_seq * seq_len, hv), BF16),
                   jax.ShapeDtypeStruct((n_seq * seq_len, hv), BF16),
                   jax.ShapeDtypeStruct((n_seq, 2, B_HEADS, B_DV, B_DK), F32)],
        scratch_shapes=[pltpu.VMEM((2, B_HEADS, B_DV, B_DK), F32)],
        compiler_params=pltpu.CompilerParams(dimension_semantics=("parallel", "arbitrary")),
        name="gla_scan",
    )(*args)


def _merge_kernel(x_ref, mod_ref, oa_ref, gf_ref, gb_ref, og_ref, yf_ref, yb_ref, xc_ref, z_ref, gt_ref,
                  awo_ref, onw_ref, bwo_ref, dexp_ref, cnw_ref, cwo_ref, wout_ref, o_ref):
    o_a = _dot(oa_ref[...], awo_ref[...])

    og = gf_ref[...].astype(F32) + gb_ref[...].astype(F32)
    onw = onw_ref[...]
    parts = []
    for h in range(B_HEADS):
        th = og[:, h * B_DV:(h + 1) * B_DV]
        parts.append(th * lax.rsqrt(jnp.mean(th * th, axis=-1, keepdims=True) + EPS) * onw)
    ob_in = jnp.concatenate(parts, axis=-1) * _silu(og_ref[...].astype(F32))
    o_b = _dot(ob_in.astype(BF16), bwo_ref[...])

    y = yf_ref[...].astype(F32) + yb_ref[...].astype(F32) + dexp_ref[...] * xc_ref[...].astype(F32)
    y = y * _silu(z_ref[...].astype(F32))
    y = y * lax.rsqrt(jnp.mean(y * y, axis=-1, keepdims=True) + EPS) * cnw_ref[...]
    o_c = _dot(y.astype(BF16), cwo_ref[...])

    d = D_MODEL
    merged = (jax.nn.sigmoid(gt_ref[:, 0:d].astype(F32)) * o_a
              + jax.nn.sigmoid(gt_ref[:, d:2 * d].astype(F32)) * o_b
              + jax.nn.sigmoid(gt_ref[:, 2 * d:3 * d].astype(F32)) * o_c)
    out = _dot(merged.astype(BF16), wout_ref[...])
    o_ref[...] = x_ref[...] + mod_ref[:, 2 * d:3 * d] * out


def _merge_call(x, mod3, u, o_attn, g_f, g_b, y_f, y_b, xbc, wts, *, n_rows, row0, layer):
    tm = 512
    r0 = row0 // tm
    d = D_MODEL
    loc = lambda w: pl.BlockSpec((tm, w), lambda i: (i, 0))
    full = lambda a: pl.BlockSpec(a.shape, lambda i: (0,) * a.ndim)
    in_specs = [pl.BlockSpec((tm, d), lambda i: (i + r0, 0)),
                pl.BlockSpec((None, 1, 6 * d), lambda i: (_mod_row(i + r0, tm, layer), 0, 0)),
                loc(A_HEADS * HEAD_PAD), loc(d), loc(d),
                pl.BlockSpec((tm, d), lambda i: (i + r0, U_OG // d)),
                loc(d), loc(d),
                pl.BlockSpec((tm, d), lambda i: (i, 0)),
                pl.BlockSpec((tm, d), lambda i: (i + r0, U_Z // d)),
                pl.BlockSpec((tm, 3 * d), lambda i: (i + r0, U_GATES // (3 * d)))]
    w_args = [wts[n] for n in ('awo', 'onw', 'bwo', 'dexp', 'cnw', 'cwo', 'wout')]
    in_specs += [full(a) for a in w_args]
    return pl.pallas_call(
        _merge_kernel,
        grid=(n_rows // tm,),
        in_specs=in_specs,
        out_specs=pl.BlockSpec((tm, d), lambda i: (i + r0, 0)),
        out_shape=jax.ShapeDtypeStruct((N_TOK, d), F32),
        input_output_aliases={0: 0},
        compiler_params=pltpu.CompilerParams(dimension_semantics=("parallel",)),
        name="mixer_merge",
    )(x, mod3, o_attn, g_f, g_b, u, y_f, y_b, xbc, u, u, *w_args)


MOE_TILE = 256
MOE_TOKEN_TILE = 512
MOE_MAX_TILES = 2 * N_TOK // MOE_TILE + N_EXPERTS
MOE_ROWS = MOE_MAX_TILES * MOE_TILE
R_E1, R_E2, R_W1, R_W2, R_P1, R_P2 = 0, 1, 2, 3, 4, 5


def _first_index(vals, target):
    idx = jnp.full(target.shape, len(vals) - 1, jnp.int32)
    for k in reversed(range(len(vals) - 1)):
        idx = jnp.where(vals[k] == target, k, idx)
    return idx


def _top2_of(vals):
    m1 = functools.reduce(jnp.maximum, vals)
    i1 = _first_index(vals, m1)
    rest = [jnp.where(i1 == k, -jnp.inf, v) for k, v in enumerate(vals)]
    m2 = functools.reduce(jnp.maximum, rest)
    return m1, i1, m2, _first_index(rest, m2)


def _pick_by(vals, idx):
    out = vals[-1]
    for k in reversed(range(len(vals) - 1)):
        out = jnp.where(idx == k, vals[k], out)
    return out


def _route_kernel(x_ref, mod_ref, nw_ref, rwh_ref, rwl_ref, rb_ref, tri_ref, h_ref, route_ref, wrec_ref, cnt_ref,
                  base_scr):
    d = D_MODEL
    ng = N_EXPERT_GROUPS
    per_group = N_EXPERTS // ng

    @pl.when(pl.program_id(0) == 0)
    def _():
        base_scr[...] = jnp.zeros(base_scr.shape, F32)

    x = x_ref[...]
    tm = x.shape[0]
    y = x * lax.rsqrt(jnp.mean(x * x, axis=-1, keepdims=True) + EPS) * nw_ref[...]
    h = y * (1.0 + mod_ref[:, 4 * d:5 * d]) + mod_ref[:, 3 * d:4 * d]
    h_ref[...] = h
    h_hi, h_lo = _split2(h)
    rwh = rwh_ref[...]
    logits = _dot_nt(rwh, h_hi) + _dot_nt(rwh, h_lo) + _dot_nt(rwl_ref[...], h_hi)
    scores = jax.nn.sigmoid(logits)
    sel = scores + rb_ref[...]
    sel_k = [sel[k * ng:(k + 1) * ng] for k in range(per_group)]
    sc_k = [scores[k * ng:(k + 1) * ng] for k in range(per_group)]
    m1, i1, m2, i2 = _top2_of(sel_k)
    gsum = m1 + m2
    rows = lambda a: [a[g:g + 1] for g in range(ng)]
    gs = rows(gsum)
    best_v, best_g = gs[0], jnp.zeros((1, tm), jnp.int32)
    for g in range(1, ng):
        upd = gs[g] > best_v
        best_g = jnp.where(upd, g, best_g)
        best_v = jnp.where(upd, gs[g], best_v)
    k1 = _pick_by(rows(i1), best_g)
    k2 = _pick_by(rows(i2), best_g)
    s1 = _pick_by(rows(_pick_by(sc_k, i1)), best_g)
    s2 = _pick_by(rows(_pick_by(sc_k, i2)), best_g)
    tot = s1 + s2
    in_best = lax.broadcasted_iota(jnp.int32, (ng, tm), 0) == best_g
    hit = jnp.concatenate(
        [jnp.where(jnp.logical_and(in_best, jnp.logical_or(k1 == k, k2 == k)), 1.0, 0.0) for k in range(per_group)],
        axis=0)
    rank = _dot(hit.astype(BF16), tri_ref[...]) + base_scr[...]
    rank_k = [rank[k * ng:(k + 1) * ng] for k in range(per_group)]
    p1 = _pick_by(rows(_pick_by(rank_k, k1)), best_g)
    p2 = _pick_by(rows(_pick_by(rank_k, k2)), best_g)
    base_scr[...] += jnp.sum(hit, axis=1, keepdims=True)
    e1 = (best_g * per_group + k1).astype(F32)
    e2 = (best_g * per_group + k2).astype(F32)
    rec = jnp.concatenate([e1, e2, s1 / tot, s2 / tot, p1, p2, jnp.zeros((LANE - 6, tm), F32)], axis=0)
    route_ref[...] = rec[0:8]
    wrec_ref[...] = rec.T
    cnt_ref[...] = jnp.broadcast_to(base_scr[...], cnt_ref.shape)


def _route_call(x, mod3, nw, rwh, rwl, rb, tri, layer):
    tm = tri.shape[0]
    d = D_MODEL
    full = lambda a: pl.BlockSpec(a.shape, lambda i: (0,) * a.ndim)
    return pl.pallas_call(
        _route_kernel,
        grid=(N_TOK // tm,),
        in_specs=[pl.BlockSpec((tm, d), lambda i: (i, 0)),
                  pl.BlockSpec((None, 1, 6 * d), lambda i: (_mod_row(i, tm, layer), 0, 0)),
                  full(nw), full(rwh), full(rwl), full(rb), full(tri)],
        out_specs=[pl.BlockSpec((tm, d), lambda i: (i, 0)),
                   pl.BlockSpec((8, tm), lambda i: (0, i)),
                   pl.BlockSpec((tm, LANE), lambda i: (i, 0)),
                   pl.BlockSpec((N_EXPERTS, LANE), lambda i: (0, 0))],
        out_shape=[jax.ShapeDtypeStruct((N_TOK, d), F32),
                   jax.ShapeDtypeStruct((8, N_TOK), F32),
                   jax.ShapeDtypeStruct((N_TOK, LANE), F32),
                   jax.ShapeDtypeStruct((N_EXPERTS, LANE), F32)],
        scratch_shapes=[pltpu.VMEM((N_EXPERTS, 1), F32)],
        compiler_params=pltpu.CompilerParams(dimension_semantics=("arbitrary",)),
        name="moe_route",
    )(x, mod3, nw, rwh, rwl, rb, tri)


def _row_copy(src, src_row, dst, dst_row, sem):
    return pltpu.make_async_copy(src.at[pl.ds(src_row, 1), :], dst.at[pl.ds(dst_row, 1), :], sem)


def _dispatch_kernel(dst_ref, ztile_ref, h_ref, xs_ref, zbuf, sem):
    tm = h_ref.shape[0]

    @pl.when(pl.program_id(0) == 0)
    def _():
        zbuf[...] = jnp.zeros(zbuf.shape, F32)
        fills = []
        for e in range(N_EXPERTS):
            row = pl.multiple_of(jnp.maximum(ztile_ref[0, e], 0), MOE_TILE)
            fills.append((ztile_ref[0, e] >= 0, pltpu.make_async_copy(zbuf, xs_ref.at[pl.ds(row, MOE_TILE), :], sem)))
        for has_tile, cp in fills:
            pl.when(has_tile)(cp.start)
        for has_tile, cp in fills:
            pl.when(has_tile)(cp.wait)

        def fill_unused(t, carry):
            cp = pltpu.make_async_copy(zbuf, xs_ref.at[pl.ds(pl.multiple_of(t * MOE_TILE, MOE_TILE), MOE_TILE), :], sem)
            cp.start()
            cp.wait()
            return carry

        lax.fori_loop(ztile_ref[0, N_EXPERTS], MOE_MAX_TILES, fill_unused, 0)

    def issue(j, carry):
        for i in range(2):
            r = 2 * j + i
            _row_copy(h_ref, r, xs_ref, dst_ref[0, r], sem).start(priority=i)
            _row_copy(h_ref, r, xs_ref, dst_ref[0, tm + r], sem).start(priority=1 - i)
        return carry

    def drain(r, carry):
        _row_copy(h_ref, 0, xs_ref, 0, sem).wait()
        _row_copy(h_ref, 0, xs_ref, 0, sem).wait()
        return carry

    lax.fori_loop(0, tm // 2, issue, 0, unroll=4)
    lax.fori_loop(0, tm, drain, 0, unroll=8)


def _dispatch_call(dst, ztile, h):
    tm = dst.shape[-1] // 2
    d = D_MODEL
    return pl.pallas_call(
        _dispatch_kernel,
        grid=(N_TOK // tm,),
        in_specs=[pl.BlockSpec((None, 1, 2 * tm), lambda i: (i, 0, 0), memory_space=pltpu.SMEM),
                  pl.BlockSpec((1, N_EXPERTS + 1), lambda i: (0, 0), memory_space=pltpu.SMEM),
                  pl.BlockSpec((tm, d), lambda i: (i, 0))],
        out_specs=pl.BlockSpec(memory_space=pl.ANY),
        out_shape=jax.ShapeDtypeStruct((MOE_ROWS, d), F32),
        scratch_shapes=[pltpu.VMEM((MOE_TILE, d), F32), pltpu.SemaphoreType.DMA(())],
        compiler_params=pltpu.CompilerParams(dimension_semantics=("arbitrary",)),
        name="moe_dispatch",
    )(dst, ztile, h)


def _expert_kernel(te_ref, nt_ref, xs_ref, w1_ref, w3_ref, w2_ref, y_ref, w1b, w3b, w2b):
    t = pl.program_id(0)
    changed = jnp.logical_or(t == 0, te_ref[t] != te_ref[jnp.maximum(t - 1, 0)])

    @pl.when(changed)
    def _():
        w1b[...] = w1_ref[...].astype(BF16)
        w3b[...] = w3_ref[...].astype(BF16)
        w2b[...] = w2_ref[...].astype(BF16)

    @pl.when(t < nt_ref[0])
    def _():
        xb = xs_ref[...].astype(BF16)
        hid = _silu(_dot(xb, w1b[...])) * _dot(xb, w3b[...])
        y_ref[...] = _dot(hid.astype(BF16), w2b[...])

    @pl.when(t >= nt_ref[0])
    def _():
        y_ref[...] = jnp.zeros(y_ref.shape, F32)


def _expert_call(tile_expert, n_tiles, xs, w1, w3, w2, layer):
    d = D_MODEL
    grid_spec = pltpu.PrefetchScalarGridSpec(
        num_scalar_prefetch=2,
        grid=(MOE_MAX_TILES,),
        in_specs=[pl.BlockSpec((MOE_TILE, d), lambda t, te, nt: (jnp.minimum(t, nt[0] - 1), 0)),
                  pl.BlockSpec((None, None, d, D_EXPERT), lambda t, te, nt: (layer, te[t], 0, 0)),
                  pl.BlockSpec((None, None, d, D_EXPERT), lambda t, te, nt: (layer, te[t], 0, 0)),
                  pl.BlockSpec((None, None, D_EXPERT, d), lambda t, te, nt: (layer, te[t], 0, 0))],
        out_specs=pl.BlockSpec((MOE_TILE, d), lambda t, te, nt: (t, 0)),
        scratch_shapes=[pltpu.VMEM((d, D_EXPERT), BF16), pltpu.VMEM((d, D_EXPERT), BF16),
                        pltpu.VMEM((D_EXPERT, d), BF16)])
    return pl.pallas_call(
        _expert_kernel,
        grid_spec=grid_spec,
        out_shape=jax.ShapeDtypeStruct((MOE_ROWS, d), F32),
        compiler_params=pltpu.CompilerParams(dimension_semantics=("arbitrary",)),
        name="moe_experts",
    )(tile_expert, n_tiles, xs, w1, w3, w2)


def _combine_kernel(dst_ref, x_ref, mod_ref, route_ref, y_ref, o_ref, b1, b2, sem):
    tm = x_ref.shape[0]
    d = D_MODEL

    def issue(j, carry):
        for i in range(2):
            r = 2 * j + i
            _row_copy(y_ref, dst_ref[0, r], b1, r, sem).start(priority=i)
            _row_copy(y_ref, dst_ref[0, tm + r], b2, r, sem).start(priority=1 - i)
        return carry

    def drain(r, carry):
        _row_copy(y_ref, 0, b1, 0, sem).wait()
        _row_copy(y_ref, 0, b2, 0, sem).wait()
        return carry

    lax.fori_loop(0, tm // 2, issue, 0, unroll=4)
    lax.fori_loop(0, tm, drain, 0, unroll=8)
    rec = route_ref[...]
    moe = rec[:, R_W1:R_W1 + 1] * b1[...] + rec[:, R_W2:R_W2 + 1] * b2[...]
    o_ref[...] = x_ref[...] + mod_ref[:, 5 * d:6 * d] * moe


def _combine_call(dst, x, mod3, route, y, layer):
    tm = dst.shape[-1] // 2
    d = D_MODEL
    return pl.pallas_call(
        _combine_kernel,
        grid=(N_TOK // tm,),
        in_specs=[pl.BlockSpec((None, 1, 2 * tm), lambda i: (i, 0, 0), memory_space=pltpu.SMEM),
                  pl.BlockSpec((tm, d), lambda i: (i, 0)),
                  pl.BlockSpec((None, 1, 6 * d), lambda i: (_mod_row(i, tm, layer), 0, 0)),
                  pl.BlockSpec((tm, LANE), lambda i: (i, 0)),
                  pl.BlockSpec(memory_space=pl.ANY)],
        out_specs=pl.BlockSpec((tm, d), lambda i: (i, 0)),
        out_shape=jax.ShapeDtypeStruct((N_TOK, d), F32),
        scratch_shapes=[pltpu.VMEM((tm, d), F32), pltpu.VMEM((tm, d), F32), pltpu.SemaphoreType.DMA(())],
        compiler_params=pltpu.CompilerParams(dimension_semantics=("arbitrary",)),
        name="moe_combine",
    )(dst, x, mod3, route, y)


def _moe_routed(x, mod3, nw, rwh, rwl, rb, tri, w1, w3, w2, layer):
    h, route, wrec, cnt = _route_call(x, mod3, nw, rwh, rwl, rb, tri, layer)
    per_group = N_EXPERTS // N_EXPERT_GROUPS
    counts = cnt[:, 0].reshape(per_group, N_EXPERT_GROUPS).T.reshape(N_EXPERTS).astype(jnp.int32)
    ntile = (counts + MOE_TILE - 1) // MOE_TILE
    tend = jnp.cumsum(ntile)
    row0 = (tend - ntile) * MOE_TILE
    n_tiles = tend[-1:]
    tids = jnp.arange(MOE_MAX_TILES)
    tile_expert = jnp.minimum(jnp.sum(tend[None, :] <= jnp.minimum(tids, n_tiles[0] - 1)[:, None], axis=1),
                              N_EXPERTS - 1).astype(jnp.int32)
    ztile = jnp.concatenate([jnp.where(ntile > 0, row0 + (ntile - 1) * MOE_TILE, -1), n_tiles])
    ztile = ztile.astype(jnp.int32).reshape(1, N_EXPERTS + 1)
    eids = jnp.arange(N_EXPERTS, dtype=F32)
    row0f = row0.astype(F32)

    def dest(e, p):
        return (jnp.sum(jnp.where(e[:, None] == eids[None, :], row0f[None, :], 0.0), axis=1) + p).astype(jnp.int32)

    tm = MOE_TOKEN_TILE
    dst = jnp.concatenate([dest(route[R_E1], route[R_P1]).reshape(N_TOK // tm, 1, tm),
                           dest(route[R_E2], route[R_P2]).reshape(N_TOK // tm, 1, tm)], axis=-1)
    xs = _dispatch_call(dst, ztile, h)
    y = _expert_call(tile_expert, n_tiles.astype(jnp.int32), xs, w1, w3, w2, layer)
    return _combine_call(dst, x, mod3, wrec, y, layer)


def _pack_w_in(w):
    p = _col_slices(w)
    z = lambda n: jnp.zeros((D_MODEL, n), w.dtype)
    main = jnp.concatenate([
        p['a_cq'], p['a_ckv'], _head_slot(jnp.concatenate([z(A_NOPE), p['a_krope']], axis=1)), z(U_QK - U_KR - HEAD_PAD),
        p['b_q'], p['b_k'], p['b_v'], p['b_og'], p['c_z'], p['c_xbc'][:, :C_INNER], p['gates'],
        p['c_xbc'][:, C_INNER:]], axis=1)
    small = jnp.concatenate([p['b_gf'], p['b_gb'], p['c_dtf'], p['c_dtb'], p['a_krope'],
                             z(LANE - S_KR - A_ROPE)], axis=1)
    return main.astype(BF16), small.astype(BF16)


def _pad_heads(t, real):
    pad = jnp.zeros(t.shape[:-1] + (HEAD_PAD - real,), t.dtype)
    t = jnp.concatenate([t, pad], axis=-1)
    return t.reshape(t.shape[:-2] + (A_HEADS * HEAD_PAD,))


def _head_slot(t):
    t = t[..., _SLOT_SRC]
    return jnp.concatenate([t, jnp.zeros(t.shape[:-1] + (HEAD_PAD - A_QK,), t.dtype)], axis=-1)


def _qk_gain(g, scale):
    return (_head_slot(g) * scale).reshape(1, HEAD_PAD)


def _lane_slot(vals, off):
    return jnp.zeros((1, LANE), F32).at[0, off:off + vals.shape[0]].set(vals.astype(F32))


def _rope_tables():
    t = jnp.arange(DEC_SEQ)
    row = (t // GRID_W).astype(F32)
    col = (t % GRID_W).astype(F32)
    n_freq = A_ROPE // 4
    inv_freq = 1.0 / (ROPE_BASE ** (jnp.arange(n_freq, dtype=F32) / n_freq))
    ang = jnp.concatenate([row[:, None] * inv_freq, col[:, None] * inv_freq], axis=-1)
    cos, sin = jnp.cos(ang), jnp.sin(ang)
    ones = lambda n: jnp.ones((DEC_SEQ, n), F32)
    zeros = lambda n: jnp.zeros((DEC_SEQ, n), F32)
    n_hi = HEAD_PAD // 2 - A_ROPE // 2
    ctab = jnp.concatenate([cos, ones(_N_LOW), cos, ones(n_hi)], axis=-1)
    stab = jnp.concatenate([-sin, zeros(_N_LOW), sin, zeros(n_hi)], axis=-1)
    return ctab, stab


def kernel(x_prompt, x_sample, cache_ckv, cache_krope, state_gla, state_ssd, c, c_ctx, w_ada, b_ada, norm1, norm2, w_in, a_q_norm, a_wq, a_kv_norm, a_wkv, a_qk_qnorm, a_qk_knorm, a_wo, b_wg, b_bg, b_onorm, b_wo, c_conv_w, c_conv_b, c_dt_bias, c_A_log, c_D, c_norm, c_wo, w_out, router_w, router_bias, e_w1, e_w3, e_w2):
    d = D_MODEL
    x = jnp.concatenate([x_prompt.reshape(N_CTX, d), x_sample.reshape(N_DEC, d)], axis=0)

    cvecs = jnp.zeros((8, d), F32).at[0].set(c_ctx).at[1:1 + DEC_BATCH].set(c)
    mod3 = _mod_call(cvecs, w_ada, b_ada).reshape(DEPTH * 8, 1, 6 * d)

    rope_tabs = _rope_tables()
    gla_np = (_gla_consts(False), _gla_consts(True))
    gla_mall = jnp.asarray(np.stack([gla_np[0][0], gla_np[1][0]]), BF16)
    gla_masks = jnp.asarray(np.stack([gla_np[0][1], gla_np[1][1]]))
    ssd_consts = _ssd_consts()
    per_group = N_EXPERTS // N_EXPERT_GROUPS
    k_major = np.arange(N_EXPERTS).reshape(N_EXPERT_GROUPS, per_group).T.reshape(-1)
    rwh, rwl = _split2(router_w.T[k_major])
    rb = router_bias.astype(F32)[k_major].reshape(N_EXPERTS, 1)
    ridx = np.arange(1024)
    route_tri = jnp.asarray(ridx[:, None] < ridx[None, :], BF16)

    ckvs, kropes, glas, ssds = [], [], [], []
    for l in range(DEPTH):
        w_main, w_small = _pack_w_in(w_in[l])
        u, s_all = _norm_proj_call(x, mod3, norm1[l].reshape(1, d), w_main, w_small, l)
        kropes.append(s_all[:N_CTX, S_KR:S_KR + A_ROPE].reshape(BATCH, SEQ, A_ROPE))

        pw = A_HEADS * HEAD_PAD
        wq = _head_slot(a_wq[l].reshape(A_QLORA, A_HEADS, A_QK)).reshape(A_QLORA, pw)
        wkv = a_wkv[l].reshape(A_KVLORA, A_HEADS, A_NOPE + A_V)
        wkn = jnp.concatenate([wkv[..., :A_NOPE], jnp.zeros((A_KVLORA, A_HEADS, A_ROPE), wkv.dtype)], axis=-1)
        aw = dict(qnw=a_q_norm[l].reshape(1, A_QLORA), wq=wq.astype(BF16),
                  gq=_qk_gain(a_qk_qnorm[l], A_QK ** -0.5),
                  kvnw=a_kv_norm[l].reshape(1, A_KVLORA),
                  wkn=_head_slot(wkn).reshape(A_KVLORA, pw).astype(BF16),
                  wv=_pad_heads(wkv[..., A_NOPE:], A_V).astype(BF16),
                  gk=_qk_gain(a_qk_knorm[l], 1.0))
        q_c, k_c, v_c, ckvn = _attn_prep_call(N_CTX, 0, (u,), aw, has_q=True, norm_kv=True, rope=False, emit_ckv=True)
        o_c = _attn_call(q_c, k_c, v_c, None, n_seq=BATCH, seq_len=SEQ, hps=A_HEADS, tq=SEQ)
        ckvs.append(ckvn.reshape(BATCH, SEQ, A_KVLORA))
        q_d, k_d, v_d = _attn_prep_call(N_DEC, N_CTX, (u,), aw, has_q=True, norm_kv=True, rope=True, emit_ckv=False,
                                        tables=rope_tabs)
        kr_x = _head_slot(jnp.concatenate([jnp.zeros((DEC_BATCH * PAST_LEN, A_NOPE), F32),
                                            cache_krope[:, l].reshape(DEC_BATCH * PAST_LEN, A_ROPE)], axis=1))
        k_x, v_x = _attn_prep_call(DEC_BATCH * PAST_LEN, 0, (cache_ckv[:, l].reshape(DEC_BATCH * PAST_LEN, A_KVLORA), kr_x),
                                   aw, has_q=False, norm_kv=False, rope=False, emit_ckv=False)
        o_d = _attn_call(q_d, k_d, v_d, (k_x, v_x), n_seq=DEC_BATCH, seq_len=DEC_SEQ, hps=2, tq=512)

        hk = B_HEADS * B_DK
        wg = (jnp.zeros((2, LANE, hk), F32).at[0, S_GF:S_GF + B_GATE_RANK].set(b_wg[l, 0])
              .at[1, S_GB:S_GB + B_GATE_RANK].set(b_wg[l, 1]))
        wgh, wgl = _split2(wg)
        bg = b_bg[l].reshape(2, 1, hk)
        g_fc, g_bc, st_gc = _gla_call(u, s_all, wgh, wgl, bg, gla_mall, gla_masks, None,
                                      n_seq=BATCH, seq_len=SEQ, row0=0)
        g_fd, g_bd, _ = _gla_call(u, s_all, wgh, wgl, bg, gla_mall, gla_masks, jnp.swapaxes(state_gla[:, l], -1, -2),
                                  n_seq=DEC_BATCH, seq_len=DEC_SEQ, row0=N_CTX)
        glas.append(jnp.swapaxes(st_gc, -1, -2))

        w8 = jnp.zeros((8, C_XBC), F32).at[:C_CONV].set(c_conv_w[l])
        cb = c_conv_b[l].reshape(1, C_XBC)
        xbc_c = _conv_call(u, w8, cb, n_seq=BATCH, seq_len=SEQ, row0=0)
        xbc_d = _conv_call(u, w8, cb, n_seq=DEC_BATCH, seq_len=DEC_SEQ, row0=N_CTX)
        dtb = jnp.stack([_lane_slot(c_dt_bias[l, 0], S_DTF), _lane_slot(c_dt_bias[l, 1], S_DTB)])
        a_neg = -jnp.exp(c_A_log[l].astype(F32))
        aneg = jnp.stack([_lane_slot(a_neg[0], S_DTF), _lane_slot(a_neg[1], S_DTB)])
        y_fc, y_bc, st_c = _ssd_call(xbc_c, s_all, dtb, aneg, ssd_consts, None, n_seq=BATCH, seq_len=SEQ, row0=0)
        st0 = state_ssd[:, l].reshape(DEC_BATCH, 2, C_HEADS // 2, LANE, C_STATE)
        y_fd, y_bd, _ = _ssd_call(xbc_d, s_all, dtb, aneg, ssd_consts, st0, n_seq=DEC_BATCH, seq_len=DEC_SEQ, row0=N_CTX)
        ssds.append(st_c.reshape(BATCH, 2, C_HEADS, C_HEADDIM, C_STATE))

        awo = a_wo[l].reshape(A_HEADS, A_V, d)
        awo = jnp.concatenate([awo, jnp.zeros((A_HEADS, HEAD_PAD - A_V, d), awo.dtype)], axis=1)
        mw = dict(awo=awo.reshape(A_HEADS * HEAD_PAD, d).astype(BF16), onw=b_onorm[l].reshape(1, B_DV),
                  bwo=b_wo[l].astype(BF16), dexp=jnp.repeat(c_D[l], C_HEADDIM).reshape(1, C_INNER),
                  cnw=c_norm[l].reshape(1, C_INNER), cwo=c_wo[l].astype(BF16), wout=w_out[l].astype(BF16))
        x = _merge_call(x, mod3, u, o_c, g_fc, g_bc, y_fc, y_bc, xbc_c, mw, n_rows=N_CTX, row0=0, layer=l)
        x = _merge_call(x, mod3, u, o_d, g_fd, g_bd, y_fd, y_bd, xbc_d, mw, n_rows=N_DEC, row0=N_CTX, layer=l)

        x = _moe_routed(x, mod3, norm2[l].reshape(1, d), rwh, rwl, rb, route_tri, e_w1, e_w3, e_w2, l)

    y_prompt = x[:N_CTX].reshape(BATCH, SEQ, d)
    y_sample = x[N_CTX:].reshape(DEC_BATCH, DEC_SEQ, d)
    new_ckv = jnp.stack(ckvs, axis=1)
    new_krope = jnp.stack(kropes, axis=1)
    new_state_gla = jnp.stack(glas, axis=1)
    new_state_ssd = jnp.stack(ssds, axis=1)
    return (y_prompt, y_sample, new_ckv, new_krope, new_state_gla, new_state_ssd)
```

```python
import functools
import math

import numpy as np
import jax
import jax.numpy as jnp
from jax import lax
from jax.experimental import pallas as pl
from jax.experimental.pallas import tpu as pltpu

F32 = jnp.float32
BF16 = jnp.bfloat16

D_MODEL = 1024
BATCH = 16
SEQ = 256
DEPTH = 2
DEC_BATCH = 2
DEC_SEQ = 2048
PAST_LEN = 512
GRID_W = 64
EPS = 1e-6
ROPE_BASE = 10000.0

A_HEADS = 16
A_NOPE = 64
A_ROPE = 32
A_QK = A_NOPE + A_ROPE
A_V = 64
A_QLORA = 512
A_KVLORA = 256

B_HEADS = 4
B_DK = 128
B_DV = 256
B_GATE_RANK = 16
B_GATE_NORM = 16.0

C_HEADS = 16
C_HEADDIM = 64
C_INNER = C_HEADS * C_HEADDIM
C_GROUPS = 2
C_STATE = 128
C_XBC = C_INNER + 2 * C_GROUPS * C_STATE
C_CONV = 5

N_EXPERTS = 16
N_EXPERT_GROUPS = 4
D_EXPERT = 512

N_CTX = BATCH * SEQ
N_DEC = DEC_BATCH * DEC_SEQ
N_TOK = N_CTX + N_DEC

LANE = 128
HEAD_PAD = 128
CHUNK = 128

U_CQ = 0
U_CKV = 512
U_KR = 768
U_QK = 1024
U_V = 2048
U_OG = 3072
U_Z = 4096
U_X = 5120
U_GATES = 6144
U_BC = 9216
U_WIDTH = 9728
S_GF, S_GB, S_DTF, S_DTB, S_KR = 0, 16, 32, 48, 64

_IN_SPLITS = (
    ('a_cq', A_QLORA), ('a_ckv', A_KVLORA), ('a_krope', A_ROPE),
    ('b_q', B_HEADS * B_DK), ('b_k', B_HEADS * B_DK), ('b_v', B_HEADS * B_DV),
    ('b_og', B_HEADS * B_DV), ('b_gf', B_GATE_RANK), ('b_gb', B_GATE_RANK),
    ('c_z', C_INNER), ('c_xbc', C_XBC), ('c_dtf', C_HEADS), ('c_dtb', C_HEADS),
    ('gates', 3 * D_MODEL),
)
_N_LOW = HEAD_PAD // 2 - A_ROPE // 2
_SLOT_SRC = np.concatenate([A_NOPE + np.arange(0, A_ROPE, 2), np.arange(0, _N_LOW),
                            A_NOPE + np.arange(1, A_ROPE, 2), np.arange(_N_LOW, A_NOPE)])


def _col_slices(w):
    parts, start = {}, 0
    for name, size in _IN_SPLITS:
        parts[name] = w[:, start:start + size]
        start += size
    return parts


def _dot(a, b):
    return jnp.dot(a, b, preferred_element_type=F32)


def _dot_nt(a, b):
    return lax.dot_general(a, b, (((1,), (1,)), ((), ())), preferred_element_type=F32)


def _dot_tn(a, b):
    return lax.dot_general(a, b, (((0,), (0,)), ((), ())), preferred_element_type=F32)


def _split2(x):
    hi = x.astype(BF16)
    lo = (x - hi.astype(F32)).astype(BF16)
    return hi, lo


def _silu(x):
    return x * jax.nn.sigmoid(x)


def _softplus(x):
    return jnp.maximum(x, 0.0) + jnp.log1p(jnp.exp(-jnp.abs(x)))


def _log_sigmoid(x):
    return jnp.minimum(x, 0.0) - jnp.log1p(jnp.exp(-jnp.abs(x)))


def _mod_kernel(c_ref, w_ref, b_ref, o_ref):
    s = _silu(c_ref[...]).astype(BF16)
    o_ref[...] = _dot(s, w_ref[...].astype(BF16)) + b_ref[...]


def _mod_call(cvecs, w_ada, b_ada):
    tn = 1536
    return pl.pallas_call(
        _mod_kernel,
        grid=(DEPTH, 6 * D_MODEL // tn),
        in_specs=[pl.BlockSpec((8, D_MODEL), lambda l, j: (0, 0)),
                  pl.BlockSpec((None, D_MODEL, tn), lambda l, j: (l, 0, j)),
                  pl.BlockSpec((None, 1, tn), lambda l, j: (l, 0, j))],
        out_specs=pl.BlockSpec((None, 8, tn), lambda l, j: (l, 0, j)),
        out_shape=jax.ShapeDtypeStruct((DEPTH, 8, 6 * D_MODEL), F32),
        name="adaln_mod",
    )(cvecs, w_ada, b_ada.reshape(DEPTH, 1, 6 * D_MODEL))


def _mod_row(i, tm, layer):
    n_ctx = N_CTX // tm
    per_b = DEC_SEQ // tm
    return layer * 8 + jnp.where(i < n_ctx, 0, 1 + (i - n_ctx) // per_b)


def _norm_proj_kernel(x_ref, mod_ref, nw_ref, w_ref, ws_ref, u_ref, s_ref, h_scr):
    @pl.when(pl.program_id(1) == 0)
    def _():
        x = x_ref[...]
        m = mod_ref[...]
        y = x * lax.rsqrt(jnp.mean(x * x, axis=-1, keepdims=True) + EPS) * nw_ref[...]
        h = (y * (1.0 + m[:, D_MODEL:2 * D_MODEL]) + m[:, 0:D_MODEL]).astype(BF16)
        h_scr[...] = h
        s_ref[...] = _dot(h, ws_ref[...])

    u_ref[...] = _dot(h_scr[...], w_ref[...]).astype(BF16)


def _norm_proj_call(x, mod3, nw, w_main, w_small, layer):
    tm, tn = 2048, 512
    return pl.pallas_call(
        _norm_proj_kernel,
        grid=(N_TOK // tm, U_WIDTH // tn),
        in_specs=[pl.BlockSpec((tm, D_MODEL), lambda i, j: (i, 0)),
                  pl.BlockSpec((None, 1, 6 * D_MODEL), lambda i, j: (_mod_row(i, tm, layer), 0, 0)),
                  pl.BlockSpec((1, D_MODEL), lambda i, j: (0, 0)),
                  pl.BlockSpec((D_MODEL, tn), lambda i, j: (0, j)),
                  pl.BlockSpec((D_MODEL, LANE), lambda i, j: (0, 0))],
        out_specs=[pl.BlockSpec((tm, tn), lambda i, j: (i, j)),
                   pl.BlockSpec((tm, LANE), lambda i, j: (i, 0))],
        out_shape=[jax.ShapeDtypeStruct((N_TOK, U_WIDTH), BF16),
                   jax.ShapeDtypeStruct((N_TOK, LANE), F32)],
        scratch_shapes=[pltpu.VMEM((tm, D_MODEL), BF16)],
        compiler_params=pltpu.CompilerParams(dimension_semantics=("parallel", "arbitrary")),
        name="norm_in_proj",
    )(x, mod3, nw, w_main, w_small)


def _rope(t, cos, sin):
    return t * cos + pltpu.roll(t, HEAD_PAD // 2, 1) * sin


def _head_scale(ss):
    return lax.rsqrt(ss * (1.0 / A_QK) + EPS)


def _attn_prep_kernel(*refs, has_q, norm_kv, rope, emit_ckv):
    refs = list(refs)
    cq_ref = refs.pop(0) if has_q else None
    ckv_ref, kr_ref = refs.pop(0), refs.pop(0)
    if has_q:
        qnw_ref, wq_ref, gq_ref = refs.pop(0), refs.pop(0), refs.pop(0)
    kvnw_ref, wkn_ref, wv_ref, gk_ref = refs.pop(0), refs.pop(0), refs.pop(0), refs.pop(0)
    cos = sin = None
    if rope:
        cos, sin = refs.pop(0)[...], refs.pop(0)[...]
    q_ref = refs.pop(0) if has_q else None
    k_ref, v_ref = refs.pop(0), refs.pop(0)
    ckvn_ref = refs.pop(0) if emit_ckv else None

    ckv = ckv_ref[...].astype(F32)
    if norm_kv:
        ckv = ckv * lax.rsqrt(jnp.mean(ckv * ckv, axis=-1, keepdims=True) + EPS) * kvnw_ref[...]
    if emit_ckv:
        ckvn_ref[...] = ckv
    ckv_b = ckv.astype(BF16)
    v_ref[...] = _dot(ckv_b, wv_ref[...]).astype(BF16)
    kf = _dot(ckv_b, wkn_ref[...])
    kr = kr_ref[...].astype(F32)
    gk = gk_ref[...]
    ss_kr = jnp.sum(kr * kr, axis=-1, keepdims=True)
    krg = kr * gk
    if rope:
        krg = _rope(krg, cos, sin)
    for h in range(A_HEADS):
        sl = slice(h * HEAD_PAD, (h + 1) * HEAD_PAD)
        kn = kf[:, sl]
        scale = _head_scale(jnp.sum(kn * kn, axis=-1, keepdims=True) + ss_kr)
        k_ref[:, sl] = ((kn * gk + krg) * scale).astype(BF16)

    if has_q:
        cq = cq_ref[...].astype(F32)
        cq = cq * lax.rsqrt(jnp.mean(cq * cq, axis=-1, keepdims=True) + EPS) * qnw_ref[...]
        qf = _dot(cq.astype(BF16), wq_ref[...])
        gq = gq_ref[...]
        for h in range(A_HEADS):
            sl = slice(h * HEAD_PAD, (h + 1) * HEAD_PAD)
            t = qf[:, sl]
            t = t * _head_scale(jnp.sum(t * t, axis=-1, keepdims=True)) * gq
            q_ref[:, sl] = (_rope(t, cos, sin) if rope else t).astype(BF16)


def _attn_prep_call(n_rows, row0, srcs, wts, *, has_q, norm_kv, rope, emit_ckv, tables=None):
    tm = 256
    r0 = row0 // tm
    pw = A_HEADS * HEAD_PAD
    full = lambda shape: pl.BlockSpec(shape, lambda i: (0,) * len(shape))
    args, in_specs = [], []
    if len(srcs) == 1:
        u = srcs[0]
        if has_q:
            args.append(u)
            in_specs.append(pl.BlockSpec((tm, A_QLORA), lambda i: (i + r0, U_CQ // A_QLORA)))
        args += [u, u]
        in_specs += [pl.BlockSpec((tm, A_KVLORA), lambda i: (i + r0, U_CKV // A_KVLORA)),
                     pl.BlockSpec((tm, LANE), lambda i: (i + r0, U_KR // LANE))]
    else:
        args += list(srcs)
        in_specs += [pl.BlockSpec((tm, A_KVLORA), lambda i: (i, 0)),
                     pl.BlockSpec((tm, LANE), lambda i: (i, 0))]
    if has_q:
        args += [wts['qnw'], wts['wq'], wts['gq']]
        in_specs += [full((1, A_QLORA)), full((A_QLORA, pw)), full((1, HEAD_PAD))]
    args += [wts['kvnw'], wts['wkn'], wts['wv'], wts['gk']]
    in_specs += [full((1, A_KVLORA)), full((A_KVLORA, pw)), full((A_KVLORA, pw)), full((1, HEAD_PAD))]
    if rope:
        per_seq = DEC_SEQ // tm
        args += list(tables)
        in_specs += [pl.BlockSpec((tm, HEAD_PAD), lambda i: (i % per_seq, 0))] * 2
    out_specs, out_shape = [], []
    for _ in range((1 if has_q else 0) + 2):
        out_specs.append(pl.BlockSpec((tm, pw), lambda i: (i, 0)))
        out_shape.append(jax.ShapeDtypeStruct((n_rows, pw), BF16))
    if emit_ckv:
        out_specs.append(pl.BlockSpec((tm, A_KVLORA), lambda i: (i, 0)))
        out_shape.append(jax.ShapeDtypeStruct((n_rows, A_KVLORA), F32))
    return pl.pallas_call(
        functools.partial(_attn_prep_kernel, has_q=has_q, norm_kv=norm_kv, rope=rope, emit_ckv=emit_ckv),
        grid=(n_rows // tm,),
        in_specs=in_specs, out_specs=out_specs, out_shape=out_shape,
        compiler_params=pltpu.CompilerParams(dimension_semantics=("parallel",)),
        name="mla_prep",
    )(*args)


def _attn_kernel(*refs, hps, has_ctx):
    if has_ctx:
        q_ref, k_ref, v_ref, kc_ref, vc_ref, o_ref = refs
    else:
        q_ref, k_ref, v_ref, o_ref = refs

    def scores(hh):
        sl = slice(hh * HEAD_PAD, (hh + 1) * HEAD_PAD)
        q = q_ref[:, sl]
        return _dot_nt(q, k_ref[:, sl]), (_dot_nt(q, kc_ref[:, sl]) if has_ctx else None)

    nxt = scores(0)
    even = None
    for hh in range(hps):
        sl = slice(hh * HEAD_PAD, (hh + 1) * HEAD_PAD)
        s, s2 = nxt
        if hh + 1 < hps:
            nxt = scores(hh + 1)
        m = jnp.max(s, axis=-1, keepdims=True)
        if has_ctx:
            m = jnp.maximum(m, jnp.max(s2, axis=-1, keepdims=True))
        p = jnp.exp(s - m)
        den = jnp.sum(p, axis=-1, keepdims=True)
        o = _dot(p.astype(BF16), v_ref[:, sl])
        if has_ctx:
            p2 = jnp.exp(s2 - m)
            den = den + jnp.sum(p2, axis=-1, keepdims=True)
            o = o + _dot(p2.astype(BF16), vc_ref[:, sl])
        o = o / den
        if hh % 2 == 0:
            even = o
        else:
            pair = hh // 2
            o_ref[:, pair * LANE:(pair + 1) * LANE] = (even + o).astype(BF16)


def _attn_call(q, k, v, ctx_kv, *, n_seq, seq_len, hps, tq):
    pw = A_HEADS * HEAD_PAD
    bw = hps * HEAD_PAD
    nq = seq_len // tq
    in_specs = [pl.BlockSpec((tq, bw), lambda b, h, i: (b * nq + i, h)),
                pl.BlockSpec((seq_len, bw), lambda b, h, i: (b, h)),
                pl.BlockSpec((seq_len, bw), lambda b, h, i: (b, h))]
    args = [q, k, v]
    if ctx_kv is not None:
        in_specs += [pl.BlockSpec((PAST_LEN, bw), lambda b, h, i: (b, h))] * 2
        args += list(ctx_kv)
    return pl.pallas_call(
        functools.partial(_attn_kernel, hps=hps, has_ctx=ctx_kv is not None),
        grid=(n_seq, A_HEADS // hps, nq),
        in_specs=in_specs,
        out_specs=pl.BlockSpec((tq, hps * A_V), lambda b, h, i: (b * nq + i, h)),
        out_shape=jax.ShapeDtypeStruct((n_seq * seq_len, A_HEADS * A_V), BF16),
        compiler_params=pltpu.CompilerParams(dimension_semantics=("parallel", "parallel", "arbitrary")),
        name="mla_attention",
    )(*args)


def _conv_kernel(x_ref, w_ref, b_ref, o_ref):
    x = x_ref[...].astype(F32)
    n = x.shape[0]
    row = lax.broadcasted_iota(jnp.int32, x.shape, 0)
    half = (C_CONV - 1) // 2
    acc = x * w_ref[half:half + 1, :] + b_ref[...]
    for d in range(-half, half + 1):
        if d == 0:
            continue
        shifted = pltpu.roll(x, (-d) % n, 0)
        valid = jnp.logical_and(row + d >= 0, row + d < n)
        acc = acc + jnp.where(valid, shifted, 0.0) * w_ref[half + d:half + d + 1, :]
    o_ref[...] = _silu(acc).astype(BF16)


def _conv_call(u, w8, b, *, n_seq, seq_len, row0):
    tc = 512
    b0 = row0 // seq_len
    nx = C_INNER // tc
    ucol = lambda j: jnp.where(j < nx, U_X // tc + j, U_BC // tc + j - nx)
    return pl.pallas_call(
        _conv_kernel,
        grid=(n_seq, C_XBC // tc),
        in_specs=[pl.BlockSpec((seq_len, tc), lambda s, j: (s + b0, ucol(j))),
                  pl.BlockSpec((8, tc), lambda s, j: (0, j)),
                  pl.BlockSpec((1, tc), lambda s, j: (0, j))],
        out_specs=pl.BlockSpec((seq_len, tc), lambda s, j: (s, j)),
        out_shape=jax.ShapeDtypeStruct((n_seq * seq_len, C_XBC), BF16),
        compiler_params=pltpu.CompilerParams(dimension_semantics=("parallel", "parallel")),
        name="ssd_conv",
    )(u, w8, b)


def _ssd_consts():
    idx = np.arange(CHUNK)
    tri = np.stack([idx[None, :] <= idx[:, None], idx[None, :] >= idx[:, None]]).astype(np.float32)
    expand = np.zeros((2, LANE, C_INNER), np.float32)
    tile = np.zeros((2, LANE, C_HEADS * LANE), np.float32)
    for di, off in enumerate((S_DTF, S_DTB)):
        for h in range(C_HEADS):
            expand[di, off + h, h * C_HEADDIM:(h + 1) * C_HEADDIM] = 1.0
            tile[di, off + h, h * LANE:(h + 1) * LANE] = 1.0
    return jnp.asarray(tri, BF16), jnp.asarray(expand, BF16), jnp.asarray(tile, BF16)


def _ssd_kernel(*refs, has_init, n_prev):
    refs = list(refs)
    xs = [(refs.pop(0), refs.pop(0), refs.pop(0)) for _ in range(2)]
    dtb_ref, an_ref, tri_ref, exp_ref, tile_ref = (refs.pop(0) for _ in range(5))
    st0_ref = refs.pop(0) if has_init else None
    prev_ref = refs.pop(0) if n_prev else None
    y_refs = (refs.pop(0), refs.pop(0))
    stf_ref, st_scr = refs

    @pl.when(pl.program_id(1) == 0)
    def _():
        st_scr[...] = st0_ref[...] if has_init else jnp.zeros(st_scr.shape, F32)

    t = CHUNK
    ri = lax.broadcasted_iota(jnp.int32, (t, t), 0)
    ci = lax.broadcasted_iota(jnp.int32, (t, t), 1)
    prow = lax.broadcasted_iota(jnp.int32, (LANE, C_STATE), 0)
    gs = C_GROUPS * C_STATE
    pairs_per_group = C_HEADS // C_GROUPS // 2
    for di in range(2):
        x_ref, bc_ref, s_ref = xs[di]
        y_ref = y_refs[di]
        off = (S_DTF, S_DTB)[di]
        dt = _softplus(s_ref[...] + dtb_ref[di])
        a = dt * an_ref[di]
        a_hi, a_lo = _split2(a)
        tri = tri_ref[di]
        acum = _dot(tri, a_hi) + _dot(tri, a_lo)
        acum_t = acum.T
        dt_t = dt.T
        edge = 0 if di else t - 1
        atot = acum[edge:edge + 1, :]
        causal = (ci >= ri) if di else (ci <= ri)
        dec_all = jnp.exp(atot)
        e_exp = _dot(jnp.exp(acum).astype(BF16), exp_ref[di])
        wj_exp = _dot((jnp.exp(atot - acum) * dt).astype(BF16), exp_ref[di])
        c_hi, c_lo = _split2(acum)
        acol = _dot(c_hi, tile_ref[di]) + _dot(c_lo, tile_ref[di])
        bc = bc_ref[...]
        cb, bmat, cmat = [], [], []
        for g in range(C_GROUPS):
            bmat.append(bc[:, g * C_STATE:(g + 1) * C_STATE])
            cmat.append(bc[:, gs + g * C_STATE:gs + (g + 1) * C_STATE])
            cb.append(_dot_nt(cmat[g], bmat[g]))
        lane = lax.broadcasted_iota(jnp.int32, (t, LANE), 1)
        low = lane < C_HEADDIM
        for m in range(C_HEADS // 2):
            g = m // pairs_per_group
            ps = slice(m * LANE, (m + 1) * LANE)
            xp = x_ref[:, ps]
            y_diag = []
            for hh in range(2):
                h = 2 * m + hh
                la = off + h
                lm = jnp.exp(jnp.where(causal, acol[:, h * LANE:(h + 1) * LANE] - acum_t[la:la + 1, :], -jnp.inf))
                w = (cb[g] * lm * dt_t[la:la + 1, :]).astype(BF16)
                y_diag.append(_dot(w, xp))
            st = st_scr[di, m]
            y_off = _dot_nt(cmat[g], st.astype(BF16)) * e_exp[:, ps]
            y_ref[:, ps] = (jnp.where(low, y_diag[0], y_diag[1]) + y_off).astype(BF16)
            xw = (xp.astype(F32) * wj_exp[:, ps]).astype(BF16)
            la = off + 2 * m
            dec = jnp.where(prow < C_HEADDIM, dec_all[:, la:la + 1], dec_all[:, la + 1:la + 2])
            st_scr[di, m] = dec * st + _dot_tn(xw, bmat[g])

    @pl.when(pl.program_id(1) == pl.num_programs(1) - 1)
    def _():
        if n_prev:
            stf_ref[0:n_prev] = prev_ref[...]
            stf_ref[n_prev] = st_scr[...]
        else:
            stf_ref[...] = st_scr[...]


def _ssd_call(xbc, s_all, dtb, aneg, consts, st0, *, n_seq, seq_len, row0, prev=None):
    t = CHUNK
    nch = seq_len // t
    r0 = row0 // t
    npair = C_HEADS // 2
    bcw = 2 * C_GROUPS * C_STATE
    tri, expand, tile = consts
    full = lambda a: pl.BlockSpec(a.shape, lambda b, c: (0,) * a.ndim)
    in_specs, args = [], []
    for cidx in (lambda c: c, lambda c: nch - 1 - c):
        in_specs += [pl.BlockSpec((t, C_INNER), lambda b, c, cidx=cidx: (b * nch + cidx(c), 0)),
                     pl.BlockSpec((t, bcw), lambda b, c, cidx=cidx: (b * nch + cidx(c), C_INNER // bcw)),
                     pl.BlockSpec((t, LANE), lambda b, c, cidx=cidx: (r0 + b * nch + cidx(c), 0))]
        args += [xbc, xbc, s_all]
    in_specs += [full(dtb), full(aneg), full(tri), full(expand), full(tile)]
    args += [dtb, aneg, tri, expand, tile]
    st_shape = (2, npair, LANE, C_STATE)
    st_spec = pl.BlockSpec((None,) + st_shape, lambda b, c: (b, 0, 0, 0, 0))
    if st0 is not None:
        in_specs.append(st_spec)
        args.append(st0)
    n_prev = 0 if prev is None else prev.shape[1]
    if n_prev:
        in_specs.append(pl.BlockSpec((None, n_prev) + st_shape, lambda b, c: (b, 0, 0, 0, 0, 0)))
        args.append(prev)
        st_shape = (n_prev + 1,) + st_shape
        st_spec = pl.BlockSpec((None,) + st_shape, lambda b, c: (b, 0, 0, 0, 0, 0))
    return pl.pallas_call(
        functools.partial(_ssd_kernel, has_init=st0 is not None, n_prev=n_prev),
        grid=(n_seq, nch),
        in_specs=in_specs,
        out_specs=[pl.BlockSpec((t, C_INNER), lambda b, c: (b * nch + c, 0)),
                   pl.BlockSpec((t, C_INNER), lambda b, c: (b * nch + nch - 1 - c, 0)),
                   st_spec],
        out_shape=[jax.ShapeDtypeStruct((n_seq * seq_len, C_INNER), BF16),
                   jax.ShapeDtypeStruct((n_seq * seq_len, C_INNER), BF16),
                   jax.ShapeDtypeStruct((n_seq,) + st_shape, F32)],
        scratch_shapes=[pltpu.VMEM((2, npair, LANE, C_STATE), F32)],
        compiler_params=pltpu.CompilerParams(dimension_semantics=("parallel", "arbitrary")),
        name="ssd_scan",
    )(*args)


def _gla_consts(bwd):
    t = CHUNK
    nlev = int(math.log2(t))
    idx = np.arange(t)
    mats = np.zeros((nlev + 2, t, t), np.float32)
    masks = np.zeros((nlev + 1, t, t), np.float32)
    masks[0] = np.eye(t)
    for lvl in range(nlev):
        s = 1 << lvl
        blk = idx // (2 * s)
        upper = (idx % (2 * s)) >= s
        last_low = blk * 2 * s + s - 1
        rowtok = ~upper if bwd else upper
        for i in range(t):
            r = last_low[i]
            if not bwd:
                if upper[i]:
                    mats[lvl, i, r + 1:i + 1] = 1.0
                else:
                    mats[lvl, i, i + 1:r + 1] = 1.0
            else:
                if upper[i]:
                    mats[lvl, i, r + 1:i] = 1.0
                else:
                    mats[lvl, i, i:r + 1] = 1.0
        masks[lvl + 1] = ((blk[:, None] == blk[None, :]) & rowtok[:, None] & (~rowtok)[None, :])
    incl = (idx[None, :] >= idx[:, None]) if bwd else (idx[None, :] <= idx[:, None])
    mats[nlev] = incl
    mats[nlev + 1] = 1.0 - incl
    return mats.reshape((nlev + 2) * t, t), masks


def _gla_kernel(*refs, has_init, n_prev):
    refs = list(refs)
    xs = [(refs.pop(0), refs.pop(0), refs.pop(0)) for _ in range(2)]
    wgh_ref, wgl_ref, bg_ref, mall_ref, mask_ref = (refs.pop(0) for _ in range(5))
    st0_ref = refs.pop(0) if has_init else None
    prev_ref = refs.pop(0) if n_prev else None
    o_refs = (refs.pop(0), refs.pop(0))
    stf_ref, st_scr = refs

    @pl.when(pl.program_id(1) == 0)
    def _():
        st_scr[...] = st0_ref[...] if has_init else jnp.zeros(st_scr.shape, F32)

    t = CHUNK
    nlev = mask_ref.shape[1] - 1
    hk = B_HEADS * B_DK
    e_alls = []
    for di in range(2):
        s_hi, s_lo = _split2(xs[di][2][...])
        wgh = wgh_ref[di]
        logit = _dot(s_hi, wgh) + _dot(s_lo, wgh) + _dot(s_hi, wgl_ref[di]) + bg_ref[di]
        g_all = _log_sigmoid(logit) * (1.0 / B_GATE_NORM)
        e_alls.append(jnp.exp(_dot(mall_ref[di], g_all.astype(BF16))))
    for h in range(B_HEADS):
        for di in range(2):
            qk_ref, v_ref, _ = xs[di]
            o_ref = o_refs[di]
            e_all = e_alls[di]
            edge = 0 if di else t - 1
            ks = slice(h * B_DK, (h + 1) * B_DK)
            vs = slice(h * B_DV, (h + 1) * B_DV)
            q = qk_ref[:, ks].astype(F32) * (B_DK ** -0.5)
            k = qk_ref[:, hk + h * B_DK:hk + (h + 1) * B_DK]
            kf = k.astype(F32)
            v = v_ref[:, vs]
            amat = _dot_nt(q.astype(BF16), k) * mask_ref[di, 0]
            for lvl in range(nlev):
                e = e_all[lvl * t:(lvl + 1) * t, ks]
                amat = amat + _dot_nt((q * e).astype(BF16), (kf * e).astype(BF16)) * mask_ref[di, lvl + 1]
            e_in = e_all[nlev * t:(nlev + 1) * t, ks]
            e_out = e_all[(nlev + 1) * t:(nlev + 2) * t, ks]
            st = st_scr[di, h]
            o = _dot(amat.astype(BF16), v) + _dot_nt((q * e_in).astype(BF16), st.astype(BF16))
            o_ref[:, vs] = o.astype(BF16)
            st_scr[di, h] = st * e_in[edge:edge + 1, :] + _dot_tn(v, (kf * e_out).astype(BF16))

    @pl.when(pl.program_id(1) == pl.num_programs(1) - 1)
    def _():
        if n_prev:
            for di in range(2):
                for h in range(B_HEADS):
                    for lp in range(n_prev):
                        stf_ref[lp, di, h] = prev_ref[lp, di, h].T
                    stf_ref[n_prev, di, h] = st_scr[di, h].T
        else:
            stf_ref[...] = st_scr[...]


def _gla_call(u, s_all, wgh, wgl, bg, mall, masks, st0, *, n_seq, seq_len, row0, prev=None):
    t = CHUNK
    nch = seq_len // t
    r0 = row0 // t
    hk, hv = B_HEADS * B_DK, B_HEADS * B_DV
    full = lambda a: pl.BlockSpec(a.shape, lambda b, c: (0,) * a.ndim)
    in_specs, args = [], []
    for cidx in (lambda c: c, lambda c: nch - 1 - c):
        in_specs += [pl.BlockSpec((t, 2 * hk), lambda b, c, cidx=cidx: (r0 + b * nch + cidx(c), U_QK // (2 * hk))),
                     pl.BlockSpec((t, hv), lambda b, c, cidx=cidx: (r0 + b * nch + cidx(c), U_V // hv)),
                     pl.BlockSpec((t, LANE), lambda b, c, cidx=cidx: (r0 + b * nch + cidx(c), 0))]
        args += [u, u, s_all]
    in_specs += [full(wgh), full(wgl), full(bg), full(mall), full(masks)]
    args += [wgh, wgl, bg, mall, masks]
    st_shape = (2, B_HEADS, B_DV, B_DK)
    st_spec = pl.BlockSpec((None,) + st_shape, lambda b, c: (b, 0, 0, 0, 0))
    if st0 is not None:
        in_specs.append(st_spec)
        args.append(st0)
    n_prev = 0 if prev is None else prev.shape[1]
    if n_prev:
        in_specs.append(pl.BlockSpec((None, n_prev) + st_shape, lambda b, c: (b, 0, 0, 0, 0, 0)))
        args.append(prev)
        st_shape = (n_prev + 1, 2, B_HEADS, B_DK, B_DV)
        st_spec = pl.BlockSpec((None,) + st_shape, lambda b, c: (b, 0, 0, 0, 0, 0))
    return pl.pallas_call(
        functools.partial(_gla_kernel, has_init=st0 is not None, n_prev=n_prev),
        grid=(n_seq, nch),
        in_specs=in_specs,
        out_specs=[pl.BlockSpec((t, hv), lambda b, c: (b * nch + c, 0)),
                   pl.BlockSpec((t, hv), lambda b, c: (b * nch + nch - 1 - c, 0)),
                   st_spec],
        out_shape=[jax.ShapeDtypeStruct((n_seq * seq_len, hv), BF16),
                   jax.ShapeDtypeStruct((n_seq * seq_len, hv), BF16),
                   jax.ShapeDtypeStruct((n_seq,) + st_shape, F32)],
        scratch_shapes=[pltpu.VMEM((2, B_HEADS, B_DV, B_DK), F32)],
        compiler_params=pltpu.CompilerParams(dimension_semantics=("parallel", "arbitrary")),
        name="gla_scan",
    )(*args)


def _merge_kernel(x_ref, mod_ref, oa_ref, gf_ref, gb_ref, og_ref, yf_ref, yb_ref, xc_ref, z_ref, gt_ref,
                  awo_ref, onw_ref, bwo_ref, dexp_ref, cnw_ref, cwo_ref, wout_ref, o_ref):
    o_a = _dot(oa_ref[...], awo_ref[...])

    og = gf_ref[...].astype(F32) + gb_ref[...].astype(F32)
    onw = onw_ref[...]
    parts = []
    for h in range(B_HEADS):
        th = og[:, h * B_DV:(h + 1) * B_DV]
        parts.append(th * lax.rsqrt(jnp.mean(th * th, axis=-1, keepdims=True) + EPS) * onw)
    ob_in = jnp.concatenate(parts, axis=-1) * _silu(og_ref[...].astype(F32))
    o_b = _dot(ob_in.astype(BF16), bwo_ref[...])

    y = yf_ref[...].astype(F32) + yb_ref[...].astype(F32) + dexp_ref[...] * xc_ref[...].astype(F32)
    y = y * _silu(z_ref[...].astype(F32))
    y = y * lax.rsqrt(jnp.mean(y * y, axis=-1, keepdims=True) + EPS) * cnw_ref[...]
    o_c = _dot(y.astype(BF16), cwo_ref[...])

    d = D_MODEL
    merged = (jax.nn.sigmoid(gt_ref[:, 0:d].astype(F32)) * o_a
              + jax.nn.sigmoid(gt_ref[:, d:2 * d].astype(F32)) * o_b
              + jax.nn.sigmoid(gt_ref[:, 2 * d:3 * d].astype(F32)) * o_c)
    out = _dot(merged.astype(BF16), wout_ref[...])
    o_ref[...] = x_ref[...] + mod_ref[:, 2 * d:3 * d] * out


def _merge_call(x, mod3, u, o_attn, g_f, g_b, y_f, y_b, xbc, wts, *, n_rows, row0, layer):
    tm = 512
    r0 = row0 // tm
    d = D_MODEL
    loc = lambda w: pl.BlockSpec((tm, w), lambda i: (i, 0))
    full = lambda a: pl.BlockSpec(a.shape, lambda i: (0,) * a.ndim)
    in_specs = [pl.BlockSpec((tm, d), lambda i: (i + r0, 0)),
                pl.BlockSpec((None, 1, 6 * d), lambda i: (_mod_row(i + r0, tm, layer), 0, 0)),
                loc(A_HEADS * A_V), loc(d), loc(d),
                pl.BlockSpec((tm, d), lambda i: (i + r0, U_OG // d)),
                loc(d), loc(d),
                pl.BlockSpec((tm, d), lambda i: (i, 0)),
                pl.BlockSpec((tm, d), lambda i: (i + r0, U_Z // d)),
                pl.BlockSpec((tm, 3 * d), lambda i: (i + r0, U_GATES // (3 * d)))]
    w_args = [wts[n] for n in ('awo', 'onw', 'bwo', 'dexp', 'cnw', 'cwo', 'wout')]
    in_specs += [full(a) for a in w_args]
    return pl.pallas_call(
        _merge_kernel,
        grid=(n_rows // tm,),
        in_specs=in_specs,
        out_specs=pl.BlockSpec((tm, d), lambda i: (i + r0, 0)),
        out_shape=jax.ShapeDtypeStruct((N_TOK, d), F32),
        input_output_aliases={0: 0},
        compiler_params=pltpu.CompilerParams(dimension_semantics=("parallel",)),
        name="mixer_merge",
    )(x, mod3, o_attn, g_f, g_b, u, y_f, y_b, xbc, u, u, *w_args)


MOE_TILE = 512
MOE_TOKEN_TILE = 512
MOE_MAX_TILES = 2 * N_TOK // MOE_TILE + N_EXPERTS
MOE_ROWS = MOE_MAX_TILES * MOE_TILE
R_E1, R_E2, R_W1, R_W2, R_P1, R_P2 = 0, 1, 2, 3, 4, 5


def _first_index(vals, target):
    idx = jnp.full(target.shape, len(vals) - 1, jnp.int32)
    for k in reversed(range(len(vals) - 1)):
        idx = jnp.where(vals[k] == target, k, idx)
    return idx


def _top2_of(vals):
    m1 = functools.reduce(jnp.maximum, vals)
    i1 = _first_index(vals, m1)
    rest = [jnp.where(i1 == k, -jnp.inf, v) for k, v in enumerate(vals)]
    m2 = functools.reduce(jnp.maximum, rest)
    return m1, i1, m2, _first_index(rest, m2)


def _pick_by(vals, idx):
    out = vals[-1]
    for k in reversed(range(len(vals) - 1)):
        out = jnp.where(idx == k, vals[k], out)
    return out


def _route_kernel(x_ref, mod_ref, nw_ref, rwh_ref, rwl_ref, rb_ref, tri_ref, h_ref, route_ref, wrec_ref, cnt_ref,
                  base_scr):
    d = D_MODEL
    ng = N_EXPERT_GROUPS
    per_group = N_EXPERTS // ng

    @pl.when(pl.program_id(0) == 0)
    def _():
        base_scr[...] = jnp.zeros(base_scr.shape, F32)

    x = x_ref[...]
    tm = x.shape[0]
    y = x * lax.rsqrt(jnp.mean(x * x, axis=-1, keepdims=True) + EPS) * nw_ref[...]
    h = y * (1.0 + mod_ref[:, 4 * d:5 * d]) + mod_ref[:, 3 * d:4 * d]
    h_ref[...] = h
    h_hi, h_lo = _split2(h)
    rwh = rwh_ref[...]
    logits = _dot_nt(rwh, h_hi) + _dot_nt(rwh, h_lo) + _dot_nt(rwl_ref[...], h_hi)
    scores = jax.nn.sigmoid(logits)
    sel = scores + rb_ref[...]
    sel_k = [sel[k * ng:(k + 1) * ng] for k in range(per_group)]
    sc_k = [scores[k * ng:(k + 1) * ng] for k in range(per_group)]
    m1, i1, m2, i2 = _top2_of(sel_k)
    gsum = m1 + m2
    rows = lambda a: [a[g:g + 1] for g in range(ng)]
    gs = rows(gsum)
    best_v, best_g = gs[0], jnp.zeros((1, tm), jnp.int32)
    for g in range(1, ng):
        upd = gs[g] > best_v
        best_g = jnp.where(upd, g, best_g)
        best_v = jnp.where(upd, gs[g], best_v)
    k1 = _pick_by(rows(i1), best_g)
    k2 = _pick_by(rows(i2), best_g)
    s1 = _pick_by(rows(_pick_by(sc_k, i1)), best_g)
    s2 = _pick_by(rows(_pick_by(sc_k, i2)), best_g)
    tot = s1 + s2
    in_best = lax.broadcasted_iota(jnp.int32, (ng, tm), 0) == best_g
    hit = jnp.concatenate(
        [jnp.where(jnp.logical_and(in_best, jnp.logical_or(k1 == k, k2 == k)), 1.0, 0.0) for k in range(per_group)],
        axis=0)
    rank = _dot(hit.astype(BF16), tri_ref[...]) + base_scr[...]
    rank_k = [rank[k * ng:(k + 1) * ng] for k in range(per_group)]
    p1 = _pick_by(rows(_pick_by(rank_k, k1)), best_g)
    p2 = _pick_by(rows(_pick_by(rank_k, k2)), best_g)
    base_scr[...] += jnp.sum(hit, axis=1, keepdims=True)
    e1 = (best_g * per_group + k1).astype(F32)
    e2 = (best_g * per_group + k2).astype(F32)
    rec = jnp.concatenate([e1, e2, s1 / tot, s2 / tot, p1, p2, jnp.zeros((LANE - 6, tm), F32)], axis=0)
    route_ref[...] = rec[0:8]
    wrec_ref[...] = rec.T
    cnt_ref[...] = jnp.broadcast_to(base_scr[...], cnt_ref.shape)


def _route_call(x, mod3, nw, rwh, rwl, rb, tri, layer):
    tm = tri.shape[0]
    d = D_MODEL
    full = lambda a: pl.BlockSpec(a.shape, lambda i: (0,) * a.ndim)
    return pl.pallas_call(
        _route_kernel,
        grid=(N_TOK // tm,),
        in_specs=[pl.BlockSpec((tm, d), lambda i: (i, 0)),
                  pl.BlockSpec((None, 1, 6 * d), lambda i: (_mod_row(i, tm, layer), 0, 0)),
                  full(nw), full(rwh), full(rwl), full(rb), full(tri)],
        out_specs=[pl.BlockSpec((tm, d), lambda i: (i, 0)),
                   pl.BlockSpec((8, tm), lambda i: (0, i)),
                   pl.BlockSpec((tm, LANE), lambda i: (i, 0)),
                   pl.BlockSpec((N_EXPERTS, LANE), lambda i: (0, 0))],
        out_shape=[jax.ShapeDtypeStruct((N_TOK, d), F32),
                   jax.ShapeDtypeStruct((8, N_TOK), F32),
                   jax.ShapeDtypeStruct((N_TOK, LANE), F32),
                   jax.ShapeDtypeStruct((N_EXPERTS, LANE), F32)],
        scratch_shapes=[pltpu.VMEM((N_EXPERTS, 1), F32)],
        compiler_params=pltpu.CompilerParams(dimension_semantics=("arbitrary",)),
        name="moe_route",
    )(x, mod3, nw, rwh, rwl, rb, tri)


def _row_copy(src, src_row, dst, dst_row, sem):
    return pltpu.make_async_copy(src.at[pl.ds(src_row, 1), :], dst.at[pl.ds(dst_row, 1), :], sem)


def _dispatch_kernel(dst_ref, ztile_ref, h_ref, xs_ref, zbuf, sem):
    tm = h_ref.shape[0]

    @pl.when(pl.program_id(0) == 0)
    def _():
        zbuf[...] = jnp.zeros(zbuf.shape, F32)
        fills = []
        for e in range(N_EXPERTS):
            row = pl.multiple_of(jnp.maximum(ztile_ref[0, e], 0), MOE_TILE)
            fills.append((ztile_ref[0, e] >= 0, pltpu.make_async_copy(zbuf, xs_ref.at[pl.ds(row, MOE_TILE), :], sem)))
        for has_tile, cp in fills:
            pl.when(has_tile)(cp.start)
        for has_tile, cp in fills:
            pl.when(has_tile)(cp.wait)

        def fill_unused(t, carry):
            cp = pltpu.make_async_copy(zbuf, xs_ref.at[pl.ds(pl.multiple_of(t * MOE_TILE, MOE_TILE), MOE_TILE), :], sem)
            cp.start()
            cp.wait()
            return carry

        lax.fori_loop(ztile_ref[0, N_EXPERTS], MOE_MAX_TILES, fill_unused, 0)

    def issue(j, carry):
        for i in range(2):
            r = 2 * j + i
            _row_copy(h_ref, r, xs_ref, dst_ref[0, r], sem).start(priority=i)
            _row_copy(h_ref, r, xs_ref, dst_ref[0, tm + r], sem).start(priority=1 - i)
        return carry

    def drain(r, carry):
        _row_copy(h_ref, 0, xs_ref, 0, sem).wait()
        _row_copy(h_ref, 0, xs_ref, 0, sem).wait()
        return carry

    lax.fori_loop(0, tm // 2, issue, 0, unroll=4)
    lax.fori_loop(0, tm, drain, 0, unroll=8)


def _dispatch_call(dst, ztile, h):
    tm = dst.shape[-1] // 2
    d = D_MODEL
    return pl.pallas_call(
        _dispatch_kernel,
        grid=(N_TOK // tm,),
        in_specs=[pl.BlockSpec((None, 1, 2 * tm), lambda i: (i, 0, 0), memory_space=pltpu.SMEM),
                  pl.BlockSpec((1, N_EXPERTS + 1), lambda i: (0, 0), memory_space=pltpu.SMEM),
                  pl.BlockSpec((tm, d), lambda i: (i, 0))],
        out_specs=pl.BlockSpec(memory_space=pl.ANY),
        out_shape=jax.ShapeDtypeStruct((MOE_ROWS, d), F32),
        scratch_shapes=[pltpu.VMEM((MOE_TILE, d), F32), pltpu.SemaphoreType.DMA(())],
        compiler_params=pltpu.CompilerParams(dimension_semantics=("arbitrary",)),
        name="moe_dispatch",
    )(dst, ztile, h)


def _expert_kernel(te_ref, nt_ref, xs_ref, w1_ref, w3_ref, w2_ref, y_ref, w1b, w3b, w2b):
    t = pl.program_id(0)
    changed = jnp.logical_or(t == 0, te_ref[t] != te_ref[jnp.maximum(t - 1, 0)])

    @pl.when(changed)
    def _():
        w1b[...] = w1_ref[...].astype(BF16)
        w3b[...] = w3_ref[...].astype(BF16)
        w2b[...] = w2_ref[...].astype(BF16)

    @pl.when(t < nt_ref[0])
    def _():
        xb = xs_ref[...].astype(BF16)
        hid = _silu(_dot(xb, w1b[...])) * _dot(xb, w3b[...])
        y_ref[...] = _dot(hid.astype(BF16), w2b[...])

    @pl.when(t >= nt_ref[0])
    def _():
        y_ref[...] = jnp.zeros(y_ref.shape, F32)


def _expert_call(tile_expert, n_tiles, xs, w1, w3, w2, layer):
    d = D_MODEL
    grid_spec = pltpu.PrefetchScalarGridSpec(
        num_scalar_prefetch=2,
        grid=(MOE_MAX_TILES,),
        in_specs=[pl.BlockSpec((MOE_TILE, d), lambda t, te, nt: (jnp.minimum(t, nt[0] - 1), 0)),
                  pl.BlockSpec((None, None, d, D_EXPERT), lambda t, te, nt: (layer, te[t], 0, 0)),
                  pl.BlockSpec((None, None, d, D_EXPERT), lambda t, te, nt: (layer, te[t], 0, 0)),
                  pl.BlockSpec((None, None, D_EXPERT, d), lambda t, te, nt: (layer, te[t], 0, 0))],
        out_specs=pl.BlockSpec((MOE_TILE, d), lambda t, te, nt: (t, 0)),
        scratch_shapes=[pltpu.VMEM((d, D_EXPERT), BF16), pltpu.VMEM((d, D_EXPERT), BF16),
                        pltpu.VMEM((D_EXPERT, d), BF16)])
    return pl.pallas_call(
        _expert_kernel,
        grid_spec=grid_spec,
        out_shape=jax.ShapeDtypeStruct((MOE_ROWS, d), F32),
        compiler_params=pltpu.CompilerParams(dimension_semantics=("arbitrary",)),
        name="moe_experts",
    )(tile_expert, n_tiles, xs, w1, w3, w2)


def _combine_kernel(dst_ref, x_ref, mod_ref, route_ref, y_ref, o_ref, b1, b2, sem):
    tm = x_ref.shape[0]
    d = D_MODEL

    def issue(j, carry):
        for i in range(2):
            r = 2 * j + i
            _row_copy(y_ref, dst_ref[0, r], b1, r, sem).start(priority=i)
            _row_copy(y_ref, dst_ref[0, tm + r], b2, r, sem).start(priority=1 - i)
        return carry

    def drain(r, carry):
        _row_copy(y_ref, 0, b1, 0, sem).wait()
        _row_copy(y_ref, 0, b2, 0, sem).wait()
        return carry

    lax.fori_loop(0, tm // 2, issue, 0, unroll=4)
    lax.fori_loop(0, tm, drain, 0, unroll=8)
    rec = route_ref[...]
    moe = rec[:, R_W1:R_W1 + 1] * b1[...] + rec[:, R_W2:R_W2 + 1] * b2[...]
    o_ref[...] = x_ref[...] + mod_ref[:, 5 * d:6 * d] * moe


def _combine_call(dst, x, mod3, route, y, layer):
    tm = dst.shape[-1] // 2
    d = D_MODEL
    return pl.pallas_call(
        _combine_kernel,
        grid=(N_TOK // tm,),
        in_specs=[pl.BlockSpec((None, 1, 2 * tm), lambda i: (i, 0, 0), memory_space=pltpu.SMEM),
                  pl.BlockSpec((tm, d), lambda i: (i, 0)),
                  pl.BlockSpec((None, 1, 6 * d), lambda i: (_mod_row(i, tm, layer), 0, 0)),
                  pl.BlockSpec((tm, LANE), lambda i: (i, 0)),
                  pl.BlockSpec(memory_space=pl.ANY)],
        out_specs=pl.BlockSpec((tm, d), lambda i: (i, 0)),
        out_shape=jax.ShapeDtypeStruct((N_TOK, d), F32),
        scratch_shapes=[pltpu.VMEM((tm, d), F32), pltpu.VMEM((tm, d), F32), pltpu.SemaphoreType.DMA(())],
        compiler_params=pltpu.CompilerParams(dimension_semantics=("arbitrary",)),
        name="moe_combine",
    )(dst, x, mod3, route, y)


def _moe_routed(x, mod3, nw, rwh, rwl, rb, tri, w1, w3, w2, layer):
    h, route, wrec, cnt = _route_call(x, mod3, nw, rwh, rwl, rb, tri, layer)
    per_group = N_EXPERTS // N_EXPERT_GROUPS
    counts = cnt[:, 0].reshape(per_group, N_EXPERT_GROUPS).T.reshape(N_EXPERTS).astype(jnp.int32)
    ntile = (counts + MOE_TILE - 1) // MOE_TILE
    tend = jnp.cumsum(ntile)
    row0 = (tend - ntile) * MOE_TILE
    n_tiles = tend[-1:]
    tids = jnp.arange(MOE_MAX_TILES)
    tile_expert = jnp.minimum(jnp.sum(tend[None, :] <= jnp.minimum(tids, n_tiles[0] - 1)[:, None], axis=1),
                              N_EXPERTS - 1).astype(jnp.int32)
    ztile = jnp.concatenate([jnp.where(ntile > 0, row0 + (ntile - 1) * MOE_TILE, -1), n_tiles])
    ztile = ztile.astype(jnp.int32).reshape(1, N_EXPERTS + 1)
    eids = jnp.arange(N_EXPERTS, dtype=F32)
    row0f = row0.astype(F32)

    def dest(e, p):
        return (jnp.sum(jnp.where(e[:, None] == eids[None, :], row0f[None, :], 0.0), axis=1) + p).astype(jnp.int32)

    tm = MOE_TOKEN_TILE
    dst = jnp.concatenate([dest(route[R_E1], route[R_P1]).reshape(N_TOK // tm, 1, tm),
                           dest(route[R_E2], route[R_P2]).reshape(N_TOK // tm, 1, tm)], axis=-1)
    xs = _dispatch_call(dst, ztile, h)
    y = _expert_call(tile_expert, n_tiles.astype(jnp.int32), xs, w1, w3, w2, layer)
    return _combine_call(dst, x, mod3, wrec, y, layer)


def _pack_w_in(w):
    p = _col_slices(w)
    z = lambda n: jnp.zeros((D_MODEL, n), w.dtype)
    main = jnp.concatenate([
        p['a_cq'], p['a_ckv'], _head_slot(jnp.concatenate([z(A_NOPE), p['a_krope']], axis=1)), z(U_QK - U_KR - HEAD_PAD),
        p['b_q'], p['b_k'], p['b_v'], p['b_og'], p['c_z'], p['c_xbc'][:, :C_INNER], p['gates'],
        p['c_xbc'][:, C_INNER:]], axis=1)
    small = jnp.concatenate([p['b_gf'], p['b_gb'], p['c_dtf'], p['c_dtb'], p['a_krope'],
                             z(LANE - S_KR - A_ROPE)], axis=1)
    return main.astype(BF16), small.astype(BF16)


def _pair_slots(v):
    z = jnp.zeros_like(v)
    odd = (np.arange(A_HEADS) % 2 == 1)[None, :, None]
    return jnp.concatenate([jnp.where(odd, z, v), jnp.where(odd, v, z)], axis=-1).reshape(v.shape[0], A_HEADS * HEAD_PAD)


def _head_slot(t):
    t = t[..., _SLOT_SRC]
    return jnp.concatenate([t, jnp.zeros(t.shape[:-1] + (HEAD_PAD - A_QK,), t.dtype)], axis=-1)


def _qk_gain(g, scale):
    return (_head_slot(g) * scale).reshape(1, HEAD_PAD)


def _lane_slot(vals, off):
    return jnp.zeros((1, LANE), F32).at[0, off:off + vals.shape[0]].set(vals.astype(F32))


def _rope_tables():
    t = jnp.arange(DEC_SEQ)
    row = (t // GRID_W).astype(F32)
    col = (t % GRID_W).astype(F32)
    n_freq = A_ROPE // 4
    inv_freq = 1.0 / (ROPE_BASE ** (jnp.arange(n_freq, dtype=F32) / n_freq))
    ang = jnp.concatenate([row[:, None] * inv_freq, col[:, None] * inv_freq], axis=-1)
    cos, sin = jnp.cos(ang), jnp.sin(ang)
    ones = lambda n: jnp.ones((DEC_SEQ, n), F32)
    zeros = lambda n: jnp.zeros((DEC_SEQ, n), F32)
    n_hi = HEAD_PAD // 2 - A_ROPE // 2
    ctab = jnp.concatenate([cos, ones(_N_LOW), cos, ones(n_hi)], axis=-1)
    stab = jnp.concatenate([-sin, zeros(_N_LOW), sin, zeros(n_hi)], axis=-1)
    return ctab, stab


def kernel(x_prompt, x_sample, cache_ckv, cache_krope, state_gla, state_ssd, c, c_ctx, w_ada, b_ada, norm1, norm2, w_in, a_q_norm, a_wq, a_kv_norm, a_wkv, a_qk_qnorm, a_qk_knorm, a_wo, b_wg, b_bg, b_onorm, b_wo, c_conv_w, c_conv_b, c_dt_bias, c_A_log, c_D, c_norm, c_wo, w_out, router_w, router_bias, e_w1, e_w3, e_w2):
    d = D_MODEL
    x = jnp.concatenate([x_prompt.reshape(N_CTX, d), x_sample.reshape(N_DEC, d)], axis=0)

    cvecs = jnp.zeros((8, d), F32).at[0].set(c_ctx).at[1:1 + DEC_BATCH].set(c)
    mod3 = _mod_call(cvecs, w_ada, b_ada).reshape(DEPTH * 8, 1, 6 * d)

    rope_tabs = _rope_tables()
    gla_np = (_gla_consts(False), _gla_consts(True))
    gla_mall = jnp.asarray(np.stack([gla_np[0][0], gla_np[1][0]]), BF16)
    gla_masks = jnp.asarray(np.stack([gla_np[0][1], gla_np[1][1]]))
    ssd_consts = _ssd_consts()
    per_group = N_EXPERTS // N_EXPERT_GROUPS
    k_major = np.arange(N_EXPERTS).reshape(N_EXPERT_GROUPS, per_group).T.reshape(-1)
    rwh, rwl = _split2(router_w.T[k_major])
    rb = router_bias.astype(F32)[k_major].reshape(N_EXPERTS, 1)
    ridx = np.arange(1024)
    route_tri = jnp.asarray(ridx[:, None] < ridx[None, :], BF16)

    ckvs, kropes, glas, ssds = [], [], [], []
    for l in range(DEPTH):
        w_main, w_small = _pack_w_in(w_in[l])
        u, s_all = _norm_proj_call(x, mod3, norm1[l].reshape(1, d), w_main, w_small, l)
        kropes.append(s_all[:N_CTX, S_KR:S_KR + A_ROPE].reshape(BATCH, SEQ, A_ROPE))

        pw = A_HEADS * HEAD_PAD
        wq = _head_slot(a_wq[l].reshape(A_QLORA, A_HEADS, A_QK)).reshape(A_QLORA, pw)
        wkv = a_wkv[l].reshape(A_KVLORA, A_HEADS, A_NOPE + A_V)
        wkn = jnp.concatenate([wkv[..., :A_NOPE], jnp.zeros((A_KVLORA, A_HEADS, A_ROPE), wkv.dtype)], axis=-1)
        aw = dict(qnw=a_q_norm[l].reshape(1, A_QLORA), wq=wq.astype(BF16),
                  gq=_qk_gain(a_qk_qnorm[l], A_QK ** -0.5),
                  kvnw=a_kv_norm[l].reshape(1, A_KVLORA),
                  wkn=_head_slot(wkn).reshape(A_KVLORA, pw).astype(BF16),
                  wv=_pair_slots(wkv[..., A_NOPE:]).astype(BF16),
                  gk=_qk_gain(a_qk_knorm[l], 1.0))
        q_c, k_c, v_c, ckvn = _attn_prep_call(N_CTX, 0, (u,), aw, has_q=True, norm_kv=True, rope=False, emit_ckv=True)
        o_c = _attn_call(q_c, k_c, v_c, None, n_seq=BATCH, seq_len=SEQ, hps=A_HEADS, tq=SEQ)
        ckvs.append(ckvn.reshape(BATCH, SEQ, A_KVLORA))
        q_d, k_d, v_d = _attn_prep_call(N_DEC, N_CTX, (u,), aw, has_q=True, norm_kv=True, rope=True, emit_ckv=False,
                                        tables=rope_tabs)
        kr_x = _head_slot(jnp.concatenate([jnp.zeros((DEC_BATCH * PAST_LEN, A_NOPE), F32),
                                            cache_krope[:, l].reshape(DEC_BATCH * PAST_LEN, A_ROPE)], axis=1))
        k_x, v_x = _attn_prep_call(DEC_BATCH * PAST_LEN, 0, (cache_ckv[:, l].reshape(DEC_BATCH * PAST_LEN, A_KVLORA), kr_x),
                                   aw, has_q=False, norm_kv=False, rope=False, emit_ckv=False)
        o_d = _attn_call(q_d, k_d, v_d, (k_x, v_x), n_seq=DEC_BATCH, seq_len=DEC_SEQ, hps=2, tq=512)

        hk = B_HEADS * B_DK
        wg = (jnp.zeros((2, LANE, hk), F32).at[0, S_GF:S_GF + B_GATE_RANK].set(b_wg[l, 0])
              .at[1, S_GB:S_GB + B_GATE_RANK].set(b_wg[l, 1]))
        wgh, wgl = _split2(wg)
        bg = b_bg[l].reshape(2, 1, hk)
        last = l == DEPTH - 1
        g_fc, g_bc, st_gc = _gla_call(u, s_all, wgh, wgl, bg, gla_mall, gla_masks, None, n_seq=BATCH, seq_len=SEQ, row0=0,
                                      prev=jnp.stack(glas, axis=1) if last and glas else None)
        g_fd, g_bd, _ = _gla_call(u, s_all, wgh, wgl, bg, gla_mall, gla_masks, jnp.swapaxes(state_gla[:, l], -1, -2),
                                  n_seq=DEC_BATCH, seq_len=DEC_SEQ, row0=N_CTX)
        glas.append(st_gc)

        w8 = jnp.zeros((8, C_XBC), F32).at[:C_CONV].set(c_conv_w[l])
        cb = c_conv_b[l].reshape(1, C_XBC)
        xbc_c = _conv_call(u, w8, cb, n_seq=BATCH, seq_len=SEQ, row0=0)
        xbc_d = _conv_call(u, w8, cb, n_seq=DEC_BATCH, seq_len=DEC_SEQ, row0=N_CTX)
        dtb = jnp.stack([_lane_slot(c_dt_bias[l, 0], S_DTF), _lane_slot(c_dt_bias[l, 1], S_DTB)])
        a_neg = -jnp.exp(c_A_log[l].astype(F32))
        aneg = jnp.stack([_lane_slot(a_neg[0], S_DTF), _lane_slot(a_neg[1], S_DTB)])
        y_fc, y_bc, st_c = _ssd_call(xbc_c, s_all, dtb, aneg, ssd_consts, None, n_seq=BATCH, seq_len=SEQ, row0=0,
                                     prev=jnp.stack(ssds, axis=1) if last and ssds else None)
        st0 = state_ssd[:, l].reshape(DEC_BATCH, 2, C_HEADS // 2, LANE, C_STATE)
        y_fd, y_bd, _ = _ssd_call(xbc_d, s_all, dtb, aneg, ssd_consts, st0, n_seq=DEC_BATCH, seq_len=DEC_SEQ, row0=N_CTX)
        ssds.append(st_c)

        mw = dict(awo=a_wo[l].astype(BF16), onw=b_onorm[l].reshape(1, B_DV),
                  bwo=b_wo[l].astype(BF16), dexp=jnp.repeat(c_D[l], C_HEADDIM).reshape(1, C_INNER),
                  cnw=c_norm[l].reshape(1, C_INNER), cwo=c_wo[l].astype(BF16), wout=w_out[l].astype(BF16))
        x = _merge_call(x, mod3, u, o_c, g_fc, g_bc, y_fc, y_bc, xbc_c, mw, n_rows=N_CTX, row0=0, layer=l)
        x = _merge_call(x, mod3, u, o_d, g_fd, g_bd, y_fd, y_bd, xbc_d, mw, n_rows=N_DEC, row0=N_CTX, layer=l)

        x = _moe_routed(x, mod3, norm2[l].reshape(1, d), rwh, rwl, rb, route_tri, e_w1, e_w3, e_w2, l)

    y_prompt = x[:N_CTX].reshape(BATCH, SEQ, d)
    y_sample = x[N_CTX:].reshape(DEC_BATCH, DEC_SEQ, d)
    new_ckv = jnp.stack(ckvs, axis=1)
    new_krope = jnp.stack(kropes, axis=1)
    new_state_gla = glas[-1] if DEPTH > 1 else jnp.swapaxes(glas[-1], -1, -2)[:, None]
    new_state_ssd = (ssds[-1] if DEPTH > 1 else ssds[-1][:, None]).reshape(BATCH, DEPTH, 2, C_HEADS, C_HEADDIM, C_STATE)
    return (y_prompt, y_sample, new_ckv, new_krope, new_state_gla, new_state_ssd)
```

```python
import functools
import math

import numpy as np
import jax
import jax.numpy as jnp
from jax import lax
from jax.experimental import pallas as pl
from jax.experimental.pallas import tpu as pltpu

F32 = jnp.float32
BF16 = jnp.bfloat16

D_MODEL = 1024
BATCH = 16
SEQ = 256
DEPTH = 2
DEC_BATCH = 2
DEC_SEQ = 2048
PAST_LEN = 512
GRID_W = 64
EPS = 1e-6
ROPE_BASE = 10000.0

A_HEADS = 16
A_NOPE = 64
A_ROPE = 32
A_QK = A_NOPE + A_ROPE
A_V = 64
A_QLORA = 512
A_KVLORA = 256

B_HEADS = 4
B_DK = 128
B_DV = 256
B_GATE_RANK = 16
B_GATE_NORM = 16.0

C_HEADS = 16
C_HEADDIM = 64
C_INNER = C_HEADS * C_HEADDIM
C_GROUPS = 2
C_STATE = 128
C_XBC = C_INNER + 2 * C_GROUPS * C_STATE
C_CONV = 5

N_EXPERTS = 16
N_EXPERT_GROUPS = 4
D_EXPERT = 512

N_CTX = BATCH * SEQ
N_DEC = DEC_BATCH * DEC_SEQ
N_TOK = N_CTX + N_DEC

LANE = 128
HEAD_PAD = 128
CHUNK = 128
GLA_CHUNK = 256

U_CQ = 0
U_CKV = 512
U_KR = 768
U_QK = 1024
U_V = 2048
U_OG = 3072
U_Z = 4096
U_X = 5120
U_GATES = 6144
U_BC = 9216
U_WIDTH = 9728
S_GF, S_GB, S_DTF, S_DTB, S_KR = 0, 16, 32, 48, 64

_IN_SPLITS = (
    ('a_cq', A_QLORA), ('a_ckv', A_KVLORA), ('a_krope', A_ROPE),
    ('b_q', B_HEADS * B_DK), ('b_k', B_HEADS * B_DK), ('b_v', B_HEADS * B_DV),
    ('b_og', B_HEADS * B_DV), ('b_gf', B_GATE_RANK), ('b_gb', B_GATE_RANK),
    ('c_z', C_INNER), ('c_xbc', C_XBC), ('c_dtf', C_HEADS), ('c_dtb', C_HEADS),
    ('gates', 3 * D_MODEL),
)
_N_LOW = HEAD_PAD // 2 - A_ROPE // 2
_SLOT_SRC = np.concatenate([A_NOPE + np.arange(0, A_ROPE, 2), np.arange(0, _N_LOW),
                            A_NOPE + np.arange(1, A_ROPE, 2), np.arange(_N_LOW, A_NOPE)])


def _col_slices(w):
    parts, start = {}, 0
    for name, size in _IN_SPLITS:
        parts[name] = w[:, start:start + size]
        start += size
    return parts


def _dot(a, b):
    return jnp.dot(a, b, preferred_element_type=F32)


def _dot_nt(a, b):
    return lax.dot_general(a, b, (((1,), (1,)), ((), ())), preferred_element_type=F32)


def _dot_tn(a, b):
    return lax.dot_general(a, b, (((0,), (0,)), ((), ())), preferred_element_type=F32)


def _split2(x):
    hi = x.astype(BF16)
    lo = (x - hi.astype(F32)).astype(BF16)
    return hi, lo


def _silu(x):
    return x * jax.nn.sigmoid(x)


def _softplus(x):
    return jnp.maximum(x, 0.0) + jnp.log1p(jnp.exp(-jnp.abs(x)))


def _log_sigmoid(x):
    return jnp.minimum(x, 0.0) - jnp.log1p(jnp.exp(-jnp.abs(x)))


def _mod_kernel(c_ref, w_ref, b_ref, o_ref):
    s = _silu(c_ref[...]).astype(BF16)
    o_ref[...] = _dot(s, w_ref[...].astype(BF16)) + b_ref[...]


def _mod_call(cvecs, w_ada, b_ada):
    tn = 1536
    return pl.pallas_call(
        _mod_kernel,
        grid=(DEPTH, 6 * D_MODEL // tn),
        in_specs=[pl.BlockSpec((8, D_MODEL), lambda l, j: (0, 0)),
                  pl.BlockSpec((None, D_MODEL, tn), lambda l, j: (l, 0, j)),
                  pl.BlockSpec((None, 1, tn), lambda l, j: (l, 0, j))],
        out_specs=pl.BlockSpec((None, 8, tn), lambda l, j: (l, 0, j)),
        out_shape=jax.ShapeDtypeStruct((DEPTH, 8, 6 * D_MODEL), F32),
        name="adaln_mod",
    )(cvecs, w_ada, b_ada.reshape(DEPTH, 1, 6 * D_MODEL))


def _mod_row(i, tm, layer):
    n_ctx = N_CTX // tm
    per_b = DEC_SEQ // tm
    return layer * 8 + jnp.where(i < n_ctx, 0, 1 + (i - n_ctx) // per_b)


def _norm_proj_kernel(x_ref, mod_ref, nw_ref, w_ref, ws_ref, u_ref, s_ref, h_scr):
    @pl.when(pl.program_id(1) == 0)
    def _():
        x = x_ref[...]
        m = mod_ref[...]
        y = x * lax.rsqrt(jnp.mean(x * x, axis=-1, keepdims=True) + EPS) * nw_ref[...]
        h = (y * (1.0 + m[:, D_MODEL:2 * D_MODEL]) + m[:, 0:D_MODEL]).astype(BF16)
        h_scr[...] = h
        s_ref[...] = _dot(h, ws_ref[...])

    u_ref[...] = _dot(h_scr[...], w_ref[...]).astype(BF16)


def _norm_proj_call(x, mod3, nw, w_main, w_small, layer):
    tm, tn = 2048, 512
    return pl.pallas_call(
        _norm_proj_kernel,
        grid=(N_TOK // tm, U_WIDTH // tn),
        in_specs=[pl.BlockSpec((tm, D_MODEL), lambda i, j: (i, 0)),
                  pl.BlockSpec((None, 1, 6 * D_MODEL), lambda i, j: (_mod_row(i, tm, layer), 0, 0)),
                  pl.BlockSpec((1, D_MODEL), lambda i, j: (0, 0)),
                  pl.BlockSpec((D_MODEL, tn), lambda i, j: (0, j)),
                  pl.BlockSpec((D_MODEL, LANE), lambda i, j: (0, 0))],
        out_specs=[pl.BlockSpec((tm, tn), lambda i, j: (i, j)),
                   pl.BlockSpec((tm, LANE), lambda i, j: (i, 0))],
        out_shape=[jax.ShapeDtypeStruct((N_TOK, U_WIDTH), BF16),
                   jax.ShapeDtypeStruct((N_TOK, LANE), F32)],
        scratch_shapes=[pltpu.VMEM((tm, D_MODEL), BF16)],
        compiler_params=pltpu.CompilerParams(dimension_semantics=("parallel", "arbitrary")),
        name="norm_in_proj",
    )(x, mod3, nw, w_main, w_small)


def _rope(t, cos, sin):
    return t * cos + pltpu.roll(t, HEAD_PAD // 2, 1) * sin


def _head_scale(ss):
    return lax.rsqrt(ss * (1.0 / A_QK) + EPS)


def _attn_prep_kernel(*refs, has_q, norm_kv, rope, emit_ckv):
    refs = list(refs)
    cq_ref = refs.pop(0) if has_q else None
    ckv_ref, kr_ref = refs.pop(0), refs.pop(0)
    if has_q:
        qnw_ref, wq_ref, gq_ref = refs.pop(0), refs.pop(0), refs.pop(0)
    kvnw_ref, wkn_ref, wv_ref, gk_ref = refs.pop(0), refs.pop(0), refs.pop(0), refs.pop(0)
    cos = sin = None
    if rope:
        cos, sin = refs.pop(0)[...], refs.pop(0)[...]
    q_ref = refs.pop(0) if has_q else None
    k_ref, v_ref = refs.pop(0), refs.pop(0)
    ckvn_ref = refs.pop(0) if emit_ckv else None

    ckv = ckv_ref[...].astype(F32)
    if norm_kv:
        ckv = ckv * lax.rsqrt(jnp.mean(ckv * ckv, axis=-1, keepdims=True) + EPS) * kvnw_ref[...]
    if emit_ckv:
        ckvn_ref[...] = ckv
    ckv_b = ckv.astype(BF16)
    v_ref[...] = _dot(ckv_b, wv_ref[...]).astype(BF16)
    kf = _dot(ckv_b, wkn_ref[...])
    kr = kr_ref[...].astype(F32)
    gk = gk_ref[...]
    ss_kr = jnp.sum(kr * kr, axis=-1, keepdims=True)
    krg = kr * gk
    if rope:
        krg = _rope(krg, cos, sin)
    for h in range(A_HEADS):
        sl = slice(h * HEAD_PAD, (h + 1) * HEAD_PAD)
        kn = kf[:, sl]
        scale = _head_scale(jnp.sum(kn * kn, axis=-1, keepdims=True) + ss_kr)
        k_ref[:, sl] = ((kn * gk + krg) * scale).astype(BF16)

    if has_q:
        cq = cq_ref[...].astype(F32)
        cq = cq * lax.rsqrt(jnp.mean(cq * cq, axis=-1, keepdims=True) + EPS) * qnw_ref[...]
        qf = _dot(cq.astype(BF16), wq_ref[...])
        gq = gq_ref[...]
        for h in range(A_HEADS):
            sl = slice(h * HEAD_PAD, (h + 1) * HEAD_PAD)
            t = qf[:, sl]
            t = t * _head_scale(jnp.sum(t * t, axis=-1, keepdims=True)) * gq
            q_ref[:, sl] = (_rope(t, cos, sin) if rope else t).astype(BF16)


def _attn_prep_call(n_rows, row0, srcs, wts, *, has_q, norm_kv, rope, emit_ckv, tables=None):
    tm = 256
    r0 = row0 // tm
    pw = A_HEADS * HEAD_PAD
    full = lambda shape: pl.BlockSpec(shape, lambda i: (0,) * len(shape))
    args, in_specs = [], []
    if len(srcs) == 1:
        u = srcs[0]
        if has_q:
            args.append(u)
            in_specs.append(pl.BlockSpec((tm, A_QLORA), lambda i: (i + r0, U_CQ // A_QLORA)))
        args += [u, u]
        in_specs += [pl.BlockSpec((tm, A_KVLORA), lambda i: (i + r0, U_CKV // A_KVLORA)),
                     pl.BlockSpec((tm, LANE), lambda i: (i + r0, U_KR // LANE))]
    else:
        args += list(srcs)
        in_specs += [pl.BlockSpec((tm, A_KVLORA), lambda i: (i, 0)),
                     pl.BlockSpec((tm, LANE), lambda i: (i, 0))]
    if has_q:
        args += [wts['qnw'], wts['wq'], wts['gq']]
        in_specs += [full((1, A_QLORA)), full((A_QLORA, pw)), full((1, HEAD_PAD))]
    args += [wts['kvnw'], wts['wkn'], wts['wv'], wts['gk']]
    in_specs += [full((1, A_KVLORA)), full((A_KVLORA, pw)), full((A_KVLORA, pw)), full((1, HEAD_PAD))]
    if rope:
        per_seq = DEC_SEQ // tm
        args += list(tables)
        in_specs += [pl.BlockSpec((tm, HEAD_PAD), lambda i: (i % per_seq, 0))] * 2
    out_specs, out_shape = [], []
    for _ in range((1 if has_q else 0) + 2):
        out_specs.append(pl.BlockSpec((tm, pw), lambda i: (i, 0)))
        out_shape.append(jax.ShapeDtypeStruct((n_rows, pw), BF16))
    if emit_ckv:
        out_specs.append(pl.BlockSpec((tm, A_KVLORA), lambda i: (i, 0)))
        out_shape.append(jax.ShapeDtypeStruct((n_rows, A_KVLORA), F32))
    return pl.pallas_call(
        functools.partial(_attn_prep_kernel, has_q=has_q, norm_kv=norm_kv, rope=rope, emit_ckv=emit_ckv),
        grid=(n_rows // tm,),
        in_specs=in_specs, out_specs=out_specs, out_shape=out_shape,
        compiler_params=pltpu.CompilerParams(dimension_semantics=("parallel",)),
        name="mla_prep",
    )(*args)


def _attn_kernel(*refs, hps, has_ctx):
    if has_ctx:
        q_ref, k_ref, v_ref, kc_ref, vc_ref, o_ref = refs
    else:
        q_ref, k_ref, v_ref, o_ref = refs

    def scores(hh):
        sl = slice(hh * HEAD_PAD, (hh + 1) * HEAD_PAD)
        q = q_ref[:, sl]
        return _dot_nt(q, k_ref[:, sl]), (_dot_nt(q, kc_ref[:, sl]) if has_ctx else None)

    nxt = scores(0)
    even = None
    for hh in range(hps):
        sl = slice(hh * HEAD_PAD, (hh + 1) * HEAD_PAD)
        s, s2 = nxt
        if hh + 1 < hps:
            nxt = scores(hh + 1)
        m = jnp.max(s, axis=-1, keepdims=True)
        if has_ctx:
            m = jnp.maximum(m, jnp.max(s2, axis=-1, keepdims=True))
        p = jnp.exp(s - m)
        den = jnp.sum(p, axis=-1, keepdims=True)
        o = _dot(p.astype(BF16), v_ref[:, sl])
        if has_ctx:
            p2 = jnp.exp(s2 - m)
            den = den + jnp.sum(p2, axis=-1, keepdims=True)
            o = o + _dot(p2.astype(BF16), vc_ref[:, sl])
        o = o / den
        if hh % 2 == 0:
            even = o
        else:
            pair = hh // 2
            o_ref[:, pair * LANE:(pair + 1) * LANE] = (even + o).astype(BF16)


def _attn_call(q, k, v, ctx_kv, *, n_seq, seq_len, hps, tq):
    pw = A_HEADS * HEAD_PAD
    bw = hps * HEAD_PAD
    nq = seq_len // tq
    in_specs = [pl.BlockSpec((tq, bw), lambda b, h, i: (b * nq + i, h)),
                pl.BlockSpec((seq_len, bw), lambda b, h, i: (b, h)),
                pl.BlockSpec((seq_len, bw), lambda b, h, i: (b, h))]
    args = [q, k, v]
    if ctx_kv is not None:
        in_specs += [pl.BlockSpec((PAST_LEN, bw), lambda b, h, i: (b, h))] * 2
        args += list(ctx_kv)
    return pl.pallas_call(
        functools.partial(_attn_kernel, hps=hps, has_ctx=ctx_kv is not None),
        grid=(n_seq, A_HEADS // hps, nq),
        in_specs=in_specs,
        out_specs=pl.BlockSpec((tq, hps * A_V), lambda b, h, i: (b * nq + i, h)),
        out_shape=jax.ShapeDtypeStruct((n_seq * seq_len, A_HEADS * A_V), BF16),
        compiler_params=pltpu.CompilerParams(dimension_semantics=("parallel", "parallel", "arbitrary")),
        name="mla_attention",
    )(*args)


def _conv_kernel(x_ref, w_ref, b_ref, o_ref):
    x = x_ref[...].astype(F32)
    n = x.shape[0]
    row = lax.broadcasted_iota(jnp.int32, x.shape, 0)
    half = (C_CONV - 1) // 2
    acc = x * w_ref[half:half + 1, :] + b_ref[...]
    for d in range(-half, half + 1):
        if d == 0:
            continue
        shifted = pltpu.roll(x, (-d) % n, 0)
        valid = jnp.logical_and(row + d >= 0, row + d < n)
        acc = acc + jnp.where(valid, shifted, 0.0) * w_ref[half + d:half + d + 1, :]
    o_ref[...] = _silu(acc).astype(BF16)


def _conv_call(u, w8, b, *, n_seq, seq_len, row0):
    tc = 512
    b0 = row0 // seq_len
    nx = C_INNER // tc
    ucol = lambda j: jnp.where(j < nx, U_X // tc + j, U_BC // tc + j - nx)
    return pl.pallas_call(
        _conv_kernel,
        grid=(n_seq, C_XBC // tc),
        in_specs=[pl.BlockSpec((seq_len, tc), lambda s, j: (s + b0, ucol(j))),
                  pl.BlockSpec((8, tc), lambda s, j: (0, j)),
                  pl.BlockSpec((1, tc), lambda s, j: (0, j))],
        out_specs=pl.BlockSpec((seq_len, tc), lambda s, j: (s, j)),
        out_shape=jax.ShapeDtypeStruct((n_seq * seq_len, C_XBC), BF16),
        compiler_params=pltpu.CompilerParams(dimension_semantics=("parallel", "parallel")),
        name="ssd_conv",
    )(u, w8, b)


def _ssd_consts():
    idx = np.arange(CHUNK)
    tri = np.stack([idx[None, :] <= idx[:, None], idx[None, :] >= idx[:, None]]).astype(np.float32)
    expand = np.zeros((2, LANE, C_INNER), np.float32)
    tile = np.zeros((2, LANE, C_HEADS * LANE), np.float32)
    for di, off in enumerate((S_DTF, S_DTB)):
        for h in range(C_HEADS):
            expand[di, off + h, h * C_HEADDIM:(h + 1) * C_HEADDIM] = 1.0
            tile[di, off + h, h * LANE:(h + 1) * LANE] = 1.0
    return jnp.asarray(tri, BF16), jnp.asarray(expand, BF16), jnp.asarray(tile, BF16)


def _ssd_kernel(*refs, has_init, n_prev):
    refs = list(refs)
    xs = [(refs.pop(0), refs.pop(0), refs.pop(0)) for _ in range(2)]
    dtb_ref, an_ref, tri_ref, exp_ref, tile_ref = (refs.pop(0) for _ in range(5))
    st0_ref = refs.pop(0) if has_init else None
    prev_ref = refs.pop(0) if n_prev else None
    y_refs = (refs.pop(0), refs.pop(0))
    stf_ref, st_scr = refs

    @pl.when(pl.program_id(1) == 0)
    def _():
        st_scr[...] = st0_ref[...] if has_init else jnp.zeros(st_scr.shape, F32)

    t = CHUNK
    ri = lax.broadcasted_iota(jnp.int32, (t, t), 0)
    ci = lax.broadcasted_iota(jnp.int32, (t, t), 1)
    prow = lax.broadcasted_iota(jnp.int32, (LANE, C_STATE), 0)
    gs = C_GROUPS * C_STATE
    pairs_per_group = C_HEADS // C_GROUPS // 2
    for di in range(2):
        x_ref, bc_ref, s_ref = xs[di]
        y_ref = y_refs[di]
        off = (S_DTF, S_DTB)[di]
        dt = _softplus(s_ref[...] + dtb_ref[di])
        a = dt * an_ref[di]
        a_hi, a_lo = _split2(a)
        tri = tri_ref[di]
        acum = _dot(tri, a_hi) + _dot(tri, a_lo)
        acum_t = acum.T
        dt_t = dt.T
        edge = 0 if di else t - 1
        atot = acum[edge:edge + 1, :]
        causal = (ci >= ri) if di else (ci <= ri)
        dec_all = jnp.exp(atot)
        e_exp = _dot(jnp.exp(acum).astype(BF16), exp_ref[di])
        wj_exp = _dot((jnp.exp(atot - acum) * dt).astype(BF16), exp_ref[di])
        c_hi, c_lo = _split2(acum)
        acol = _dot(c_hi, tile_ref[di]) + _dot(c_lo, tile_ref[di])
        bc = bc_ref[...]
        cb, bmat, cmat = [], [], []
        for g in range(C_GROUPS):
            bmat.append(bc[:, g * C_STATE:(g + 1) * C_STATE])
            cmat.append(bc[:, gs + g * C_STATE:gs + (g + 1) * C_STATE])
            cb.append(_dot_nt(cmat[g], bmat[g]))
        lane = lax.broadcasted_iota(jnp.int32, (t, LANE), 1)
        low = lane < C_HEADDIM
        for m in range(C_HEADS // 2):
            g = m // pairs_per_group
            ps = slice(m * LANE, (m + 1) * LANE)
            xp = x_ref[:, ps]
            y_diag = []
            for hh in range(2):
                h = 2 * m + hh
                la = off + h
                lm = jnp.exp(jnp.where(causal, acol[:, h * LANE:(h + 1) * LANE] - acum_t[la:la + 1, :], -jnp.inf))
                w = (cb[g] * lm * dt_t[la:la + 1, :]).astype(BF16)
                y_diag.append(_dot(w, xp))
            st = st_scr[di, m]
            y_off = _dot_nt(cmat[g], st.astype(BF16)) * e_exp[:, ps]
            y_ref[:, ps] = (jnp.where(low, y_diag[0], y_diag[1]) + y_off).astype(BF16)
            xw = (xp.astype(F32) * wj_exp[:, ps]).astype(BF16)
            la = off + 2 * m
            dec = jnp.where(prow < C_HEADDIM, dec_all[:, la:la + 1], dec_all[:, la + 1:la + 2])
            st_scr[di, m] = dec * st + _dot_tn(xw, bmat[g])

    @pl.when(pl.program_id(1) == pl.num_programs(1) - 1)
    def _():
        if n_prev:
            stf_ref[0:n_prev] = prev_ref[...]
            stf_ref[n_prev] = st_scr[...]
        else:
            stf_ref[...] = st_scr[...]


def _ssd_call(xbc, s_all, dtb, aneg, consts, st0, *, n_seq, seq_len, row0, prev=None):
    t = CHUNK
    nch = seq_len // t
    r0 = row0 // t
    npair = C_HEADS // 2
    bcw = 2 * C_GROUPS * C_STATE
    tri, expand, tile = consts
    full = lambda a: pl.BlockSpec(a.shape, lambda b, c: (0,) * a.ndim)
    in_specs, args = [], []
    for cidx in (lambda c: c, lambda c: nch - 1 - c):
        in_specs += [pl.BlockSpec((t, C_INNER), lambda b, c, cidx=cidx: (b * nch + cidx(c), 0)),
                     pl.BlockSpec((t, bcw), lambda b, c, cidx=cidx: (b * nch + cidx(c), C_INNER // bcw)),
                     pl.BlockSpec((t, LANE), lambda b, c, cidx=cidx: (r0 + b * nch + cidx(c), 0))]
        args += [xbc, xbc, s_all]
    in_specs += [full(dtb), full(aneg), full(tri), full(expand), full(tile)]
    args += [dtb, aneg, tri, expand, tile]
    st_shape = (2, npair, LANE, C_STATE)
    st_spec = pl.BlockSpec((None,) + st_shape, lambda b, c: (b, 0, 0, 0, 0))
    if st0 is not None:
        in_specs.append(st_spec)
        args.append(st0)
    n_prev = 0 if prev is None else prev.shape[1]
    if n_prev:
        in_specs.append(pl.BlockSpec((None, n_prev) + st_shape, lambda b, c: (b, 0, 0, 0, 0, 0)))
        args.append(prev)
        st_shape = (n_prev + 1,) + st_shape
        st_spec = pl.BlockSpec((None,) + st_shape, lambda b, c: (b, 0, 0, 0, 0, 0))
    return pl.pallas_call(
        functools.partial(_ssd_kernel, has_init=st0 is not None, n_prev=n_prev),
        grid=(n_seq, nch),
        in_specs=in_specs,
        out_specs=[pl.BlockSpec((t, C_INNER), lambda b, c: (b * nch + c, 0)),
                   pl.BlockSpec((t, C_INNER), lambda b, c: (b * nch + nch - 1 - c, 0)),
                   st_spec],
        out_shape=[jax.ShapeDtypeStruct((n_seq * seq_len, C_INNER), BF16),
                   jax.ShapeDtypeStruct((n_seq * seq_len, C_INNER), BF16),
                   jax.ShapeDtypeStruct((n_seq,) + st_shape, F32)],
        scratch_shapes=[pltpu.VMEM((2, npair, LANE, C_STATE), F32)],
        compiler_params=pltpu.CompilerParams(dimension_semantics=("parallel", "arbitrary")),
        name="ssd_scan",
    )(*args)


def _gla_consts(bwd):
    t = GLA_CHUNK
    nlev = int(math.log2(t))
    idx = np.arange(t)
    mats = np.zeros((nlev + 2, t, t), np.float32)
    masks = np.zeros((nlev + 1, t, t), np.float32)
    masks[0] = np.eye(t)
    for lvl in range(nlev):
        s = 1 << lvl
        blk = idx // (2 * s)
        upper = (idx % (2 * s)) >= s
        last_low = blk * 2 * s + s - 1
        rowtok = ~upper if bwd else upper
        for i in range(t):
            r = last_low[i]
            if not bwd:
                if upper[i]:
                    mats[lvl, i, r + 1:i + 1] = 1.0
                else:
                    mats[lvl, i, i + 1:r + 1] = 1.0
            else:
                if upper[i]:
                    mats[lvl, i, r + 1:i] = 1.0
                else:
                    mats[lvl, i, i:r + 1] = 1.0
        masks[lvl + 1] = ((blk[:, None] == blk[None, :]) & rowtok[:, None] & (~rowtok)[None, :])
    incl = (idx[None, :] >= idx[:, None]) if bwd else (idx[None, :] <= idx[:, None])
    mats[nlev] = incl
    mats[nlev + 1] = 1.0 - incl
    return mats.reshape((nlev + 2) * t, t), masks


def _gla_kernel(*refs, has_init, n_prev):
    refs = list(refs)
    xs = [(refs.pop(0), refs.pop(0), refs.pop(0)) for _ in range(2)]
    wgh_ref, wgl_ref, bg_ref, mall_ref, mask_ref = (refs.pop(0) for _ in range(5))
    st0_ref = refs.pop(0) if has_init else None
    prev_ref = refs.pop(0) if n_prev else None
    o_refs = (refs.pop(0), refs.pop(0))
    stf_ref, st_scr = refs

    @pl.when(pl.program_id(1) == 0)
    def _():
        st_scr[...] = st0_ref[...] if has_init else jnp.zeros(st_scr.shape, F32)

    t = GLA_CHUNK
    nlev = mask_ref.shape[1] - 1
    hk = B_HEADS * B_DK
    e_alls = []
    for di in range(2):
        s_hi, s_lo = _split2(xs[di][2][...])
        wgh = wgh_ref[di]
        logit = _dot(s_hi, wgh) + _dot(s_lo, wgh) + _dot(s_hi, wgl_ref[di]) + bg_ref[di]
        g_all = _log_sigmoid(logit) * (1.0 / B_GATE_NORM)
        e_alls.append(jnp.exp(_dot(mall_ref[di], g_all.astype(BF16))))
    for h in range(B_HEADS):
        for di in range(2):
            qk_ref, v_ref, _ = xs[di]
            o_ref = o_refs[di]
            e_all = e_alls[di]
            edge = 0 if di else t - 1
            ks = slice(h * B_DK, (h + 1) * B_DK)
            vs = slice(h * B_DV, (h + 1) * B_DV)
            q = qk_ref[:, ks].astype(F32) * (B_DK ** -0.5)
            k = qk_ref[:, hk + h * B_DK:hk + (h + 1) * B_DK]
            kf = k.astype(F32)
            v = v_ref[:, vs]
            amat = _dot_nt(q.astype(BF16), k) * mask_ref[di, 0]
            for lvl in range(nlev):
                e = e_all[lvl * t:(lvl + 1) * t, ks]
                amat = amat + _dot_nt((q * e).astype(BF16), (kf * e).astype(BF16)) * mask_ref[di, lvl + 1]
            e_in = e_all[nlev * t:(nlev + 1) * t, ks]
            e_out = e_all[(nlev + 1) * t:(nlev + 2) * t, ks]
            st = st_scr[di, h]
            o = _dot(amat.astype(BF16), v) + _dot_nt((q * e_in).astype(BF16), st.astype(BF16))
            o_ref[:, vs] = o.astype(BF16)
            st_scr[di, h] = st * e_in[edge:edge + 1, :] + _dot_tn(v, (kf * e_out).astype(BF16))

    @pl.when(pl.program_id(1) == pl.num_programs(1) - 1)
    def _():
        if n_prev:
            for di in range(2):
                for h in range(B_HEADS):
                    for lp in range(n_prev):
                        stf_ref[lp, di, h] = prev_ref[lp, di, h].T
                    stf_ref[n_prev, di, h] = st_scr[di, h].T
        else:
            stf_ref[...] = st_scr[...]


def _gla_call(u, s_all, wgh, wgl, bg, mall, masks, st0, *, n_seq, seq_len, row0, prev=None):
    t = GLA_CHUNK
    nch = seq_len // t
    r0 = row0 // t
    hk, hv = B_HEADS * B_DK, B_HEADS * B_DV
    full = lambda a: pl.BlockSpec(a.shape, lambda b, c: (0,) * a.ndim)
    in_specs, args = [], []
    for cidx in (lambda c: c, lambda c: nch - 1 - c):
        in_specs += [pl.BlockSpec((t, 2 * hk), lambda b, c, cidx=cidx: (r0 + b * nch + cidx(c), U_QK // (2 * hk))),
                     pl.BlockSpec((t, hv), lambda b, c, cidx=cidx: (r0 + b * nch + cidx(c), U_V // hv)),
                     pl.BlockSpec((t, LANE), lambda b, c, cidx=cidx: (r0 + b * nch + cidx(c), 0))]
        args += [u, u, s_all]
    in_specs += [full(wgh), full(wgl), full(bg), full(mall), full(masks)]
    args += [wgh, wgl, bg, mall, masks]
    st_shape = (2, B_HEADS, B_DV, B_DK)
    st_spec = pl.BlockSpec((None,) + st_shape, lambda b, c: (b, 0, 0, 0, 0))
    if st0 is not None:
        in_specs.append(st_spec)
        args.append(st0)
    n_prev = 0 if prev is None else prev.shape[1]
    if n_prev:
        in_specs.append(pl.BlockSpec((None, n_prev) + st_shape, lambda b, c: (b, 0, 0, 0, 0, 0)))
        args.append(prev)
        st_shape = (n_prev + 1, 2, B_HEADS, B_DK, B_DV)
        st_spec = pl.BlockSpec((None,) + st_shape, lambda b, c: (b, 0, 0, 0, 0, 0))
    return pl.pallas_call(
        functools.partial(_gla_kernel, has_init=st0 is not None, n_prev=n_prev),
        grid=(n_seq, nch),
        in_specs=in_specs,
        out_specs=[pl.BlockSpec((t, hv), lambda b, c: (b * nch + c, 0)),
                   pl.BlockSpec((t, hv), lambda b, c: (b * nch + nch - 1 - c, 0)),
                   st_spec],
        out_shape=[jax.ShapeDtypeStruct((n_seq * seq_len, hv), BF16),
                   jax.ShapeDtypeStruct((n_seq * seq_len, hv), BF16),
                   jax.ShapeDtypeStruct((n_seq,) + st_shape, F32)],
        scratch_shapes=[pltpu.VMEM((2, B_HEADS, B_DV, B_DK), F32)],
        compiler_params=pltpu.CompilerParams(dimension_semantics=("parallel", "arbitrary")),
        name="gla_scan",
    )(*args)


def _merge_kernel(x_ref, mod_ref, oa_ref, gf_ref, gb_ref, og_ref, yf_ref, yb_ref, xc_ref, z_ref, gt_ref,
                  awo_ref, onw_ref, bwo_ref, dexp_ref, cnw_ref, cwo_ref, wout_ref, o_ref):
    o_a = _dot(oa_ref[...], awo_ref[...])

    og = gf_ref[...].astype(F32) + gb_ref[...].astype(F32)
    onw = onw_ref[...]
    parts = []
    for h in range(B_HEADS):
        th = og[:, h * B_DV:(h + 1) * B_DV]
        parts.append(th * lax.rsqrt(jnp.mean(th * th, axis=-1, keepdims=True) + EPS) * onw)
    ob_in = jnp.concatenate(parts, axis=-1) * _silu(og_ref[...].astype(F32))
    o_b = _dot(ob_in.astype(BF16), bwo_ref[...])

    y = yf_ref[...].astype(F32) + yb_ref[...].astype(F32) + dexp_ref[...] * xc_ref[...].astype(F32)
    y = y * _silu(z_ref[...].astype(F32))
    y = y * lax.rsqrt(jnp.mean(y * y, axis=-1, keepdims=True) + EPS) * cnw_ref[...]
    o_c = _dot(y.astype(BF16), cwo_ref[...])

    d = D_MODEL
    merged = (jax.nn.sigmoid(gt_ref[:, 0:d].astype(F32)) * o_a
              + jax.nn.sigmoid(gt_ref[:, d:2 * d].astype(F32)) * o_b
              + jax.nn.sigmoid(gt_ref[:, 2 * d:3 * d].astype(F32)) * o_c)
    out = _dot(merged.astype(BF16), wout_ref[...])
    o_ref[...] = x_ref[...] + mod_ref[:, 2 * d:3 * d] * out


def _merge_call(x, mod3, u, o_attn, g_f, g_b, y_f, y_b, xbc, wts, *, n_rows, row0, layer):
    tm = 512
    r0 = row0 // tm
    d = D_MODEL
    loc = lambda w: pl.BlockSpec((tm, w), lambda i: (i, 0))
    full = lambda a: pl.BlockSpec(a.shape, lambda i: (0,) * a.ndim)
    in_specs = [pl.BlockSpec((tm, d), lambda i: (i + r0, 0)),
                pl.BlockSpec((None, 1, 6 * d), lambda i: (_mod_row(i + r0, tm, layer), 0, 0)),
                loc(A_HEADS * A_V), loc(d), loc(d),
                pl.BlockSpec((tm, d), lambda i: (i + r0, U_OG // d)),
                loc(d), loc(d),
                pl.BlockSpec((tm, d), lambda i: (i, 0)),
                pl.BlockSpec((tm, d), lambda i: (i + r0, U_Z // d)),
                pl.BlockSpec((tm, 3 * d), lambda i: (i + r0, U_GATES // (3 * d)))]
    w_args = [wts[n] for n in ('awo', 'onw', 'bwo', 'dexp', 'cnw', 'cwo', 'wout')]
    in_specs += [full(a) for a in w_args]
    return pl.pallas_call(
        _merge_kernel,
        grid=(n_rows // tm,),
        in_specs=in_specs,
        out_specs=pl.BlockSpec((tm, d), lambda i: (i + r0, 0)),
        out_shape=jax.ShapeDtypeStruct((N_TOK, d), F32),
        input_output_aliases={0: 0},
        compiler_params=pltpu.CompilerParams(dimension_semantics=("parallel",)),
        name="mixer_merge",
    )(x, mod3, o_attn, g_f, g_b, u, y_f, y_b, xbc, u, u, *w_args)


MOE_TILE = 512
MOE_TOKEN_TILE = 512
MOE_MAX_TILES = 2 * N_TOK // MOE_TILE + N_EXPERTS
MOE_ROWS = MOE_MAX_TILES * MOE_TILE
R_E1, R_E2, R_W1, R_W2, R_P1, R_P2 = 0, 1, 2, 3, 4, 5


def _first_index(vals, target):
    idx = jnp.full(target.shape, len(vals) - 1, jnp.int32)
    for k in reversed(range(len(vals) - 1)):
        idx = jnp.where(vals[k] == target, k, idx)
    return idx


def _top2_of(vals):
    m1 = functools.reduce(jnp.maximum, vals)
    i1 = _first_index(vals, m1)
    rest = [jnp.where(i1 == k, -jnp.inf, v) for k, v in enumerate(vals)]
    m2 = functools.reduce(jnp.maximum, rest)
    return m1, i1, m2, _first_index(rest, m2)


def _pick_by(vals, idx):
    out = vals[-1]
    for k in reversed(range(len(vals) - 1)):
        out = jnp.where(idx == k, vals[k], out)
    return out


def _route_kernel(x_ref, mod_ref, nw_ref, rwh_ref, rwl_ref, rb_ref, tri_ref, h_ref, route_ref, wrec_ref, cnt_ref,
                  base_scr):
    d = D_MODEL
    ng = N_EXPERT_GROUPS
    per_group = N_EXPERTS // ng

    @pl.when(pl.program_id(0) == 0)
    def _():
        base_scr[...] = jnp.zeros(base_scr.shape, F32)

    x = x_ref[...]
    tm = x.shape[0]
    y = x * lax.rsqrt(jnp.mean(x * x, axis=-1, keepdims=True) + EPS) * nw_ref[...]
    h = y * (1.0 + mod_ref[:, 4 * d:5 * d]) + mod_ref[:, 3 * d:4 * d]
    h_ref[...] = h
    h_hi, h_lo = _split2(h)
    rwh = rwh_ref[...]
    logits = _dot_nt(rwh, h_hi) + _dot_nt(rwh, h_lo) + _dot_nt(rwl_ref[...], h_hi)
    scores = jax.nn.sigmoid(logits)
    sel = scores + rb_ref[...]
    sel_k = [sel[k * ng:(k + 1) * ng] for k in range(per_group)]
    sc_k = [scores[k * ng:(k + 1) * ng] for k in range(per_group)]
    m1, i1, m2, i2 = _top2_of(sel_k)
    gsum = m1 + m2
    rows = lambda a: [a[g:g + 1] for g in range(ng)]
    gs = rows(gsum)
    best_v, best_g = gs[0], jnp.zeros((1, tm), jnp.int32)
    for g in range(1, ng):
        upd = gs[g] > best_v
        best_g = jnp.where(upd, g, best_g)
        best_v = jnp.where(upd, gs[g], best_v)
    k1 = _pick_by(rows(i1), best_g)
    k2 = _pick_by(rows(i2), best_g)
    s1 = _pick_by(rows(_pick_by(sc_k, i1)), best_g)
    s2 = _pick_by(rows(_pick_by(sc_k, i2)), best_g)
    tot = s1 + s2
    in_best = lax.broadcasted_iota(jnp.int32, (ng, tm), 0) == best_g
    hit = jnp.concatenate(
        [jnp.where(jnp.logical_and(in_best, jnp.logical_or(k1 == k, k2 == k)), 1.0, 0.0) for k in range(per_group)],
        axis=0)
    rank = _dot(hit.astype(BF16), tri_ref[...]) + base_scr[...]
    rank_k = [rank[k * ng:(k + 1) * ng] for k in range(per_group)]
    p1 = _pick_by(rows(_pick_by(rank_k, k1)), best_g)
    p2 = _pick_by(rows(_pick_by(rank_k, k2)), best_g)
    base_scr[...] += jnp.sum(hit, axis=1, keepdims=True)
    e1 = (best_g * per_group + k1).astype(F32)
    e2 = (best_g * per_group + k2).astype(F32)
    rec = jnp.concatenate([e1, e2, s1 / tot, s2 / tot, p1, p2, jnp.zeros((LANE - 6, tm), F32)], axis=0)
    route_ref[...] = rec[0:8]
    wrec_ref[...] = rec.T
    cnt_ref[...] = jnp.broadcast_to(base_scr[...], cnt_ref.shape)


def _route_call(x, mod3, nw, rwh, rwl, rb, tri, layer):
    tm = tri.shape[0]
    d = D_MODEL
    full = lambda a: pl.BlockSpec(a.shape, lambda i: (0,) * a.ndim)
    return pl.pallas_call(
        _route_kernel,
        grid=(N_TOK // tm,),
        in_specs=[pl.BlockSpec((tm, d), lambda i: (i, 0)),
                  pl.BlockSpec((None, 1, 6 * d), lambda i: (_mod_row(i, tm, layer), 0, 0)),
                  full(nw), full(rwh), full(rwl), full(rb), full(tri)],
        out_specs=[pl.BlockSpec((tm, d), lambda i: (i, 0)),
                   pl.BlockSpec((8, tm), lambda i: (0, i)),
                   pl.BlockSpec((tm, LANE), lambda i: (i, 0)),
                   pl.BlockSpec((N_EXPERTS, LANE), lambda i: (0, 0))],
        out_shape=[jax.ShapeDtypeStruct((N_TOK, d), F32),
                   jax.ShapeDtypeStruct((8, N_TOK), F32),
                   jax.ShapeDtypeStruct((N_TOK, LANE), F32),
                   jax.ShapeDtypeStruct((N_EXPERTS, LANE), F32)],
        scratch_shapes=[pltpu.VMEM((N_EXPERTS, 1), F32)],
        compiler_params=pltpu.CompilerParams(dimension_semantics=("arbitrary",)),
        name="moe_route",
    )(x, mod3, nw, rwh, rwl, rb, tri)


def _row_copy(src, src_row, dst, dst_row, sem):
    return pltpu.make_async_copy(src.at[pl.ds(src_row, 1), :], dst.at[pl.ds(dst_row, 1), :], sem)


def _dispatch_kernel(dst_ref, ztile_ref, h_ref, xs_ref, zbuf, sem):
    tm = h_ref.shape[0]

    @pl.when(pl.program_id(0) == 0)
    def _():
        zbuf[...] = jnp.zeros(zbuf.shape, F32)
        fills = []
        for e in range(N_EXPERTS):
            row = pl.multiple_of(jnp.maximum(ztile_ref[0, e], 0), MOE_TILE)
            fills.append((ztile_ref[0, e] >= 0, pltpu.make_async_copy(zbuf, xs_ref.at[pl.ds(row, MOE_TILE), :], sem)))
        for has_tile, cp in fills:
            pl.when(has_tile)(cp.start)
        for has_tile, cp in fills:
            pl.when(has_tile)(cp.wait)

        def fill_unused(t, carry):
            cp = pltpu.make_async_copy(zbuf, xs_ref.at[pl.ds(pl.multiple_of(t * MOE_TILE, MOE_TILE), MOE_TILE), :], sem)
            cp.start()
            cp.wait()
            return carry

        lax.fori_loop(ztile_ref[0, N_EXPERTS], MOE_MAX_TILES, fill_unused, 0)

    def issue(j, carry):
        for i in range(2):
            r = 2 * j + i
            _row_copy(h_ref, r, xs_ref, dst_ref[0, r], sem).start(priority=i)
            _row_copy(h_ref, r, xs_ref, dst_ref[0, tm + r], sem).start(priority=1 - i)
        return carry

    def drain(r, carry):
        _row_copy(h_ref, 0, xs_ref, 0, sem).wait()
        _row_copy(h_ref, 0, xs_ref, 0, sem).wait()
        return carry

    lax.fori_loop(0, tm // 2, issue, 0, unroll=4)
    lax.fori_loop(0, tm, drain, 0, unroll=8)


def _dispatch_call(dst, ztile, h):
    tm = dst.shape[-1] // 2
    d = D_MODEL
    return pl.pallas_call(
        _dispatch_kernel,
        grid=(N_TOK // tm,),
        in_specs=[pl.BlockSpec((None, 1, 2 * tm), lambda i: (i, 0, 0), memory_space=pltpu.SMEM),
                  pl.BlockSpec((1, N_EXPERTS + 1), lambda i: (0, 0), memory_space=pltpu.SMEM),
                  pl.BlockSpec((tm, d), lambda i: (i, 0))],
        out_specs=pl.BlockSpec(memory_space=pl.ANY),
        out_shape=jax.ShapeDtypeStruct((MOE_ROWS, d), F32),
        scratch_shapes=[pltpu.VMEM((MOE_TILE, d), F32), pltpu.SemaphoreType.DMA(())],
        compiler_params=pltpu.CompilerParams(dimension_semantics=("arbitrary",)),
        name="moe_dispatch",
    )(dst, ztile, h)


def _expert_kernel(te_ref, nt_ref, xs_ref, w1_ref, w3_ref, w2_ref, y_ref, w1b, w3b, w2b):
    t = pl.program_id(0)
    changed = jnp.logical_or(t == 0, te_ref[t] != te_ref[jnp.maximum(t - 1, 0)])

    @pl.when(changed)
    def _():
        w1b[...] = w1_ref[...].astype(BF16)
        w3b[...] = w3_ref[...].astype(BF16)
        w2b[...] = w2_ref[...].astype(BF16)

    @pl.when(t < nt_ref[0])
    def _():
        xb = xs_ref[...].astype(BF16)
        hid = _silu(_dot(xb, w1b[...])) * _dot(xb, w3b[...])
        y_ref[...] = _dot(hid.astype(BF16), w2b[...])

    @pl.when(t >= nt_ref[0])
    def _():
        y_ref[...] = jnp.zeros(y_ref.shape, F32)


def _expert_call(tile_expert, n_tiles, xs, w1, w3, w2, layer):
    d = D_MODEL
    grid_spec = pltpu.PrefetchScalarGridSpec(
        num_scalar_prefetch=2,
        grid=(MOE_MAX_TILES,),
        in_specs=[pl.BlockSpec((MOE_TILE, d), lambda t, te, nt: (jnp.minimum(t, nt[0] - 1), 0)),
                  pl.BlockSpec((None, None, d, D_EXPERT), lambda t, te, nt: (layer, te[t], 0, 0)),
                  pl.BlockSpec((None, None, d, D_EXPERT), lambda t, te, nt: (layer, te[t], 0, 0)),
                  pl.BlockSpec((None, None, D_EXPERT, d), lambda t, te, nt: (layer, te[t], 0, 0))],
        out_specs=pl.BlockSpec((MOE_TILE, d), lambda t, te, nt: (t, 0)),
        scratch_shapes=[pltpu.VMEM((d, D_EXPERT), BF16), pltpu.VMEM((d, D_EXPERT), BF16),
                        pltpu.VMEM((D_EXPERT, d), BF16)])
    return pl.pallas_call(
        _expert_kernel,
        grid_spec=grid_spec,
        out_shape=jax.ShapeDtypeStruct((MOE_ROWS, d), F32),
        compiler_params=pltpu.CompilerParams(dimension_semantics=("arbitrary",)),
        name="moe_experts",
    )(tile_expert, n_tiles, xs, w1, w3, w2)


def _combine_kernel(dst_ref, x_ref, mod_ref, route_ref, y_ref, o_ref, b1, b2, sem):
    tm = x_ref.shape[0]
    d = D_MODEL

    def issue(j, carry):
        for i in range(2):
            r = 2 * j + i
            _row_copy(y_ref, dst_ref[0, r], b1, r, sem).start(priority=i)
            _row_copy(y_ref, dst_ref[0, tm + r], b2, r, sem).start(priority=1 - i)
        return carry

    def drain(r, carry):
        _row_copy(y_ref, 0, b1, 0, sem).wait()
        _row_copy(y_ref, 0, b2, 0, sem).wait()
        return carry

    lax.fori_loop(0, tm // 2, issue, 0, unroll=4)
    lax.fori_loop(0, tm, drain, 0, unroll=8)
    rec = route_ref[...]
    moe = rec[:, R_W1:R_W1 + 1] * b1[...] + rec[:, R_W2:R_W2 + 1] * b2[...]
    o_ref[...] = x_ref[...] + mod_ref[:, 5 * d:6 * d] * moe


def _combine_call(dst, x, mod3, route, y, layer):
    tm = dst.shape[-1] // 2
    d = D_MODEL
    return pl.pallas_call(
        _combine_kernel,
        grid=(N_TOK // tm,),
        in_specs=[pl.BlockSpec((None, 1, 2 * tm), lambda i: (i, 0, 0), memory_space=pltpu.SMEM),
                  pl.BlockSpec((tm, d), lambda i: (i, 0)),
                  pl.BlockSpec((None, 1, 6 * d), lambda i: (_mod_row(i, tm, layer), 0, 0)),
                  pl.BlockSpec((tm, LANE), lambda i: (i, 0)),
                  pl.BlockSpec(memory_space=pl.ANY)],
        out_specs=pl.BlockSpec((tm, d), lambda i: (i, 0)),
        out_shape=jax.ShapeDtypeStruct((N_TOK, d), F32),
        scratch_shapes=[pltpu.VMEM((tm, d), F32), pltpu.VMEM((tm, d), F32), pltpu.SemaphoreType.DMA(())],
        compiler_params=pltpu.CompilerParams(dimension_semantics=("arbitrary",)),
        name="moe_combine",
    )(dst, x, mod3, route, y)


def _moe_routed(x, mod3, nw, rwh, rwl, rb, tri, w1, w3, w2, layer):
    h, route, wrec, cnt = _route_call(x, mod3, nw, rwh, rwl, rb, tri, layer)
    per_group = N_EXPERTS // N_EXPERT_GROUPS
    counts = cnt[:, 0].reshape(per_group, N_EXPERT_GROUPS).T.reshape(N_EXPERTS).astype(jnp.int32)
    ntile = (counts + MOE_TILE - 1) // MOE_TILE
    tend = jnp.cumsum(ntile)
    row0 = (tend - ntile) * MOE_TILE
    n_tiles = tend[-1:]
    tids = jnp.arange(MOE_MAX_TILES)
    tile_expert = jnp.minimum(jnp.sum(tend[None, :] <= jnp.minimum(tids, n_tiles[0] - 1)[:, None], axis=1),
                              N_EXPERTS - 1).astype(jnp.int32)
    ztile = jnp.concatenate([jnp.where(ntile > 0, row0 + (ntile - 1) * MOE_TILE, -1), n_tiles])
    ztile = ztile.astype(jnp.int32).reshape(1, N_EXPERTS + 1)
    eids = jnp.arange(N_EXPERTS, dtype=F32)
    row0f = row0.astype(F32)

    def dest(e, p):
        return (jnp.sum(jnp.where(e[:, None] == eids[None, :], row0f[None, :], 0.0), axis=1) + p).astype(jnp.int32)

    tm = MOE_TOKEN_TILE
    dst = jnp.concatenate([dest(route[R_E1], route[R_P1]).reshape(N_TOK // tm, 1, tm),
                           dest(route[R_E2], route[R_P2]).reshape(N_TOK // tm, 1, tm)], axis=-1)
    xs = _dispatch_call(dst, ztile, h)
    y = _expert_call(tile_expert, n_tiles.astype(jnp.int32), xs, w1, w3, w2, layer)
    return _combine_call(dst, x, mod3, wrec, y, layer)


def _pack_w_in(w):
    p = _col_slices(w)
    z = lambda n: jnp.zeros((D_MODEL, n), w.dtype)
    main = jnp.concatenate([
        p['a_cq'], p['a_ckv'], _head_slot(jnp.concatenate([z(A_NOPE), p['a_krope']], axis=1)), z(U_QK - U_KR - HEAD_PAD),
        p['b_q'], p['b_k'], p['b_v'], p['b_og'], p['c_z'], p['c_xbc'][:, :C_INNER], p['gates'],
        p['c_xbc'][:, C_INNER:]], axis=1)
    small = jnp.concatenate([p['b_gf'], p['b_gb'], p['c_dtf'], p['c_dtb'], p['a_krope'],
                             z(LANE - S_KR - A_ROPE)], axis=1)
    return main.astype(BF16), small.astype(BF16)


def _pair_slots(v):
    z = jnp.zeros_like(v)
    odd = (np.arange(A_HEADS) % 2 == 1)[None, :, None]
    return jnp.concatenate([jnp.where(odd, z, v), jnp.where(odd, v, z)], axis=-1).reshape(v.shape[0], A_HEADS * HEAD_PAD)


def _head_slot(t):
    t = t[..., _SLOT_SRC]
    return jnp.concatenate([t, jnp.zeros(t.shape[:-1] + (HEAD_PAD - A_QK,), t.dtype)], axis=-1)


def _qk_gain(g, scale):
    return (_head_slot(g) * scale).reshape(1, HEAD_PAD)


def _lane_slot(vals, off):
    return jnp.zeros((1, LANE), F32).at[0, off:off + vals.shape[0]].set(vals.astype(F32))


def _rope_tables():
    t = jnp.arange(DEC_SEQ)
    row = (t // GRID_W).astype(F32)
    col = (t % GRID_W).astype(F32)
    n_freq = A_ROPE // 4
    inv_freq = 1.0 / (ROPE_BASE ** (jnp.arange(n_freq, dtype=F32) / n_freq))
    ang = jnp.concatenate([row[:, None] * inv_freq, col[:, None] * inv_freq], axis=-1)
    cos, sin = jnp.cos(ang), jnp.sin(ang)
    ones = lambda n: jnp.ones((DEC_SEQ, n), F32)
    zeros = lambda n: jnp.zeros((DEC_SEQ, n), F32)
    n_hi = HEAD_PAD // 2 - A_ROPE // 2
    ctab = jnp.concatenate([cos, ones(_N_LOW), cos, ones(n_hi)], axis=-1)
    stab = jnp.concatenate([-sin, zeros(_N_LOW), sin, zeros(n_hi)], axis=-1)
    return ctab, stab


def kernel(x_prompt, x_sample, cache_ckv, cache_krope, state_gla, state_ssd, c, c_ctx, w_ada, b_ada, norm1, norm2, w_in, a_q_norm, a_wq, a_kv_norm, a_wkv, a_qk_qnorm, a_qk_knorm, a_wo, b_wg, b_bg, b_onorm, b_wo, c_conv_w, c_conv_b, c_dt_bias, c_A_log, c_D, c_norm, c_wo, w_out, router_w, router_bias, e_w1, e_w3, e_w2):
    d = D_MODEL
    x = jnp.concatenate([x_prompt.reshape(N_CTX, d), x_sample.reshape(N_DEC, d)], axis=0)

    cvecs = jnp.zeros((8, d), F32).at[0].set(c_ctx).at[1:1 + DEC_BATCH].set(c)
    mod3 = _mod_call(cvecs, w_ada, b_ada).reshape(DEPTH * 8, 1, 6 * d)

    rope_tabs = _rope_tables()
    gla_np = (_gla_consts(False), _gla_consts(True))
    gla_mall = jnp.asarray(np.stack([gla_np[0][0], gla_np[1][0]]), BF16)
    gla_masks = jnp.asarray(np.stack([gla_np[0][1], gla_np[1][1]]))
    ssd_consts = _ssd_consts()
    per_group = N_EXPERTS // N_EXPERT_GROUPS
    k_major = np.arange(N_EXPERTS).reshape(N_EXPERT_GROUPS, per_group).T.reshape(-1)
    rwh, rwl = _split2(router_w.T[k_major])
    rb = router_bias.astype(F32)[k_major].reshape(N_EXPERTS, 1)
    ridx = np.arange(1024)
    route_tri = jnp.asarray(ridx[:, None] < ridx[None, :], BF16)

    ckvs, kropes, glas, ssds = [], [], [], []
    for l in range(DEPTH):
        w_main, w_small = _pack_w_in(w_in[l])
        u, s_all = _norm_proj_call(x, mod3, norm1[l].reshape(1, d), w_main, w_small, l)
        kropes.append(s_all[:N_CTX, S_KR:S_KR + A_ROPE].reshape(BATCH, SEQ, A_ROPE))

        pw = A_HEADS * HEAD_PAD
        wq = _head_slot(a_wq[l].reshape(A_QLORA, A_HEADS, A_QK)).reshape(A_QLORA, pw)
        wkv = a_wkv[l].reshape(A_KVLORA, A_HEADS, A_NOPE + A_V)
        wkn = jnp.concatenate([wkv[..., :A_NOPE], jnp.zeros((A_KVLORA, A_HEADS, A_ROPE), wkv.dtype)], axis=-1)
        aw = dict(qnw=a_q_norm[l].reshape(1, A_QLORA), wq=wq.astype(BF16),
                  gq=_qk_gain(a_qk_qnorm[l], A_QK ** -0.5),
                  kvnw=a_kv_norm[l].reshape(1, A_KVLORA),
                  wkn=_head_slot(wkn).reshape(A_KVLORA, pw).astype(BF16),
                  wv=_pair_slots(wkv[..., A_NOPE:]).astype(BF16),
                  gk=_qk_gain(a_qk_knorm[l], 1.0))
        q_c, k_c, v_c, ckvn = _attn_prep_call(N_CTX, 0, (u,), aw, has_q=True, norm_kv=True, rope=False, emit_ckv=True)
        o_c = _attn_call(q_c, k_c, v_c, None, n_seq=BATCH, seq_len=SEQ, hps=A_HEADS, tq=SEQ)
        ckvs.append(ckvn.reshape(BATCH, SEQ, A_KVLORA))
        q_d, k_d, v_d = _attn_prep_call(N_DEC, N_CTX, (u,), aw, has_q=True, norm_kv=True, rope=True, emit_ckv=False,
                                        tables=rope_tabs)
        kr_x = _head_slot(jnp.concatenate([jnp.zeros((DEC_BATCH * PAST_LEN, A_NOPE), F32),
                                            cache_krope[:, l].reshape(DEC_BATCH * PAST_LEN, A_ROPE)], axis=1))
        k_x, v_x = _attn_prep_call(DEC_BATCH * PAST_LEN, 0, (cache_ckv[:, l].reshape(DEC_BATCH * PAST_LEN, A_KVLORA), kr_x),
                                   aw, has_q=False, norm_kv=False, rope=False, emit_ckv=False)
        o_d = _attn_call(q_d, k_d, v_d, (k_x, v_x), n_seq=DEC_BATCH, seq_len=DEC_SEQ, hps=8, tq=512)

        hk = B_HEADS * B_DK
        wg = (jnp.zeros((2, LANE, hk), F32).at[0, S_GF:S_GF + B_GATE_RANK].set(b_wg[l, 0])
              .at[1, S_GB:S_GB + B_GATE_RANK].set(b_wg[l, 1]))
        wgh, wgl = _split2(wg)
        bg = b_bg[l].reshape(2, 1, hk)
        last = l == DEPTH - 1
        g_fc, g_bc, st_gc = _gla_call(u, s_all, wgh, wgl, bg, gla_mall, gla_masks, None, n_seq=BATCH, seq_len=SEQ, row0=0,
                                      prev=jnp.stack(glas, axis=1) if last and glas else None)
        g_fd, g_bd, _ = _gla_call(u, s_all, wgh, wgl, bg, gla_mall, gla_masks, jnp.swapaxes(state_gla[:, l], -1, -2),
                                  n_seq=DEC_BATCH, seq_len=DEC_SEQ, row0=N_CTX)
        glas.append(st_gc)

        w8 = jnp.zeros((8, C_XBC), F32).at[:C_CONV].set(c_conv_w[l])
        cb = c_conv_b[l].reshape(1, C_XBC)
        xbc_c = _conv_call(u, w8, cb, n_seq=BATCH, seq_len=SEQ, row0=0)
        xbc_d = _conv_call(u, w8, cb, n_seq=DEC_BATCH, seq_len=DEC_SEQ, row0=N_CTX)
        dtb = jnp.stack([_lane_slot(c_dt_bias[l, 0], S_DTF), _lane_slot(c_dt_bias[l, 1], S_DTB)])
        a_neg = -jnp.exp(c_A_log[l].astype(F32))
        aneg = jnp.stack([_lane_slot(a_neg[0], S_DTF), _lane_slot(a_neg[1], S_DTB)])
        y_fc, y_bc, st_c = _ssd_call(xbc_c, s_all, dtb, aneg, ssd_consts, None, n_seq=BATCH, seq_len=SEQ, row0=0,
                                     prev=jnp.stack(ssds, axis=1) if last and ssds else None)
        st0 = state_ssd[:, l].reshape(DEC_BATCH, 2, C_HEADS // 2, LANE, C_STATE)
        y_fd, y_bd, _ = _ssd_call(xbc_d, s_all, dtb, aneg, ssd_consts, st0, n_seq=DEC_BATCH, seq_len=DEC_SEQ, row0=N_CTX)
        ssds.append(st_c)

        mw = dict(awo=a_wo[l].astype(BF16), onw=b_onorm[l].reshape(1, B_DV),
                  bwo=b_wo[l].astype(BF16), dexp=jnp.repeat(c_D[l], C_HEADDIM).reshape(1, C_INNER),
                  cnw=c_norm[l].reshape(1, C_INNER), cwo=c_wo[l].astype(BF16), wout=w_out[l].astype(BF16))
        x = _merge_call(x, mod3, u, o_c, g_fc, g_bc, y_fc, y_bc, xbc_c, mw, n_rows=N_CTX, row0=0, layer=l)
        x = _merge_call(x, mod3, u, o_d, g_fd, g_bd, y_fd, y_bd, xbc_d, mw, n_rows=N_DEC, row0=N_CTX, layer=l)

        x = _moe_routed(x, mod3, norm2[l].reshape(1, d), rwh, rwl, rb, route_tri, e_w1, e_w3, e_w2, l)

    y_prompt = x[:N_CTX].reshape(BATCH, SEQ, d)
    y_sample = x[N_CTX:].reshape(DEC_BATCH, DEC_SEQ, d)
    new_ckv = jnp.stack(ckvs, axis=1)
    new_krope = jnp.stack(kropes, axis=1)
    new_state_gla = glas[-1] if DEPTH > 1 else jnp.swapaxes(glas[-1], -1, -2)[:, None]
    new_state_ssd = (ssds[-1] if DEPTH > 1 else ssds[-1][:, None]).reshape(BATCH, DEPTH, 2, C_HEADS, C_HEADDIM, C_STATE)
    return (y_prompt, y_sample, new_ckv, new_krope, new_state_gla, new_state_ssd)
```

```python
import functools
import math

import numpy as np
import jax
import jax.numpy as jnp
from jax import lax
from jax.experimental import pallas as pl
from jax.experimental.pallas import tpu as pltpu

F32 = jnp.float32
BF16 = jnp.bfloat16

D_MODEL = 1024
BATCH = 16
SEQ = 256
DEPTH = 2
DEC_BATCH = 2
DEC_SEQ = 2048
PAST_LEN = 512
GRID_W = 64
EPS = 1e-6
ROPE_BASE = 10000.0

A_HEADS = 16
A_NOPE = 64
A_ROPE = 32
A_QK = A_NOPE + A_ROPE
A_V = 64
A_QLORA = 512
A_KVLORA = 256

B_HEADS = 4
B_DK = 128
B_DV = 256
B_GATE_RANK = 16
B_GATE_NORM = 16.0

C_HEADS = 16
C_HEADDIM = 64
C_INNER = C_HEADS * C_HEADDIM
C_GROUPS = 2
C_STATE = 128
C_XBC = C_INNER + 2 * C_GROUPS * C_STATE
C_CONV = 5

N_EXPERTS = 16
N_EXPERT_GROUPS = 4
D_EXPERT = 512

N_CTX = BATCH * SEQ
N_DEC = DEC_BATCH * DEC_SEQ
N_TOK = N_CTX + N_DEC

LANE = 128
HEAD_PAD = 128
CHUNK = 128
GLA_CHUNK = 256

U_CQ = 0
U_CKV = 512
U_KR = 768
U_QK = 1024
U_V = 2048
U_OG = 3072
U_Z = 4096
U_X = 5120
U_GATES = 6144
U_BC = 9216
U_WIDTH = 9728
S_GF, S_GB, S_DTF, S_DTB, S_KR = 0, 16, 32, 48, 64

_IN_SPLITS = (
    ('a_cq', A_QLORA), ('a_ckv', A_KVLORA), ('a_krope', A_ROPE),
    ('b_q', B_HEADS * B_DK), ('b_k', B_HEADS * B_DK), ('b_v', B_HEADS * B_DV),
    ('b_og', B_HEADS * B_DV), ('b_gf', B_GATE_RANK), ('b_gb', B_GATE_RANK),
    ('c_z', C_INNER), ('c_xbc', C_XBC), ('c_dtf', C_HEADS), ('c_dtb', C_HEADS),
    ('gates', 3 * D_MODEL),
)
_N_LOW = HEAD_PAD // 2 - A_ROPE // 2
_SLOT_SRC = np.concatenate([A_NOPE + np.arange(0, A_ROPE, 2), np.arange(0, _N_LOW),
                            A_NOPE + np.arange(1, A_ROPE, 2), np.arange(_N_LOW, A_NOPE)])


def _col_slices(w):
    parts, start = {}, 0
    for name, size in _IN_SPLITS:
        parts[name] = w[:, start:start + size]
        start += size
    return parts


def _dot(a, b):
    return jnp.dot(a, b, preferred_element_type=F32)


def _dot_nt(a, b):
    return lax.dot_general(a, b, (((1,), (1,)), ((), ())), preferred_element_type=F32)


def _dot_tn(a, b):
    return lax.dot_general(a, b, (((0,), (0,)), ((), ())), preferred_element_type=F32)


def _split2(x):
    hi = x.astype(BF16)
    lo = (x - hi.astype(F32)).astype(BF16)
    return hi, lo


def _silu(x):
    return x * jax.nn.sigmoid(x)


def _softplus(x):
    return jnp.maximum(x, 0.0) + jnp.log1p(jnp.exp(-jnp.abs(x)))


def _log_sigmoid(x):
    return jnp.minimum(x, 0.0) - jnp.log1p(jnp.exp(-jnp.abs(x)))


def _mod_kernel(c_ref, w_ref, b_ref, o_ref):
    @pl.when(pl.program_id(1) == 0)
    def _():
        o_ref[...] = jnp.broadcast_to(b_ref[...], o_ref.shape)

    s = _silu(c_ref[...]).astype(BF16)
    o_ref[...] += _dot(s, w_ref[...].astype(BF16))


def _mod_call(cvecs, w_ada, b_ada):
    tk = 256
    return pl.pallas_call(
        _mod_kernel,
        grid=(DEPTH, D_MODEL // tk),
        in_specs=[pl.BlockSpec((8, tk), lambda l, k: (0, k)),
                  pl.BlockSpec((None, tk, 6 * D_MODEL), lambda l, k: (l, k, 0)),
                  pl.BlockSpec((None, 1, 6 * D_MODEL), lambda l, k: (l, 0, 0))],
        out_specs=pl.BlockSpec((None, 8, 6 * D_MODEL), lambda l, k: (l, 0, 0)),
        out_shape=jax.ShapeDtypeStruct((DEPTH, 8, 6 * D_MODEL), F32),
        compiler_params=pltpu.CompilerParams(dimension_semantics=("parallel", "arbitrary")),
        name="adaln_mod",
    )(cvecs, w_ada, b_ada.reshape(DEPTH, 1, 6 * D_MODEL))


def _mod_row(i, tm, layer):
    n_ctx = N_CTX // tm
    per_b = DEC_SEQ // tm
    return layer * 8 + jnp.where(i < n_ctx, 0, 1 + (i - n_ctx) // per_b)


def _norm_proj_kernel(x_ref, mod_ref, nw_ref, w_ref, ws_ref, u_ref, s_ref, h_scr):
    @pl.when(pl.program_id(1) == 0)
    def _():
        x = x_ref[...]
        m = mod_ref[...]
        y = x * lax.rsqrt(jnp.mean(x * x, axis=-1, keepdims=True) + EPS) * nw_ref[...]
        h = (y * (1.0 + m[:, D_MODEL:2 * D_MODEL]) + m[:, 0:D_MODEL]).astype(BF16)
        h_scr[...] = h
        s_ref[...] = _dot(h, ws_ref[...])

    u_ref[...] = _dot(h_scr[...], w_ref[...]).astype(BF16)


def _norm_proj_call(x, mod3, nw, w_main, w_small, layer):
    tm, tn = 2048, 512
    return pl.pallas_call(
        _norm_proj_kernel,
        grid=(N_TOK // tm, U_WIDTH // tn),
        in_specs=[pl.BlockSpec((tm, D_MODEL), lambda i, j: (i, 0)),
                  pl.BlockSpec((None, 1, 6 * D_MODEL), lambda i, j: (_mod_row(i, tm, layer), 0, 0)),
                  pl.BlockSpec((1, D_MODEL), lambda i, j: (0, 0)),
                  pl.BlockSpec((D_MODEL, tn), lambda i, j: (0, j)),
                  pl.BlockSpec((D_MODEL, LANE), lambda i, j: (0, 0))],
        out_specs=[pl.BlockSpec((tm, tn), lambda i, j: (i, j)),
                   pl.BlockSpec((tm, LANE), lambda i, j: (i, 0))],
        out_shape=[jax.ShapeDtypeStruct((N_TOK, U_WIDTH), BF16),
                   jax.ShapeDtypeStruct((N_TOK, LANE), F32)],
        scratch_shapes=[pltpu.VMEM((tm, D_MODEL), BF16)],
        compiler_params=pltpu.CompilerParams(dimension_semantics=("parallel", "arbitrary")),
        name="norm_in_proj",
    )(x, mod3, nw, w_main, w_small)


def _rope(t, cos, sin):
    return t * cos + pltpu.roll(t, HEAD_PAD // 2, 1) * sin


def _head_scale(ss):
    return lax.rsqrt(ss * (1.0 / A_QK) + EPS)


def _attn_prep_kernel(*refs, has_q, norm_kv, rope, emit_ckv):
    refs = list(refs)
    cq_ref = refs.pop(0) if has_q else None
    ckv_ref, kr_ref = refs.pop(0), refs.pop(0)
    if has_q:
        qnw_ref, wq_ref, gq_ref = refs.pop(0), refs.pop(0), refs.pop(0)
    kvnw_ref, wkn_ref, wv_ref, gk_ref = refs.pop(0), refs.pop(0), refs.pop(0), refs.pop(0)
    cos = sin = None
    if rope:
        cos, sin = refs.pop(0)[...], refs.pop(0)[...]
    q_ref = refs.pop(0) if has_q else None
    k_ref, v_ref = refs.pop(0), refs.pop(0)
    ckvn_ref = refs.pop(0) if emit_ckv else None

    ckv = ckv_ref[...].astype(F32)
    if norm_kv:
        ckv = ckv * lax.rsqrt(jnp.mean(ckv * ckv, axis=-1, keepdims=True) + EPS) * kvnw_ref[...]
    if emit_ckv:
        ckvn_ref[...] = ckv
    ckv_b = ckv.astype(BF16)
    v_ref[...] = _dot(ckv_b, wv_ref[...]).astype(BF16)
    kf = _dot(ckv_b, wkn_ref[...])
    kr = kr_ref[...].astype(F32)
    gk = gk_ref[...]
    ss_kr = jnp.sum(kr * kr, axis=-1, keepdims=True)
    krg = kr * gk
    if rope:
        krg = _rope(krg, cos, sin)
    for h in range(A_HEADS):
        sl = slice(h * HEAD_PAD, (h + 1) * HEAD_PAD)
        kn = kf[:, sl]
        scale = _head_scale(jnp.sum(kn * kn, axis=-1, keepdims=True) + ss_kr)
        k_ref[:, sl] = ((kn * gk + krg) * scale).astype(BF16)

    if has_q:
        cq = cq_ref[...].astype(F32)
        cq = cq * lax.rsqrt(jnp.mean(cq * cq, axis=-1, keepdims=True) + EPS) * qnw_ref[...]
        qf = _dot(cq.astype(BF16), wq_ref[...])
        gq = gq_ref[...]
        for h in range(A_HEADS):
            sl = slice(h * HEAD_PAD, (h + 1) * HEAD_PAD)
            t = qf[:, sl]
            t = t * _head_scale(jnp.sum(t * t, axis=-1, keepdims=True)) * gq
            q_ref[:, sl] = (_rope(t, cos, sin) if rope else t).astype(BF16)


def _attn_prep_call(n_rows, row0, srcs, wts, *, has_q, norm_kv, rope, emit_ckv, tables=None):
    tm = 256
    r0 = row0 // tm
    pw = A_HEADS * HEAD_PAD
    full = lambda shape: pl.BlockSpec(shape, lambda i: (0,) * len(shape))
    args, in_specs = [], []
    if len(srcs) == 1:
        u = srcs[0]
        if has_q:
            args.append(u)
            in_specs.append(pl.BlockSpec((tm, A_QLORA), lambda i: (i + r0, U_CQ // A_QLORA)))
        args += [u, u]
        in_specs += [pl.BlockSpec((tm, A_KVLORA), lambda i: (i + r0, U_CKV // A_KVLORA)),
                     pl.BlockSpec((tm, LANE), lambda i: (i + r0, U_KR // LANE))]
    else:
        args += list(srcs)
        in_specs += [pl.BlockSpec((tm, A_KVLORA), lambda i: (i, 0)),
                     pl.BlockSpec((tm, LANE), lambda i: (i, 0))]
    if has_q:
        args += [wts['qnw'], wts['wq'], wts['gq']]
        in_specs += [full((1, A_QLORA)), full((A_QLORA, pw)), full((1, HEAD_PAD))]
    args += [wts['kvnw'], wts['wkn'], wts['wv'], wts['gk']]
    in_specs += [full((1, A_KVLORA)), full((A_KVLORA, pw)), full((A_KVLORA, pw)), full((1, HEAD_PAD))]
    if rope:
        per_seq = DEC_SEQ // tm
        args += list(tables)
        in_specs += [pl.BlockSpec((tm, HEAD_PAD), lambda i: (i % per_seq, 0))] * 2
    out_specs, out_shape = [], []
    for _ in range((1 if has_q else 0) + 2):
        out_specs.append(pl.BlockSpec((tm, pw), lambda i: (i, 0)))
        out_shape.append(jax.ShapeDtypeStruct((n_rows, pw), BF16))
    if emit_ckv:
        out_specs.append(pl.BlockSpec((tm, A_KVLORA), lambda i: (i, 0)))
        out_shape.append(jax.ShapeDtypeStruct((n_rows, A_KVLORA), F32))
    return pl.pallas_call(
        functools.partial(_attn_prep_kernel, has_q=has_q, norm_kv=norm_kv, rope=rope, emit_ckv=emit_ckv),
        grid=(n_rows // tm,),
        in_specs=in_specs, out_specs=out_specs, out_shape=out_shape,
        compiler_params=pltpu.CompilerParams(dimension_semantics=("parallel",)),
        name="mla_prep",
    )(*args)


def _attn_kernel(*refs, hps, has_ctx):
    if has_ctx:
        q_ref, k_ref, v_ref, kc_ref, vc_ref, o_ref = refs
    else:
        q_ref, k_ref, v_ref, o_ref = refs

    def scores(hh):
        sl = slice(hh * HEAD_PAD, (hh + 1) * HEAD_PAD)
        q = q_ref[:, sl]
        return _dot_nt(q, k_ref[:, sl]), (_dot_nt(q, kc_ref[:, sl]) if has_ctx else None)

    nxt = scores(0)
    even = None
    for hh in range(hps):
        sl = slice(hh * HEAD_PAD, (hh + 1) * HEAD_PAD)
        s, s2 = nxt
        if hh + 1 < hps:
            nxt = scores(hh + 1)
        m = jnp.max(s, axis=-1, keepdims=True)
        if has_ctx:
            m = jnp.maximum(m, jnp.max(s2, axis=-1, keepdims=True))
        p = jnp.exp(s - m)
        den = jnp.sum(p, axis=-1, keepdims=True)
        o = _dot(p.astype(BF16), v_ref[:, sl])
        if has_ctx:
            p2 = jnp.exp(s2 - m)
            den = den + jnp.sum(p2, axis=-1, keepdims=True)
            o = o + _dot(p2.astype(BF16), vc_ref[:, sl])
        o = o / den
        if hh % 2 == 0:
            even = o
        else:
            pair = hh // 2
            o_ref[:, pair * LANE:(pair + 1) * LANE] = (even + o).astype(BF16)


def _attn_call(q, k, v, ctx_kv, *, n_seq, seq_len, hps, tq):
    pw = A_HEADS * HEAD_PAD
    bw = hps * HEAD_PAD
    nq = seq_len // tq
    in_specs = [pl.BlockSpec((tq, bw), lambda b, h, i: (b * nq + i, h)),
                pl.BlockSpec((seq_len, bw), lambda b, h, i: (b, h)),
                pl.BlockSpec((seq_len, bw), lambda b, h, i: (b, h))]
    args = [q, k, v]
    if ctx_kv is not None:
        in_specs += [pl.BlockSpec((PAST_LEN, bw), lambda b, h, i: (b, h))] * 2
        args += list(ctx_kv)
    return pl.pallas_call(
        functools.partial(_attn_kernel, hps=hps, has_ctx=ctx_kv is not None),
        grid=(n_seq, A_HEADS // hps, nq),
        in_specs=in_specs,
        out_specs=pl.BlockSpec((tq, hps * A_V), lambda b, h, i: (b * nq + i, h)),
        out_shape=jax.ShapeDtypeStruct((n_seq * seq_len, A_HEADS * A_V), BF16),
        compiler_params=pltpu.CompilerParams(dimension_semantics=("parallel", "parallel", "arbitrary")),
        name="mla_attention",
    )(*args)


def _conv_kernel(x_ref, w_ref, b_ref, o_ref):
    x = x_ref[...].astype(F32)
    n = x.shape[0]
    row = lax.broadcasted_iota(jnp.int32, x.shape, 0)
    half = (C_CONV - 1) // 2
    acc = x * w_ref[half:half + 1, :] + b_ref[...]
    for d in range(-half, half + 1):
        if d == 0:
            continue
        shifted = pltpu.roll(x, (-d) % n, 0)
        valid = jnp.logical_and(row + d >= 0, row + d < n)
        acc = acc + jnp.where(valid, shifted, 0.0) * w_ref[half + d:half + d + 1, :]
    o_ref[...] = _silu(acc).astype(BF16)


def _conv_call(u, w8, b, *, n_seq, seq_len, row0):
    tc = 512
    b0 = row0 // seq_len
    nx = C_INNER // tc
    ucol = lambda j: jnp.where(j < nx, U_X // tc + j, U_BC // tc + j - nx)
    return pl.pallas_call(
        _conv_kernel,
        grid=(n_seq, C_XBC // tc),
        in_specs=[pl.BlockSpec((seq_len, tc), lambda s, j: (s + b0, ucol(j))),
                  pl.BlockSpec((8, tc), lambda s, j: (0, j)),
                  pl.BlockSpec((1, tc), lambda s, j: (0, j))],
        out_specs=pl.BlockSpec((seq_len, tc), lambda s, j: (s, j)),
        out_shape=jax.ShapeDtypeStruct((n_seq * seq_len, C_XBC), BF16),
        compiler_params=pltpu.CompilerParams(dimension_semantics=("parallel", "parallel")),
        name="ssd_conv",
    )(u, w8, b)


def _ssd_consts():
    idx = np.arange(CHUNK)
    tri = np.stack([idx[None, :] <= idx[:, None], idx[None, :] >= idx[:, None]]).astype(np.float32)
    expand = np.zeros((2, LANE, C_INNER), np.float32)
    tile = np.zeros((2, LANE, C_HEADS * LANE), np.float32)
    for di, off in enumerate((S_DTF, S_DTB)):
        for h in range(C_HEADS):
            expand[di, off + h, h * C_HEADDIM:(h + 1) * C_HEADDIM] = 1.0
            tile[di, off + h, h * LANE:(h + 1) * LANE] = 1.0
    return jnp.asarray(tri, BF16), jnp.asarray(expand, BF16), jnp.asarray(tile, BF16)


def _ssd_kernel(*refs, has_init, n_prev):
    refs = list(refs)
    xs = [(refs.pop(0), refs.pop(0), refs.pop(0)) for _ in range(2)]
    dtb_ref, an_ref, tri_ref, exp_ref, tile_ref = (refs.pop(0) for _ in range(5))
    st0_ref = refs.pop(0) if has_init else None
    prev_ref = refs.pop(0) if n_prev else None
    y_refs = (refs.pop(0), refs.pop(0))
    stf_ref, st_scr = refs

    @pl.when(pl.program_id(1) == 0)
    def _():
        st_scr[...] = st0_ref[...] if has_init else jnp.zeros(st_scr.shape, F32)

    t = CHUNK
    ri = lax.broadcasted_iota(jnp.int32, (t, t), 0)
    ci = lax.broadcasted_iota(jnp.int32, (t, t), 1)
    prow = lax.broadcasted_iota(jnp.int32, (LANE, C_STATE), 0)
    gs = C_GROUPS * C_STATE
    pairs_per_group = C_HEADS // C_GROUPS // 2
    for di in range(2):
        x_ref, bc_ref, s_ref = xs[di]
        y_ref = y_refs[di]
        off = (S_DTF, S_DTB)[di]
        dt = _softplus(s_ref[...] + dtb_ref[di])
        a = dt * an_ref[di]
        a_hi, a_lo = _split2(a)
        tri = tri_ref[di]
        acum = _dot(tri, a_hi) + _dot(tri, a_lo)
        acum_t = acum.T
        dt_t = dt.T
        edge = 0 if di else t - 1
        atot = acum[edge:edge + 1, :]
        causal = (ci >= ri) if di else (ci <= ri)
        dec_all = jnp.exp(atot)
        e_exp = _dot(jnp.exp(acum).astype(BF16), exp_ref[di])
        wj_exp = _dot((jnp.exp(atot - acum) * dt).astype(BF16), exp_ref[di])
        c_hi, c_lo = _split2(acum)
        acol = _dot(c_hi, tile_ref[di]) + _dot(c_lo, tile_ref[di])
        bc = bc_ref[...]
        cb, bmat, cmat = [], [], []
        for g in range(C_GROUPS):
            bmat.append(bc[:, g * C_STATE:(g + 1) * C_STATE])
            cmat.append(bc[:, gs + g * C_STATE:gs + (g + 1) * C_STATE])
            cb.append(_dot_nt(cmat[g], bmat[g]))
        lane = lax.broadcasted_iota(jnp.int32, (t, LANE), 1)
        low = lane < C_HEADDIM
        for m in range(C_HEADS // 2):
            g = m // pairs_per_group
            ps = slice(m * LANE, (m + 1) * LANE)
            xp = x_ref[:, ps]
            y_diag = []
            for hh in range(2):
                h = 2 * m + hh
                la = off + h
                lm = jnp.exp(jnp.where(causal, acol[:, h * LANE:(h + 1) * LANE] - acum_t[la:la + 1, :], -jnp.inf))
                w = (cb[g] * lm * dt_t[la:la + 1, :]).astype(BF16)
                y_diag.append(_dot(w, xp))
            st = st_scr[di, m]
            y_off = _dot_nt(cmat[g], st.astype(BF16)) * e_exp[:, ps]
            y_ref[:, ps] = (jnp.where(low, y_diag[0], y_diag[1]) + y_off).astype(BF16)
            xw = (xp.astype(F32) * wj_exp[:, ps]).astype(BF16)
            la = off + 2 * m
            dec = jnp.where(prow < C_HEADDIM, dec_all[:, la:la + 1], dec_all[:, la + 1:la + 2])
            st_scr[di, m] = dec * st + _dot_tn(xw, bmat[g])

    @pl.when(pl.program_id(1) == pl.num_programs(1) - 1)
    def _():
        if n_prev:
            stf_ref[0:n_prev] = prev_ref[...]
            stf_ref[n_prev] = st_scr[...]
        else:
            stf_ref[...] = st_scr[...]


def _ssd_call(xbc, s_all, dtb, aneg, consts, st0, *, n_seq, seq_len, row0, prev=None):
    t = CHUNK
    nch = seq_len // t
    r0 = row0 // t
    npair = C_HEADS // 2
    bcw = 2 * C_GROUPS * C_STATE
    tri, expand, tile = consts
    full = lambda a: pl.BlockSpec(a.shape, lambda b, c: (0,) * a.ndim)
    in_specs, args = [], []
    for cidx in (lambda c: c, lambda c: nch - 1 - c):
        in_specs += [pl.BlockSpec((t, C_INNER), lambda b, c, cidx=cidx: (b * nch + cidx(c), 0)),
                     pl.BlockSpec((t, bcw), lambda b, c, cidx=cidx: (b * nch + cidx(c), C_INNER // bcw)),
                     pl.BlockSpec((t, LANE), lambda b, c, cidx=cidx: (r0 + b * nch + cidx(c), 0))]
        args += [xbc, xbc, s_all]
    in_specs += [full(dtb), full(aneg), full(tri), full(expand), full(tile)]
    args += [dtb, aneg, tri, expand, tile]
    st_shape = (2, npair, LANE, C_STATE)
    st_spec = pl.BlockSpec((None,) + st_shape, lambda b, c: (b, 0, 0, 0, 0))
    if st0 is not None:
        in_specs.append(st_spec)
        args.append(st0)
    n_prev = 0 if prev is None else prev.shape[1]
    if n_prev:
        in_specs.append(pl.BlockSpec((None, n_prev) + st_shape, lambda b, c: (b, 0, 0, 0, 0, 0)))
        args.append(prev)
        st_shape = (n_prev + 1,) + st_shape
        st_spec = pl.BlockSpec((None,) + st_shape, lambda b, c: (b, 0, 0, 0, 0, 0))
    return pl.pallas_call(
        functools.partial(_ssd_kernel, has_init=st0 is not None, n_prev=n_prev),
        grid=(n_seq, nch),
        in_specs=in_specs,
        out_specs=[pl.BlockSpec((t, C_INNER), lambda b, c: (b * nch + c, 0)),
                   pl.BlockSpec((t, C_INNER), lambda b, c: (b * nch + nch - 1 - c, 0)),
                   st_spec],
        out_shape=[jax.ShapeDtypeStruct((n_seq * seq_len, C_INNER), BF16),
                   jax.ShapeDtypeStruct((n_seq * seq_len, C_INNER), BF16),
                   jax.ShapeDtypeStruct((n_seq,) + st_shape, F32)],
        scratch_shapes=[pltpu.VMEM((2, npair, LANE, C_STATE), F32)],
        compiler_params=pltpu.CompilerParams(dimension_semantics=("parallel", "arbitrary")),
        name="ssd_scan",
    )(*args)


def _gla_consts(bwd):
    t = GLA_CHUNK
    nlev = int(math.log2(t))
    idx = np.arange(t)
    mats = np.zeros((nlev + 2, t, t), np.float32)
    masks = np.zeros((nlev + 1, t, t), np.float32)
    masks[0] = np.eye(t)
    for lvl in range(nlev):
        s = 1 << lvl
        blk = idx // (2 * s)
        upper = (idx % (2 * s)) >= s
        last_low = blk * 2 * s + s - 1
        rowtok = ~upper if bwd else upper
        for i in range(t):
            r = last_low[i]
            if not bwd:
                if upper[i]:
                    mats[lvl, i, r + 1:i + 1] = 1.0
                else:
                    mats[lvl, i, i + 1:r + 1] = 1.0
            else:
                if upper[i]:
                    mats[lvl, i, r + 1:i] = 1.0
                else:
                    mats[lvl, i, i:r + 1] = 1.0
        masks[lvl + 1] = ((blk[:, None] == blk[None, :]) & rowtok[:, None] & (~rowtok)[None, :])
    incl = (idx[None, :] >= idx[:, None]) if bwd else (idx[None, :] <= idx[:, None])
    mats[nlev] = incl
    mats[nlev + 1] = 1.0 - incl
    return mats.reshape((nlev + 2) * t, t), masks


def _gla_kernel(*refs, has_init, n_prev):
    refs = list(refs)
    xs = [(refs.pop(0), refs.pop(0), refs.pop(0)) for _ in range(2)]
    wgh_ref, wgl_ref, bg_ref, mall_ref, mask_ref = (refs.pop(0) for _ in range(5))
    st0_ref = refs.pop(0) if has_init else None
    prev_ref = refs.pop(0) if n_prev else None
    o_refs = (refs.pop(0), refs.pop(0))
    stf_ref, st_scr = refs

    @pl.when(pl.program_id(1) == 0)
    def _():
        st_scr[...] = st0_ref[...] if has_init else jnp.zeros(st_scr.shape, F32)

    t = GLA_CHUNK
    nlev = mask_ref.shape[1] - 1
    hk = B_HEADS * B_DK
    e_alls = []
    for di in range(2):
        s_hi, s_lo = _split2(xs[di][2][...])
        wgh = wgh_ref[di]
        logit = _dot(s_hi, wgh) + _dot(s_lo, wgh) + _dot(s_hi, wgl_ref[di]) + bg_ref[di]
        g_all = _log_sigmoid(logit) * (1.0 / B_GATE_NORM)
        e_alls.append(jnp.exp(_dot(mall_ref[di], g_all.astype(BF16))))
    for h in range(B_HEADS):
        for di in range(2):
            qk_ref, v_ref, _ = xs[di]
            o_ref = o_refs[di]
            e_all = e_alls[di]
            edge = 0 if di else t - 1
            ks = slice(h * B_DK, (h + 1) * B_DK)
            vs = slice(h * B_DV, (h + 1) * B_DV)
            q = qk_ref[:, ks].astype(F32) * (B_DK ** -0.5)
            k = qk_ref[:, hk + h * B_DK:hk + (h + 1) * B_DK]
            kf = k.astype(F32)
            v = v_ref[:, vs]
            amat = _dot_nt(q.astype(BF16), k) * mask_ref[di, 0]
            for lvl in range(nlev):
                e = e_all[lvl * t:(lvl + 1) * t, ks]
                amat = amat + _dot_nt((q * e).astype(BF16), (kf * e).astype(BF16)) * mask_ref[di, lvl + 1]
            e_in = e_all[nlev * t:(nlev + 1) * t, ks]
            e_out = e_all[(nlev + 1) * t:(nlev + 2) * t, ks]
            st = st_scr[di, h]
            o = _dot(amat.astype(BF16), v) + _dot_nt((q * e_in).astype(BF16), st.astype(BF16))
            o_ref[:, vs] = o.astype(BF16)
            st_scr[di, h] = st * e_in[edge:edge + 1, :] + _dot_tn(v, (kf * e_out).astype(BF16))

    @pl.when(pl.program_id(1) == pl.num_programs(1) - 1)
    def _():
        if n_prev:
            for di in range(2):
                for h in range(B_HEADS):
                    for lp in range(n_prev):
                        stf_ref[lp, di, h] = prev_ref[lp, di, h].T
                    stf_ref[n_prev, di, h] = st_scr[di, h].T
        else:
            stf_ref[...] = st_scr[...]


def _gla_call(u, s_all, wgh, wgl, bg, mall, masks, st0, *, n_seq, seq_len, row0, prev=None):
    t = GLA_CHUNK
    nch = seq_len // t
    r0 = row0 // t
    hk, hv = B_HEADS * B_DK, B_HEADS * B_DV
    full = lambda a: pl.BlockSpec(a.shape, lambda b, c: (0,) * a.ndim)
    in_specs, args = [], []
    for cidx in (lambda c: c, lambda c: nch - 1 - c):
        in_specs += [pl.BlockSpec((t, 2 * hk), lambda b, c, cidx=cidx: (r0 + b * nch + cidx(c), U_QK // (2 * hk))),
                     pl.BlockSpec((t, hv), lambda b, c, cidx=cidx: (r0 + b * nch + cidx(c), U_V // hv)),
                     pl.BlockSpec((t, LANE), lambda b, c, cidx=cidx: (r0 + b * nch + cidx(c), 0))]
        args += [u, u, s_all]
    in_specs += [full(wgh), full(wgl), full(bg), full(mall), full(masks)]
    args += [wgh, wgl, bg, mall, masks]
    st_shape = (2, B_HEADS, B_DV, B_DK)
    st_spec = pl.BlockSpec((None,) + st_shape, lambda b, c: (b, 0, 0, 0, 0))
    if st0 is not None:
        in_specs.append(st_spec)
        args.append(st0)
    n_prev = 0 if prev is None else prev.shape[1]
    if n_prev:
        in_specs.append(pl.BlockSpec((None, n_prev) + st_shape, lambda b, c: (b, 0, 0, 0, 0, 0)))
        args.append(prev)
        st_shape = (n_prev + 1, 2, B_HEADS, B_DK, B_DV)
        st_spec = pl.BlockSpec((None,) + st_shape, lambda b, c: (b, 0, 0, 0, 0, 0))
    return pl.pallas_call(
        functools.partial(_gla_kernel, has_init=st0 is not None, n_prev=n_prev),
        grid=(n_seq, nch),
        in_specs=in_specs,
        out_specs=[pl.BlockSpec((t, hv), lambda b, c: (b * nch + c, 0)),
                   pl.BlockSpec((t, hv), lambda b, c: (b * nch + nch - 1 - c, 0)),
                   st_spec],
        out_shape=[jax.ShapeDtypeStruct((n_seq * seq_len, hv), BF16),
                   jax.ShapeDtypeStruct((n_seq * seq_len, hv), BF16),
                   jax.ShapeDtypeStruct((n_seq,) + st_shape, F32)],
        scratch_shapes=[pltpu.VMEM((2, B_HEADS, B_DV, B_DK), F32)],
        compiler_params=pltpu.CompilerParams(dimension_semantics=("parallel", "arbitrary")),
        name="gla_scan",
    )(*args)


def _merge_kernel(x_ref, mod_ref, oa_ref, gf_ref, gb_ref, og_ref, yf_ref, yb_ref, xc_ref, z_ref, gt_ref,
                  awo_ref, onw_ref, bwo_ref, dexp_ref, cnw_ref, cwo_ref, wout_ref, o_ref):
    o_a = _dot(oa_ref[...], awo_ref[...])

    og = gf_ref[...].astype(F32) + gb_ref[...].astype(F32)
    onw = onw_ref[...]
    parts = []
    for h in range(B_HEADS):
        th = og[:, h * B_DV:(h + 1) * B_DV]
        parts.append(th * lax.rsqrt(jnp.mean(th * th, axis=-1, keepdims=True) + EPS) * onw)
    ob_in = jnp.concatenate(parts, axis=-1) * _silu(og_ref[...].astype(F32))
    o_b = _dot(ob_in.astype(BF16), bwo_ref[...])

    y = yf_ref[...].astype(F32) + yb_ref[...].astype(F32) + dexp_ref[...] * xc_ref[...].astype(F32)
    y = y * _silu(z_ref[...].astype(F32))
    y = y * lax.rsqrt(jnp.mean(y * y, axis=-1, keepdims=True) + EPS) * cnw_ref[...]
    o_c = _dot(y.astype(BF16), cwo_ref[...])

    d = D_MODEL
    merged = (jax.nn.sigmoid(gt_ref[:, 0:d].astype(F32)) * o_a
              + jax.nn.sigmoid(gt_ref[:, d:2 * d].astype(F32)) * o_b
              + jax.nn.sigmoid(gt_ref[:, 2 * d:3 * d].astype(F32)) * o_c)
    out = _dot(merged.astype(BF16), wout_ref[...])
    o_ref[...] = x_ref[...] + mod_ref[:, 2 * d:3 * d] * out


def _merge_call(x, mod3, u, o_attn, g_f, g_b, y_f, y_b, xbc, wts, *, n_rows, row0, layer):
    tm = 512
    r0 = row0 // tm
    d = D_MODEL
    loc = lambda w: pl.BlockSpec((tm, w), lambda i: (i, 0))
    full = lambda a: pl.BlockSpec(a.shape, lambda i: (0,) * a.ndim)
    in_specs = [pl.BlockSpec((tm, d), lambda i: (i + r0, 0)),
                pl.BlockSpec((None, 1, 6 * d), lambda i: (_mod_row(i + r0, tm, layer), 0, 0)),
                loc(A_HEADS * A_V), loc(d), loc(d),
                pl.BlockSpec((tm, d), lambda i: (i + r0, U_OG // d)),
                loc(d), loc(d),
                pl.BlockSpec((tm, d), lambda i: (i, 0)),
                pl.BlockSpec((tm, d), lambda i: (i + r0, U_Z // d)),
                pl.BlockSpec((tm, 3 * d), lambda i: (i + r0, U_GATES // (3 * d)))]
    w_args = [wts[n] for n in ('awo', 'onw', 'bwo', 'dexp', 'cnw', 'cwo', 'wout')]
    in_specs += [full(a) for a in w_args]
    return pl.pallas_call(
        _merge_kernel,
        grid=(n_rows // tm,),
        in_specs=in_specs,
        out_specs=pl.BlockSpec((tm, d), lambda i: (i + r0, 0)),
        out_shape=jax.ShapeDtypeStruct((N_TOK, d), F32),
        input_output_aliases={0: 0},
        compiler_params=pltpu.CompilerParams(dimension_semantics=("parallel",)),
        name="mixer_merge",
    )(x, mod3, o_attn, g_f, g_b, u, y_f, y_b, xbc, u, u, *w_args)


MOE_TILE = 512
MOE_TOKEN_TILE = 512
MOE_MAX_TILES = 2 * N_TOK // MOE_TILE + N_EXPERTS
MOE_ROWS = MOE_MAX_TILES * MOE_TILE
R_E1, R_E2, R_W1, R_W2, R_P1, R_P2 = 0, 1, 2, 3, 4, 5


def _first_index(vals, target):
    idx = jnp.full(target.shape, len(vals) - 1, jnp.int32)
    for k in reversed(range(len(vals) - 1)):
        idx = jnp.where(vals[k] == target, k, idx)
    return idx


def _top2_of(vals):
    m1 = functools.reduce(jnp.maximum, vals)
    i1 = _first_index(vals, m1)
    rest = [jnp.where(i1 == k, -jnp.inf, v) for k, v in enumerate(vals)]
    m2 = functools.reduce(jnp.maximum, rest)
    return m1, i1, m2, _first_index(rest, m2)


def _pick_by(vals, idx):
    out = vals[-1]
    for k in reversed(range(len(vals) - 1)):
        out = jnp.where(idx == k, vals[k], out)
    return out


def _route_kernel(x_ref, mod_ref, nw_ref, rwh_ref, rwl_ref, rb_ref, tri_ref, h_ref, route_ref, wrec_ref, cnt_ref,
                  base_scr):
    d = D_MODEL
    ng = N_EXPERT_GROUPS
    per_group = N_EXPERTS // ng

    @pl.when(pl.program_id(0) == 0)
    def _():
        base_scr[...] = jnp.zeros(base_scr.shape, F32)

    x = x_ref[...]
    tm = x.shape[0]
    y = x * lax.rsqrt(jnp.mean(x * x, axis=-1, keepdims=True) + EPS) * nw_ref[...]
    h = y * (1.0 + mod_ref[:, 4 * d:5 * d]) + mod_ref[:, 3 * d:4 * d]
    h_ref[...] = h
    h_hi, h_lo = _split2(h)
    rwh = rwh_ref[...]
    logits = _dot_nt(rwh, h_hi) + _dot_nt(rwh, h_lo) + _dot_nt(rwl_ref[...], h_hi)
    scores = jax.nn.sigmoid(logits)
    sel = scores + rb_ref[...]
    sel_k = [sel[k * ng:(k + 1) * ng] for k in range(per_group)]
    sc_k = [scores[k * ng:(k + 1) * ng] for k in range(per_group)]
    m1, i1, m2, i2 = _top2_of(sel_k)
    gsum = m1 + m2
    rows = lambda a: [a[g:g + 1] for g in range(ng)]
    gs = rows(gsum)
    best_v, best_g = gs[0], jnp.zeros((1, tm), jnp.int32)
    for g in range(1, ng):
        upd = gs[g] > best_v
        best_g = jnp.where(upd, g, best_g)
        best_v = jnp.where(upd, gs[g], best_v)
    k1 = _pick_by(rows(i1), best_g)
    k2 = _pick_by(rows(i2), best_g)
    s1 = _pick_by(rows(_pick_by(sc_k, i1)), best_g)
    s2 = _pick_by(rows(_pick_by(sc_k, i2)), best_g)
    tot = s1 + s2
    in_best = lax.broadcasted_iota(jnp.int32, (ng, tm), 0) == best_g
    hit = jnp.concatenate(
        [jnp.where(jnp.logical_and(in_best, jnp.logical_or(k1 == k, k2 == k)), 1.0, 0.0) for k in range(per_group)],
        axis=0)
    rank = _dot(hit.astype(BF16), tri_ref[...]) + base_scr[...]
    rank_k = [rank[k * ng:(k + 1) * ng] for k in range(per_group)]
    p1 = _pick_by(rows(_pick_by(rank_k, k1)), best_g)
    p2 = _pick_by(rows(_pick_by(rank_k, k2)), best_g)
    base_scr[...] += jnp.sum(hit, axis=1, keepdims=True)
    e1 = (best_g * per_group + k1).astype(F32)
    e2 = (best_g * per_group + k2).astype(F32)
    rec = jnp.concatenate([e1, e2, s1 / tot, s2 / tot, p1, p2, jnp.zeros((LANE - 6, tm), F32)], axis=0)
    route_ref[...] = rec[0:8]
    wrec_ref[...] = rec.T
    cnt_ref[...] = jnp.broadcast_to(base_scr[...], cnt_ref.shape)


def _route_call(x, mod3, nw, rwh, rwl, rb, tri, layer):
    tm = tri.shape[0]
    d = D_MODEL
    full = lambda a: pl.BlockSpec(a.shape, lambda i: (0,) * a.ndim)
    return pl.pallas_call(
        _route_kernel,
        grid=(N_TOK // tm,),
        in_specs=[pl.BlockSpec((tm, d), lambda i: (i, 0)),
                  pl.BlockSpec((None, 1, 6 * d), lambda i: (_mod_row(i, tm, layer), 0, 0)),
                  full(nw), full(rwh), full(rwl), full(rb), full(tri)],
        out_specs=[pl.BlockSpec((tm, d), lambda i: (i, 0)),
                   pl.BlockSpec((8, tm), lambda i: (0, i)),
                   pl.BlockSpec((tm, LANE), lambda i: (i, 0)),
                   pl.BlockSpec((N_EXPERTS, LANE), lambda i: (0, 0))],
        out_shape=[jax.ShapeDtypeStruct((N_TOK, d), F32),
                   jax.ShapeDtypeStruct((8, N_TOK), F32),
                   jax.ShapeDtypeStruct((N_TOK, LANE), F32),
                   jax.ShapeDtypeStruct((N_EXPERTS, LANE), F32)],
        scratch_shapes=[pltpu.VMEM((N_EXPERTS, 1), F32)],
        compiler_params=pltpu.CompilerParams(dimension_semantics=("arbitrary",)),
        name="moe_route",
    )(x, mod3, nw, rwh, rwl, rb, tri)


def _row_copy(src, src_row, dst, dst_row, sem):
    return pltpu.make_async_copy(src.at[pl.ds(src_row, 1), :], dst.at[pl.ds(dst_row, 1), :], sem)


def _dispatch_kernel(dst_ref, ztile_ref, h_ref, xs_ref, zbuf, sem):
    tm = h_ref.shape[0]

    @pl.when(pl.program_id(0) == 0)
    def _():
        zbuf[...] = jnp.zeros(zbuf.shape, F32)
        fills = []
        for e in range(N_EXPERTS):
            row = pl.multiple_of(jnp.maximum(ztile_ref[0, e], 0), MOE_TILE)
            fills.append((ztile_ref[0, e] >= 0, pltpu.make_async_copy(zbuf, xs_ref.at[pl.ds(row, MOE_TILE), :], sem)))
        for has_tile, cp in fills:
            pl.when(has_tile)(cp.start)
        for has_tile, cp in fills:
            pl.when(has_tile)(cp.wait)

        def fill_unused(t, carry):
            cp = pltpu.make_async_copy(zbuf, xs_ref.at[pl.ds(pl.multiple_of(t * MOE_TILE, MOE_TILE), MOE_TILE), :], sem)
            cp.start()
            cp.wait()
            return carry

        lax.fori_loop(ztile_ref[0, N_EXPERTS], MOE_MAX_TILES, fill_unused, 0)

    def issue(j, carry):
        for i in range(2):
            r = 2 * j + i
            _row_copy(h_ref, r, xs_ref, dst_ref[0, r], sem).start(priority=i)
            _row_copy(h_ref, r, xs_ref, dst_ref[0, tm + r], sem).start(priority=1 - i)
        return carry

    def drain(r, carry):
        _row_copy(h_ref, 0, xs_ref, 0, sem).wait()
        _row_copy(h_ref, 0, xs_ref, 0, sem).wait()
        return carry

    lax.fori_loop(0, tm // 2, issue, 0, unroll=4)
    lax.fori_loop(0, tm, drain, 0, unroll=8)


def _dispatch_call(dst, ztile, h):
    tm = dst.shape[-1] // 2
    d = D_MODEL
    return pl.pallas_call(
        _dispatch_kernel,
        grid=(N_TOK // tm,),
        in_specs=[pl.BlockSpec((None, 1, 2 * tm), lambda i: (i, 0, 0), memory_space=pltpu.SMEM),
                  pl.BlockSpec((1, N_EXPERTS + 1), lambda i: (0, 0), memory_space=pltpu.SMEM),
                  pl.BlockSpec((tm, d), lambda i: (i, 0))],
        out_specs=pl.BlockSpec(memory_space=pl.ANY),
        out_shape=jax.ShapeDtypeStruct((MOE_ROWS, d), F32),
        scratch_shapes=[pltpu.VMEM((MOE_TILE, d), F32), pltpu.SemaphoreType.DMA(())],
        compiler_params=pltpu.CompilerParams(dimension_semantics=("arbitrary",)),
        name="moe_dispatch",
    )(dst, ztile, h)


def _expert_kernel(te_ref, nt_ref, xs_ref, w1_ref, w3_ref, w2_ref, y_ref, w1b, w3b, w2b):
    t = pl.program_id(0)
    changed = jnp.logical_or(t == 0, te_ref[t] != te_ref[jnp.maximum(t - 1, 0)])

    @pl.when(changed)
    def _():
        w1b[...] = w1_ref[...].astype(BF16)
        w3b[...] = w3_ref[...].astype(BF16)
        w2b[...] = w2_ref[...].astype(BF16)

    @pl.when(t < nt_ref[0])
    def _():
        xb = xs_ref[...].astype(BF16)
        hid = _silu(_dot(xb, w1b[...])) * _dot(xb, w3b[...])
        y_ref[...] = _dot(hid.astype(BF16), w2b[...])

    @pl.when(t >= nt_ref[0])
    def _():
        y_ref[...] = jnp.zeros(y_ref.shape, F32)


def _expert_call(tile_expert, n_tiles, xs, w1, w3, w2, layer):
    d = D_MODEL
    grid_spec = pltpu.PrefetchScalarGridSpec(
        num_scalar_prefetch=2,
        grid=(MOE_MAX_TILES,),
        in_specs=[pl.BlockSpec((MOE_TILE, d), lambda t, te, nt: (jnp.minimum(t, nt[0] - 1), 0)),
                  pl.BlockSpec((None, None, d, D_EXPERT), lambda t, te, nt: (layer, te[t], 0, 0)),
                  pl.BlockSpec((None, None, d, D_EXPERT), lambda t, te, nt: (layer, te[t], 0, 0)),
                  pl.BlockSpec((None, None, D_EXPERT, d), lambda t, te, nt: (layer, te[t], 0, 0))],
        out_specs=pl.BlockSpec((MOE_TILE, d), lambda t, te, nt: (t, 0)),
        scratch_shapes=[pltpu.VMEM((d, D_EXPERT), BF16), pltpu.VMEM((d, D_EXPERT), BF16),
                        pltpu.VMEM((D_EXPERT, d), BF16)])
    return pl.pallas_call(
        _expert_kernel,
        grid_spec=grid_spec,
        out_shape=jax.ShapeDtypeStruct((MOE_ROWS, d), F32),
        compiler_params=pltpu.CompilerParams(dimension_semantics=("arbitrary",)),
        name="moe_experts",
    )(tile_expert, n_tiles, xs, w1, w3, w2)


def _combine_kernel(dst_ref, x_ref, mod_ref, route_ref, y_ref, *rest, split):
    o_refs, (b1, b2, sem) = rest[:-3], rest[-3:]
    tm = x_ref.shape[0]
    d = D_MODEL

    def issue(j, carry):
        for i in range(2):
            r = 2 * j + i
            _row_copy(y_ref, dst_ref[0, r], b1, r, sem).start(priority=i)
            _row_copy(y_ref, dst_ref[0, tm + r], b2, r, sem).start(priority=1 - i)
        return carry

    def drain(r, carry):
        _row_copy(y_ref, 0, b1, 0, sem).wait()
        _row_copy(y_ref, 0, b2, 0, sem).wait()
        return carry

    lax.fori_loop(0, tm // 2, issue, 0, unroll=4)
    lax.fori_loop(0, tm, drain, 0, unroll=8)
    rec = route_ref[...]
    moe = rec[:, R_W1:R_W1 + 1] * b1[...] + rec[:, R_W2:R_W2 + 1] * b2[...]
    out = x_ref[...] + mod_ref[:, 5 * d:6 * d] * moe
    if split:
        is_ctx = pl.program_id(0) < N_CTX // tm

        @pl.when(is_ctx)
        def _():
            o_refs[0][...] = out

        @pl.when(jnp.logical_not(is_ctx))
        def _():
            o_refs[1][...] = out
    else:
        o_refs[0][...] = out


def _combine_call(dst, x, mod3, route, y, layer, split):
    tm = dst.shape[-1] // 2
    d = D_MODEL
    n_ctx = N_CTX // tm
    if split:
        out_specs = [pl.BlockSpec((tm, d), lambda i: (jnp.minimum(i, n_ctx - 1), 0)),
                     pl.BlockSpec((tm, d), lambda i: (jnp.maximum(i - n_ctx, 0), 0))]
        out_shape = [jax.ShapeDtypeStruct((N_CTX, d), F32), jax.ShapeDtypeStruct((N_DEC, d), F32)]
    else:
        out_specs = pl.BlockSpec((tm, d), lambda i: (i, 0))
        out_shape = jax.ShapeDtypeStruct((N_TOK, d), F32)
    return pl.pallas_call(
        functools.partial(_combine_kernel, split=split),
        grid=(N_TOK // tm,),
        in_specs=[pl.BlockSpec((None, 1, 2 * tm), lambda i: (i, 0, 0), memory_space=pltpu.SMEM),
                  pl.BlockSpec((tm, d), lambda i: (i, 0)),
                  pl.BlockSpec((None, 1, 6 * d), lambda i: (_mod_row(i, tm, layer), 0, 0)),
                  pl.BlockSpec((tm, LANE), lambda i: (i, 0)),
                  pl.BlockSpec(memory_space=pl.ANY)],
        out_specs=out_specs,
        out_shape=out_shape,
        scratch_shapes=[pltpu.VMEM((tm, d), F32), pltpu.VMEM((tm, d), F32), pltpu.SemaphoreType.DMA(())],
        compiler_params=pltpu.CompilerParams(dimension_semantics=("arbitrary",)),
        name="moe_combine",
    )(dst, x, mod3, route, y)


def _moe_routed(x, mod3, nw, rwh, rwl, rb, tri, w1, w3, w2, layer, split):
    h, route, wrec, cnt = _route_call(x, mod3, nw, rwh, rwl, rb, tri, layer)
    per_group = N_EXPERTS // N_EXPERT_GROUPS
    counts = cnt[:, 0].reshape(per_group, N_EXPERT_GROUPS).T.reshape(N_EXPERTS).astype(jnp.int32)
    ntile = (counts + MOE_TILE - 1) // MOE_TILE
    tend = jnp.cumsum(ntile)
    row0 = (tend - ntile) * MOE_TILE
    n_tiles = tend[-1:]
    tids = jnp.arange(MOE_MAX_TILES)
    tile_expert = jnp.minimum(jnp.sum(tend[None, :] <= jnp.minimum(tids, n_tiles[0] - 1)[:, None], axis=1),
                              N_EXPERTS - 1).astype(jnp.int32)
    ztile = jnp.concatenate([jnp.where(ntile > 0, row0 + (ntile - 1) * MOE_TILE, -1), n_tiles])
    ztile = ztile.astype(jnp.int32).reshape(1, N_EXPERTS + 1)
    eids = jnp.arange(N_EXPERTS, dtype=F32)
    row0f = row0.astype(F32)

    def dest(e, p):
        return (jnp.sum(jnp.where(e[:, None] == eids[None, :], row0f[None, :], 0.0), axis=1) + p).astype(jnp.int32)

    tm = MOE_TOKEN_TILE
    dst = jnp.concatenate([dest(route[R_E1], route[R_P1]).reshape(N_TOK // tm, 1, tm),
                           dest(route[R_E2], route[R_P2]).reshape(N_TOK // tm, 1, tm)], axis=-1)
    xs = _dispatch_call(dst, ztile, h)
    y = _expert_call(tile_expert, n_tiles.astype(jnp.int32), xs, w1, w3, w2, layer)
    return _combine_call(dst, x, mod3, wrec, y, layer, split)


def _pack_w_in(w):
    p = _col_slices(w)
    z = lambda n: jnp.zeros((D_MODEL, n), w.dtype)
    main = jnp.concatenate([
        p['a_cq'], p['a_ckv'], _head_slot(jnp.concatenate([z(A_NOPE), p['a_krope']], axis=1)), z(U_QK - U_KR - HEAD_PAD),
        p['b_q'], p['b_k'], p['b_v'], p['b_og'], p['c_z'], p['c_xbc'][:, :C_INNER], p['gates'],
        p['c_xbc'][:, C_INNER:]], axis=1)
    small = jnp.concatenate([p['b_gf'], p['b_gb'], p['c_dtf'], p['c_dtb'], p['a_krope'],
                             z(LANE - S_KR - A_ROPE)], axis=1)
    return main.astype(BF16), small.astype(BF16)


def _pair_slots(v):
    z = jnp.zeros_like(v)
    odd = (np.arange(A_HEADS) % 2 == 1)[None, :, None]
    return jnp.concatenate([jnp.where(odd, z, v), jnp.where(odd, v, z)], axis=-1).reshape(v.shape[0], A_HEADS * HEAD_PAD)


def _head_slot(t):
    t = t[..., _SLOT_SRC]
    return jnp.concatenate([t, jnp.zeros(t.shape[:-1] + (HEAD_PAD - A_QK,), t.dtype)], axis=-1)


def _qk_gain(g, scale):
    return (_head_slot(g) * scale).reshape(1, HEAD_PAD)


def _lane_slot(vals, off):
    return jnp.zeros((1, LANE), F32).at[0, off:off + vals.shape[0]].set(vals.astype(F32))


def _rope_tables():
    t = jnp.arange(DEC_SEQ)
    row = (t // GRID_W).astype(F32)
    col = (t % GRID_W).astype(F32)
    n_freq = A_ROPE // 4
    inv_freq = 1.0 / (ROPE_BASE ** (jnp.arange(n_freq, dtype=F32) / n_freq))
    ang = jnp.concatenate([row[:, None] * inv_freq, col[:, None] * inv_freq], axis=-1)
    cos, sin = jnp.cos(ang), jnp.sin(ang)
    ones = lambda n: jnp.ones((DEC_SEQ, n), F32)
    zeros = lambda n: jnp.zeros((DEC_SEQ, n), F32)
    n_hi = HEAD_PAD // 2 - A_ROPE // 2
    ctab = jnp.concatenate([cos, ones(_N_LOW), cos, ones(n_hi)], axis=-1)
    stab = jnp.concatenate([-sin, zeros(_N_LOW), sin, zeros(n_hi)], axis=-1)
    return ctab, stab


def kernel(x_prompt, x_sample, cache_ckv, cache_krope, state_gla, state_ssd, c, c_ctx, w_ada, b_ada, norm1, norm2, w_in, a_q_norm, a_wq, a_kv_norm, a_wkv, a_qk_qnorm, a_qk_knorm, a_wo, b_wg, b_bg, b_onorm, b_wo, c_conv_w, c_conv_b, c_dt_bias, c_A_log, c_D, c_norm, c_wo, w_out, router_w, router_bias, e_w1, e_w3, e_w2):
    d = D_MODEL
    x = jnp.concatenate([x_prompt.reshape(N_CTX, d), x_sample.reshape(N_DEC, d)], axis=0)

    cvecs = jnp.zeros((8, d), F32).at[0].set(c_ctx).at[1:1 + DEC_BATCH].set(c)
    mod3 = _mod_call(cvecs, w_ada, b_ada).reshape(DEPTH * 8, 1, 6 * d)

    rope_tabs = _rope_tables()
    gla_np = (_gla_consts(False), _gla_consts(True))
    gla_mall = jnp.asarray(np.stack([gla_np[0][0], gla_np[1][0]]), BF16)
    gla_masks = jnp.asarray(np.stack([gla_np[0][1], gla_np[1][1]]))
    ssd_consts = _ssd_consts()
    per_group = N_EXPERTS // N_EXPERT_GROUPS
    k_major = np.arange(N_EXPERTS).reshape(N_EXPERT_GROUPS, per_group).T.reshape(-1)
    rwh, rwl = _split2(router_w.T[k_major])
    rb = router_bias.astype(F32)[k_major].reshape(N_EXPERTS, 1)
    ridx = np.arange(1024)
    route_tri = jnp.asarray(ridx[:, None] < ridx[None, :], BF16)

    ckvs, kropes, glas, ssds = [], [], [], []
    for l in range(DEPTH):
        w_main, w_small = _pack_w_in(w_in[l])
        u, s_all = _norm_proj_call(x, mod3, norm1[l].reshape(1, d), w_main, w_small, l)
        kropes.append(s_all[:N_CTX, S_KR:S_KR + A_ROPE].reshape(BATCH, SEQ, A_ROPE))

        pw = A_HEADS * HEAD_PAD
        wq = _head_slot(a_wq[l].reshape(A_QLORA, A_HEADS, A_QK)).reshape(A_QLORA, pw)
        wkv = a_wkv[l].reshape(A_KVLORA, A_HEADS, A_NOPE + A_V)
        wkn = jnp.concatenate([wkv[..., :A_NOPE], jnp.zeros((A_KVLORA, A_HEADS, A_ROPE), wkv.dtype)], axis=-1)
        aw = dict(qnw=a_q_norm[l].reshape(1, A_QLORA), wq=wq.astype(BF16),
                  gq=_qk_gain(a_qk_qnorm[l], A_QK ** -0.5),
                  kvnw=a_kv_norm[l].reshape(1, A_KVLORA),
                  wkn=_head_slot(wkn).reshape(A_KVLORA, pw).astype(BF16),
                  wv=_pair_slots(wkv[..., A_NOPE:]).astype(BF16),
                  gk=_qk_gain(a_qk_knorm[l], 1.0))
        q_c, k_c, v_c, ckvn = _attn_prep_call(N_CTX, 0, (u,), aw, has_q=True, norm_kv=True, rope=False, emit_ckv=True)
        o_c = _attn_call(q_c, k_c, v_c, None, n_seq=BATCH, seq_len=SEQ, hps=A_HEADS, tq=SEQ)
        ckvs.append(ckvn.reshape(BATCH, SEQ, A_KVLORA))
        q_d, k_d, v_d = _attn_prep_call(N_DEC, N_CTX, (u,), aw, has_q=True, norm_kv=True, rope=True, emit_ckv=False,
                                        tables=rope_tabs)
        kr_x = _head_slot(jnp.concatenate([jnp.zeros((DEC_BATCH * PAST_LEN, A_NOPE), F32),
                                            cache_krope[:, l].reshape(DEC_BATCH * PAST_LEN, A_ROPE)], axis=1))
        k_x, v_x = _attn_prep_call(DEC_BATCH * PAST_LEN, 0, (cache_ckv[:, l].reshape(DEC_BATCH * PAST_LEN, A_KVLORA), kr_x),
                                   aw, has_q=False, norm_kv=False, rope=False, emit_ckv=False)
        o_d = _attn_call(q_d, k_d, v_d, (k_x, v_x), n_seq=DEC_BATCH, seq_len=DEC_SEQ, hps=8, tq=512)

        hk = B_HEADS * B_DK
        wg = (jnp.zeros((2, LANE, hk), F32).at[0, S_GF:S_GF + B_GATE_RANK].set(b_wg[l, 0])
              .at[1, S_GB:S_GB + B_GATE_RANK].set(b_wg[l, 1]))
        wgh, wgl = _split2(wg)
        bg = b_bg[l].reshape(2, 1, hk)
        last = l == DEPTH - 1
        g_fc, g_bc, st_gc = _gla_call(u, s_all, wgh, wgl, bg, gla_mall, gla_masks, None, n_seq=BATCH, seq_len=SEQ, row0=0,
                                      prev=jnp.stack(glas, axis=1) if last and glas else None)
        g_fd, g_bd, _ = _gla_call(u, s_all, wgh, wgl, bg, gla_mall, gla_masks, jnp.swapaxes(state_gla[:, l], -1, -2),
                                  n_seq=DEC_BATCH, seq_len=DEC_SEQ, row0=N_CTX)
        glas.append(st_gc)

        w8 = jnp.zeros((8, C_XBC), F32).at[:C_CONV].set(c_conv_w[l])
        cb = c_conv_b[l].reshape(1, C_XBC)
        xbc_c = _conv_call(u, w8, cb, n_seq=BATCH, seq_len=SEQ, row0=0)
        xbc_d = _conv_call(u, w8, cb, n_seq=DEC_BATCH, seq_len=DEC_SEQ, row0=N_CTX)
        dtb = jnp.stack([_lane_slot(c_dt_bias[l, 0], S_DTF), _lane_slot(c_dt_bias[l, 1], S_DTB)])
        a_neg = -jnp.exp(c_A_log[l].astype(F32))
        aneg = jnp.stack([_lane_slot(a_neg[0], S_DTF), _lane_slot(a_neg[1], S_DTB)])
        y_fc, y_bc, st_c = _ssd_call(xbc_c, s_all, dtb, aneg, ssd_consts, None, n_seq=BATCH, seq_len=SEQ, row0=0,
                                     prev=jnp.stack(ssds, axis=1) if last and ssds else None)
        st0 = state_ssd[:, l].reshape(DEC_BATCH, 2, C_HEADS // 2, LANE, C_STATE)
        y_fd, y_bd, _ = _ssd_call(xbc_d, s_all, dtb, aneg, ssd_consts, st0, n_seq=DEC_BATCH, seq_len=DEC_SEQ, row0=N_CTX)
        ssds.append(st_c)

        mw = dict(awo=a_wo[l].astype(BF16), onw=b_onorm[l].reshape(1, B_DV),
                  bwo=b_wo[l].astype(BF16), dexp=jnp.repeat(c_D[l], C_HEADDIM).reshape(1, C_INNER),
                  cnw=c_norm[l].reshape(1, C_INNER), cwo=c_wo[l].astype(BF16), wout=w_out[l].astype(BF16))
        x = _merge_call(x, mod3, u, o_c, g_fc, g_bc, y_fc, y_bc, xbc_c, mw, n_rows=N_CTX, row0=0, layer=l)
        x = _merge_call(x, mod3, u, o_d, g_fd, g_bd, y_fd, y_bd, xbc_d, mw, n_rows=N_DEC, row0=N_CTX, layer=l)

        x = _moe_routed(x, mod3, norm2[l].reshape(1, d), rwh, rwl, rb, route_tri, e_w1, e_w3, e_w2, l, split=last)

    y_prompt = x[0].reshape(BATCH, SEQ, d)
    y_sample = x[1].reshape(DEC_BATCH, DEC_SEQ, d)
    new_ckv = jnp.stack(ckvs, axis=1)
    new_krope = jnp.stack(kropes, axis=1)
    new_state_gla = glas[-1] if DEPTH > 1 else jnp.swapaxes(glas[-1], -1, -2)[:, None]
    new_state_ssd = (ssds[-1] if DEPTH > 1 else ssds[-1][:, None]).reshape(BATCH, DEPTH, 2, C_HEADS, C_HEADDIM, C_STATE)
    return (y_prompt, y_sample, new_ckv, new_krope, new_state_gla, new_state_ssd)
```

```python
import functools
import math

import numpy as np
import jax
import jax.numpy as jnp
from jax import lax
from jax.experimental import pallas as pl
from jax.experimental.pallas import tpu as pltpu

F32 = jnp.float32
BF16 = jnp.bfloat16

D_MODEL = 1024
BATCH = 16
SEQ = 256
DEPTH = 2
DEC_BATCH = 2
DEC_SEQ = 2048
PAST_LEN = 512
GRID_W = 64
EPS = 1e-6
ROPE_BASE = 10000.0

A_HEADS = 16
A_NOPE = 64
A_ROPE = 32
A_QK = A_NOPE + A_ROPE
A_V = 64
A_QLORA = 512
A_KVLORA = 256

B_HEADS = 4
B_DK = 128
B_DV = 256
B_GATE_RANK = 16
B_GATE_NORM = 16.0

C_HEADS = 16
C_HEADDIM = 64
C_INNER = C_HEADS * C_HEADDIM
C_GROUPS = 2
C_STATE = 128
C_XBC = C_INNER + 2 * C_GROUPS * C_STATE
C_CONV = 5

N_EXPERTS = 16
N_EXPERT_GROUPS = 4
D_EXPERT = 512

N_CTX = BATCH * SEQ
N_DEC = DEC_BATCH * DEC_SEQ
N_TOK = N_CTX + N_DEC

LANE = 128
HEAD_PAD = 128
CHUNK = 128
GLA_CHUNK = 256

U_CQ = 0
U_CKV = 512
U_KR = 768
U_QK = 1024
U_V = 2048
U_OG = 3072
U_Z = 4096
U_X = 5120
U_GATES = 6144
U_BC = 9216
U_WIDTH = 9728
S_GF, S_GB, S_DTF, S_DTB, S_KR = 0, 16, 32, 48, 64

_IN_SPLITS = (
    ('a_cq', A_QLORA), ('a_ckv', A_KVLORA), ('a_krope', A_ROPE),
    ('b_q', B_HEADS * B_DK), ('b_k', B_HEADS * B_DK), ('b_v', B_HEADS * B_DV),
    ('b_og', B_HEADS * B_DV), ('b_gf', B_GATE_RANK), ('b_gb', B_GATE_RANK),
    ('c_z', C_INNER), ('c_xbc', C_XBC), ('c_dtf', C_HEADS), ('c_dtb', C_HEADS),
    ('gates', 3 * D_MODEL),
)
_N_LOW = HEAD_PAD // 2 - A_ROPE // 2
_SLOT_SRC = np.concatenate([A_NOPE + np.arange(0, A_ROPE, 2), np.arange(0, _N_LOW),
                            A_NOPE + np.arange(1, A_ROPE, 2), np.arange(_N_LOW, A_NOPE)])


def _col_slices(w):
    parts, start = {}, 0
    for name, size in _IN_SPLITS:
        parts[name] = w[:, start:start + size]
        start += size
    return parts


def _dot(a, b):
    return jnp.dot(a, b, preferred_element_type=F32)


def _dot_nt(a, b):
    return lax.dot_general(a, b, (((1,), (1,)), ((), ())), preferred_element_type=F32)


def _dot_tn(a, b):
    return lax.dot_general(a, b, (((0,), (0,)), ((), ())), preferred_element_type=F32)


def _split2(x):
    hi = x.astype(BF16)
    lo = (x - hi.astype(F32)).astype(BF16)
    return hi, lo


def _silu(x):
    return x * jax.nn.sigmoid(x)


def _softplus(x):
    return jnp.maximum(x, 0.0) + jnp.log1p(jnp.exp(-jnp.abs(x)))


def _log_sigmoid(x):
    return jnp.minimum(x, 0.0) - jnp.log1p(jnp.exp(-jnp.abs(x)))


def _mod_kernel(c_ref, w_ref, b_ref, o_ref):
    @pl.when(pl.program_id(1) == 0)
    def _():
        o_ref[...] = jnp.broadcast_to(b_ref[...], o_ref.shape)

    s = _silu(c_ref[...]).astype(BF16)
    o_ref[...] += _dot(s, w_ref[...].astype(BF16))


def _mod_call(cvecs, w_ada, b_ada):
    tk = 256
    return pl.pallas_call(
        _mod_kernel,
        grid=(DEPTH, D_MODEL // tk),
        in_specs=[pl.BlockSpec((8, tk), lambda l, k: (0, k)),
                  pl.BlockSpec((None, tk, 6 * D_MODEL), lambda l, k: (l, k, 0)),
                  pl.BlockSpec((None, 1, 6 * D_MODEL), lambda l, k: (l, 0, 0))],
        out_specs=pl.BlockSpec((None, 8, 6 * D_MODEL), lambda l, k: (l, 0, 0)),
        out_shape=jax.ShapeDtypeStruct((DEPTH, 8, 6 * D_MODEL), F32),
        compiler_params=pltpu.CompilerParams(dimension_semantics=("parallel", "arbitrary")),
        name="adaln_mod",
    )(cvecs, w_ada, b_ada.reshape(DEPTH, 1, 6 * D_MODEL))


def _mod_row(i, tm, layer):
    n_ctx = N_CTX // tm
    per_b = DEC_SEQ // tm
    return layer * 8 + jnp.where(i < n_ctx, 0, 1 + (i - n_ctx) // per_b)


def _norm_proj_kernel(x_ref, mod_ref, nw_ref, w_ref, ws_ref, u_ref, s_ref, h_scr):
    @pl.when(pl.program_id(1) == 0)
    def _():
        x = x_ref[...]
        m = mod_ref[...]
        y = x * lax.rsqrt(jnp.mean(x * x, axis=-1, keepdims=True) + EPS) * nw_ref[...]
        h = (y * (1.0 + m[:, D_MODEL:2 * D_MODEL]) + m[:, 0:D_MODEL]).astype(BF16)
        h_scr[...] = h
        s_ref[...] = _dot(h, ws_ref[...])

    u_ref[...] = _dot(h_scr[...], w_ref[...]).astype(BF16)


def _norm_proj_call(x, mod3, nw, w_main, w_small, layer):
    tm, tn = 2048, 512
    return pl.pallas_call(
        _norm_proj_kernel,
        grid=(N_TOK // tm, U_WIDTH // tn),
        in_specs=[pl.BlockSpec((tm, D_MODEL), lambda i, j: (i, 0)),
                  pl.BlockSpec((None, 1, 6 * D_MODEL), lambda i, j: (_mod_row(i, tm, layer), 0, 0)),
                  pl.BlockSpec((1, D_MODEL), lambda i, j: (0, 0)),
                  pl.BlockSpec((D_MODEL, tn), lambda i, j: (0, j)),
                  pl.BlockSpec((D_MODEL, LANE), lambda i, j: (0, 0))],
        out_specs=[pl.BlockSpec((tm, tn), lambda i, j: (i, j)),
                   pl.BlockSpec((tm, LANE), lambda i, j: (i, 0))],
        out_shape=[jax.ShapeDtypeStruct((N_TOK, U_WIDTH), BF16),
                   jax.ShapeDtypeStruct((N_TOK, LANE), F32)],
        scratch_shapes=[pltpu.VMEM((tm, D_MODEL), BF16)],
        compiler_params=pltpu.CompilerParams(dimension_semantics=("parallel", "arbitrary")),
        name="norm_in_proj",
    )(x, mod3, nw, w_main, w_small)


def _rope(t, cos, sin):
    return t * cos + pltpu.roll(t, HEAD_PAD // 2, 1) * sin


def _head_scale(ss):
    return lax.rsqrt(ss * (1.0 / A_QK) + EPS)


def _attn_prep_kernel(*refs, has_q, norm_kv, rope, emit_ckv):
    refs = list(refs)
    cq_ref = refs.pop(0) if has_q else None
    ckv_ref, kr_ref = refs.pop(0), refs.pop(0)
    if has_q:
        qnw_ref, wq_ref, gq_ref = refs.pop(0), refs.pop(0), refs.pop(0)
    kvnw_ref, wkn_ref, wv_ref, gk_ref = refs.pop(0), refs.pop(0), refs.pop(0), refs.pop(0)
    cos = sin = None
    if rope:
        cos, sin = refs.pop(0)[...], refs.pop(0)[...]
    q_ref = refs.pop(0) if has_q else None
    k_ref, v_ref = refs.pop(0), refs.pop(0)
    ckvn_ref = refs.pop(0) if emit_ckv else None

    ckv = ckv_ref[...].astype(F32)
    if norm_kv:
        ckv = ckv * lax.rsqrt(jnp.mean(ckv * ckv, axis=-1, keepdims=True) + EPS) * kvnw_ref[...]
    if emit_ckv:
        ckvn_ref[...] = ckv
    ckv_b = ckv.astype(BF16)
    v_ref[...] = _dot(ckv_b, wv_ref[...]).astype(BF16)
    kf = _dot(ckv_b, wkn_ref[...])
    kr = kr_ref[...].astype(F32)
    gk = gk_ref[...]
    ss_kr = jnp.sum(kr * kr, axis=-1, keepdims=True)
    krg = kr * gk
    if rope:
        krg = _rope(krg, cos, sin)
    for h in range(A_HEADS):
        sl = slice(h * HEAD_PAD, (h + 1) * HEAD_PAD)
        kn = kf[:, sl]
        scale = _head_scale(jnp.sum(kn * kn, axis=-1, keepdims=True) + ss_kr)
        k_ref[:, sl] = ((kn * gk + krg) * scale).astype(BF16)

    if has_q:
        cq = cq_ref[...].astype(F32)
        cq = cq * lax.rsqrt(jnp.mean(cq * cq, axis=-1, keepdims=True) + EPS) * qnw_ref[...]
        qf = _dot(cq.astype(BF16), wq_ref[...])
        gq = gq_ref[...]
        for h in range(A_HEADS):
            sl = slice(h * HEAD_PAD, (h + 1) * HEAD_PAD)
            t = qf[:, sl]
            t = t * _head_scale(jnp.sum(t * t, axis=-1, keepdims=True)) * gq
            q_ref[:, sl] = (_rope(t, cos, sin) if rope else t).astype(BF16)


def _attn_prep_call(n_rows, row0, srcs, wts, *, has_q, norm_kv, rope, emit_ckv, tables=None):
    tm = 512
    r0 = row0 // tm
    pw = A_HEADS * HEAD_PAD
    full = lambda shape: pl.BlockSpec(shape, lambda i: (0,) * len(shape))
    args, in_specs = [], []
    if len(srcs) == 1:
        u = srcs[0]
        if has_q:
            args.append(u)
            in_specs.append(pl.BlockSpec((tm, A_QLORA), lambda i: (i + r0, U_CQ // A_QLORA)))
        args += [u, u]
        in_specs += [pl.BlockSpec((tm, A_KVLORA), lambda i: (i + r0, U_CKV // A_KVLORA)),
                     pl.BlockSpec((tm, LANE), lambda i: (i + r0, U_KR // LANE))]
    else:
        args += list(srcs)
        in_specs += [pl.BlockSpec((tm, A_KVLORA), lambda i: (i, 0)),
                     pl.BlockSpec((tm, LANE), lambda i: (i, 0))]
    if has_q:
        args += [wts['qnw'], wts['wq'], wts['gq']]
        in_specs += [full((1, A_QLORA)), full((A_QLORA, pw)), full((1, HEAD_PAD))]
    args += [wts['kvnw'], wts['wkn'], wts['wv'], wts['gk']]
    in_specs += [full((1, A_KVLORA)), full((A_KVLORA, pw)), full((A_KVLORA, pw)), full((1, HEAD_PAD))]
    if rope:
        per_seq = DEC_SEQ // tm
        args += list(tables)
        in_specs += [pl.BlockSpec((tm, HEAD_PAD), lambda i: (i % per_seq, 0))] * 2
    out_specs, out_shape = [], []
    for _ in range((1 if has_q else 0) + 2):
        out_specs.append(pl.BlockSpec((tm, pw), lambda i: (i, 0)))
        out_shape.append(jax.ShapeDtypeStruct((n_rows, pw), BF16))
    if emit_ckv:
        out_specs.append(pl.BlockSpec((tm, A_KVLORA), lambda i: (i, 0)))
        out_shape.append(jax.ShapeDtypeStruct((n_rows, A_KVLORA), F32))
    return pl.pallas_call(
        functools.partial(_attn_prep_kernel, has_q=has_q, norm_kv=norm_kv, rope=rope, emit_ckv=emit_ckv),
        grid=(n_rows // tm,),
        in_specs=in_specs, out_specs=out_specs, out_shape=out_shape,
        compiler_params=pltpu.CompilerParams(dimension_semantics=("parallel",)),
        name="mla_prep",
    )(*args)


def _attn_kernel(*refs, hps, has_ctx):
    if has_ctx:
        q_ref, k_ref, v_ref, kc_ref, vc_ref, o_ref = refs
    else:
        q_ref, k_ref, v_ref, o_ref = refs

    def scores(hh):
        sl = slice(hh * HEAD_PAD, (hh + 1) * HEAD_PAD)
        q = q_ref[:, sl]
        return _dot_nt(q, k_ref[:, sl]), (_dot_nt(q, kc_ref[:, sl]) if has_ctx else None)

    nxt = scores(0)
    even = None
    for hh in range(hps):
        sl = slice(hh * HEAD_PAD, (hh + 1) * HEAD_PAD)
        s, s2 = nxt
        if hh + 1 < hps:
            nxt = scores(hh + 1)
        m = jnp.max(s, axis=-1, keepdims=True)
        if has_ctx:
            m = jnp.maximum(m, jnp.max(s2, axis=-1, keepdims=True))
        p = jnp.exp(s - m)
        den = jnp.sum(p, axis=-1, keepdims=True)
        o = _dot(p.astype(BF16), v_ref[:, sl])
        if has_ctx:
            p2 = jnp.exp(s2 - m)
            den = den + jnp.sum(p2, axis=-1, keepdims=True)
            o = o + _dot(p2.astype(BF16), vc_ref[:, sl])
        o = o / den
        if hh % 2 == 0:
            even = o
        else:
            pair = hh // 2
            o_ref[:, pair * LANE:(pair + 1) * LANE] = (even + o).astype(BF16)


def _attn_call(q, k, v, ctx_kv, *, n_seq, seq_len, hps, tq):
    pw = A_HEADS * HEAD_PAD
    bw = hps * HEAD_PAD
    nq = seq_len // tq
    in_specs = [pl.BlockSpec((tq, bw), lambda b, h, i: (b * nq + i, h)),
                pl.BlockSpec((seq_len, bw), lambda b, h, i: (b, h)),
                pl.BlockSpec((seq_len, bw), lambda b, h, i: (b, h))]
    args = [q, k, v]
    if ctx_kv is not None:
        in_specs += [pl.BlockSpec((PAST_LEN, bw), lambda b, h, i: (b, h))] * 2
        args += list(ctx_kv)
    return pl.pallas_call(
        functools.partial(_attn_kernel, hps=hps, has_ctx=ctx_kv is not None),
        grid=(n_seq, A_HEADS // hps, nq),
        in_specs=in_specs,
        out_specs=pl.BlockSpec((tq, hps * A_V), lambda b, h, i: (b * nq + i, h)),
        out_shape=jax.ShapeDtypeStruct((n_seq * seq_len, A_HEADS * A_V), BF16),
        compiler_params=pltpu.CompilerParams(dimension_semantics=("parallel", "parallel", "arbitrary")),
        name="mla_attention",
    )(*args)


def _conv_kernel(x_ref, w_ref, b_ref, o_ref):
    x = x_ref[...].astype(F32)
    n = x.shape[0]
    row = lax.broadcasted_iota(jnp.int32, x.shape, 0)
    half = (C_CONV - 1) // 2
    acc = x * w_ref[half:half + 1, :] + b_ref[...]
    for d in range(-half, half + 1):
        if d == 0:
            continue
        shifted = pltpu.roll(x, (-d) % n, 0)
        valid = jnp.logical_and(row + d >= 0, row + d < n)
        acc = acc + jnp.where(valid, shifted, 0.0) * w_ref[half + d:half + d + 1, :]
    o_ref[...] = _silu(acc).astype(BF16)


def _conv_call(u, w8, b, *, n_seq, seq_len, row0):
    tc = 512
    b0 = row0 // seq_len
    nx = C_INNER // tc
    ucol = lambda j: jnp.where(j < nx, U_X // tc + j, U_BC // tc + j - nx)
    return pl.pallas_call(
        _conv_kernel,
        grid=(n_seq, C_XBC // tc),
        in_specs=[pl.BlockSpec((seq_len, tc), lambda s, j: (s + b0, ucol(j))),
                  pl.BlockSpec((8, tc), lambda s, j: (0, j)),
                  pl.BlockSpec((1, tc), lambda s, j: (0, j))],
        out_specs=pl.BlockSpec((seq_len, tc), lambda s, j: (s, j)),
        out_shape=jax.ShapeDtypeStruct((n_seq * seq_len, C_XBC), BF16),
        compiler_params=pltpu.CompilerParams(dimension_semantics=("parallel", "parallel")),
        name="ssd_conv",
    )(u, w8, b)


def _ssd_consts():
    idx = np.arange(CHUNK)
    tri = np.stack([idx[None, :] <= idx[:, None], idx[None, :] >= idx[:, None]]).astype(np.float32)
    expand = np.zeros((2, LANE, C_INNER), np.float32)
    tile = np.zeros((2, LANE, C_HEADS * LANE), np.float32)
    for di, off in enumerate((S_DTF, S_DTB)):
        for h in range(C_HEADS):
            expand[di, off + h, h * C_HEADDIM:(h + 1) * C_HEADDIM] = 1.0
            tile[di, off + h, h * LANE:(h + 1) * LANE] = 1.0
    return jnp.asarray(tri, BF16), jnp.asarray(expand, BF16), jnp.asarray(tile, BF16)


def _ssd_kernel(*refs, has_init, n_prev):
    refs = list(refs)
    xs = [(refs.pop(0), refs.pop(0), refs.pop(0)) for _ in range(2)]
    dtb_ref, an_ref, tri_ref, exp_ref, tile_ref = (refs.pop(0) for _ in range(5))
    st0_ref = refs.pop(0) if has_init else None
    prev_ref = refs.pop(0) if n_prev else None
    y_refs = (refs.pop(0), refs.pop(0))
    stf_ref, st_scr = refs

    @pl.when(pl.program_id(1) == 0)
    def _():
        st_scr[...] = st0_ref[...] if has_init else jnp.zeros(st_scr.shape, F32)

    t = CHUNK
    ri = lax.broadcasted_iota(jnp.int32, (t, t), 0)
    ci = lax.broadcasted_iota(jnp.int32, (t, t), 1)
    prow = lax.broadcasted_iota(jnp.int32, (LANE, C_STATE), 0)
    gs = C_GROUPS * C_STATE
    pairs_per_group = C_HEADS // C_GROUPS // 2
    for di in range(2):
        x_ref, bc_ref, s_ref = xs[di]
        y_ref = y_refs[di]
        off = (S_DTF, S_DTB)[di]
        dt = _softplus(s_ref[...] + dtb_ref[di])
        a = dt * an_ref[di]
        a_hi, a_lo = _split2(a)
        tri = tri_ref[di]
        acum = _dot(tri, a_hi) + _dot(tri, a_lo)
        acum_t = acum.T
        dt_t = dt.T
        edge = 0 if di else t - 1
        atot = acum[edge:edge + 1, :]
        causal = (ci >= ri) if di else (ci <= ri)
        dec_all = jnp.exp(atot)
        e_exp = _dot(jnp.exp(acum).astype(BF16), exp_ref[di])
        wj_exp = _dot((jnp.exp(atot - acum) * dt).astype(BF16), exp_ref[di])
        c_hi, c_lo = _split2(acum)
        acol = _dot(c_hi, tile_ref[di]) + _dot(c_lo, tile_ref[di])
        bc = bc_ref[...]
        cb, bmat, cmat = [], [], []
        for g in range(C_GROUPS):
            bmat.append(bc[:, g * C_STATE:(g + 1) * C_STATE])
            cmat.append(bc[:, gs + g * C_STATE:gs + (g + 1) * C_STATE])
            cb.append(_dot_nt(cmat[g], bmat[g]))
        lane = lax.broadcasted_iota(jnp.int32, (t, LANE), 1)
        low = lane < C_HEADDIM
        for m in range(C_HEADS // 2):
            g = m // pairs_per_group
            ps = slice(m * LANE, (m + 1) * LANE)
            xp = x_ref[:, ps]
            y_diag = []
            for hh in range(2):
                h = 2 * m + hh
                la = off + h
                lm = jnp.exp(jnp.where(causal, acol[:, h * LANE:(h + 1) * LANE] - acum_t[la:la + 1, :], -jnp.inf))
                w = (cb[g] * lm * dt_t[la:la + 1, :]).astype(BF16)
                y_diag.append(_dot(w, xp))
            st = st_scr[di, m]
            y_off = _dot_nt(cmat[g], st.astype(BF16)) * e_exp[:, ps]
            y_ref[:, ps] = (jnp.where(low, y_diag[0], y_diag[1]) + y_off).astype(BF16)
            xw = (xp.astype(F32) * wj_exp[:, ps]).astype(BF16)
            la = off + 2 * m
            dec = jnp.where(prow < C_HEADDIM, dec_all[:, la:la + 1], dec_all[:, la + 1:la + 2])
            st_scr[di, m] = dec * st + _dot_tn(xw, bmat[g])

    @pl.when(pl.program_id(1) == pl.num_programs(1) - 1)
    def _():
        if n_prev:
            stf_ref[0:n_prev] = prev_ref[...]
            stf_ref[n_prev] = st_scr[...]
        else:
            stf_ref[...] = st_scr[...]


def _ssd_call(xbc, s_all, dtb, aneg, consts, st0, *, n_seq, seq_len, row0, prev=None):
    t = CHUNK
    nch = seq_len // t
    r0 = row0 // t
    npair = C_HEADS // 2
    bcw = 2 * C_GROUPS * C_STATE
    tri, expand, tile = consts
    full = lambda a: pl.BlockSpec(a.shape, lambda b, c: (0,) * a.ndim)
    in_specs, args = [], []
    for cidx in (lambda c: c, lambda c: nch - 1 - c):
        in_specs += [pl.BlockSpec((t, C_INNER), lambda b, c, cidx=cidx: (b * nch + cidx(c), 0)),
                     pl.BlockSpec((t, bcw), lambda b, c, cidx=cidx: (b * nch + cidx(c), C_INNER // bcw)),
                     pl.BlockSpec((t, LANE), lambda b, c, cidx=cidx: (r0 + b * nch + cidx(c), 0))]
        args += [xbc, xbc, s_all]
    in_specs += [full(dtb), full(aneg), full(tri), full(expand), full(tile)]
    args += [dtb, aneg, tri, expand, tile]
    st_shape = (2, npair, LANE, C_STATE)
    st_spec = pl.BlockSpec((None,) + st_shape, lambda b, c: (b, 0, 0, 0, 0))
    if st0 is not None:
        in_specs.append(st_spec)
        args.append(st0)
    n_prev = 0 if prev is None else prev.shape[1]
    if n_prev:
        in_specs.append(pl.BlockSpec((None, n_prev) + st_shape, lambda b, c: (b, 0, 0, 0, 0, 0)))
        args.append(prev)
        st_shape = (n_prev + 1,) + st_shape
        st_spec = pl.BlockSpec((None,) + st_shape, lambda b, c: (b, 0, 0, 0, 0, 0))
    return pl.pallas_call(
        functools.partial(_ssd_kernel, has_init=st0 is not None, n_prev=n_prev),
        grid=(n_seq, nch),
        in_specs=in_specs,
        out_specs=[pl.BlockSpec((t, C_INNER), lambda b, c: (b * nch + c, 0)),
                   pl.BlockSpec((t, C_INNER), lambda b, c: (b * nch + nch - 1 - c, 0)),
                   st_spec],
        out_shape=[jax.ShapeDtypeStruct((n_seq * seq_len, C_INNER), BF16),
                   jax.ShapeDtypeStruct((n_seq * seq_len, C_INNER), BF16),
                   jax.ShapeDtypeStruct((n_seq,) + st_shape, F32)],
        scratch_shapes=[pltpu.VMEM((2, npair, LANE, C_STATE), F32)],
        compiler_params=pltpu.CompilerParams(dimension_semantics=("parallel", "arbitrary")),
        name="ssd_scan",
    )(*args)


def _gla_consts(bwd):
    t = GLA_CHUNK
    nlev = int(math.log2(t))
    idx = np.arange(t)
    mats = np.zeros((nlev + 2, t, t), np.float32)
    masks = np.zeros((nlev + 1, t, t), np.float32)
    masks[0] = np.eye(t)
    for lvl in range(nlev):
        s = 1 << lvl
        blk = idx // (2 * s)
        upper = (idx % (2 * s)) >= s
        last_low = blk * 2 * s + s - 1
        rowtok = ~upper if bwd else upper
        for i in range(t):
            r = last_low[i]
            if not bwd:
                if upper[i]:
                    mats[lvl, i, r + 1:i + 1] = 1.0
                else:
                    mats[lvl, i, i + 1:r + 1] = 1.0
            else:
                if upper[i]:
                    mats[lvl, i, r + 1:i] = 1.0
                else:
                    mats[lvl, i, i:r + 1] = 1.0
        masks[lvl + 1] = ((blk[:, None] == blk[None, :]) & rowtok[:, None] & (~rowtok)[None, :])
    incl = (idx[None, :] >= idx[:, None]) if bwd else (idx[None, :] <= idx[:, None])
    mats[nlev] = incl
    mats[nlev + 1] = 1.0 - incl
    return mats.reshape((nlev + 2) * t, t), masks


def _gla_kernel(*refs, has_init, n_prev):
    refs = list(refs)
    xs = [(refs.pop(0), refs.pop(0), refs.pop(0)) for _ in range(2)]
    wgh_ref, wgl_ref, bg_ref, mall_ref, mask_ref = (refs.pop(0) for _ in range(5))
    st0_ref = refs.pop(0) if has_init else None
    prev_ref = refs.pop(0) if n_prev else None
    o_refs = (refs.pop(0), refs.pop(0))
    stf_ref, st_scr = refs

    @pl.when(pl.program_id(1) == 0)
    def _():
        st_scr[...] = st0_ref[...] if has_init else jnp.zeros(st_scr.shape, F32)

    t = GLA_CHUNK
    nlev = mask_ref.shape[1] - 1
    hk = B_HEADS * B_DK
    e_alls = []
    for di in range(2):
        s_hi, s_lo = _split2(xs[di][2][...])
        wgh = wgh_ref[di]
        logit = _dot(s_hi, wgh) + _dot(s_lo, wgh) + _dot(s_hi, wgl_ref[di]) + bg_ref[di]
        g_all = _log_sigmoid(logit) * (1.0 / B_GATE_NORM)
        e_alls.append(jnp.exp(_dot(mall_ref[di], g_all.astype(BF16))))
    for h in range(B_HEADS):
        for di in range(2):
            qk_ref, v_ref, _ = xs[di]
            o_ref = o_refs[di]
            e_all = e_alls[di]
            edge = 0 if di else t - 1
            ks = slice(h * B_DK, (h + 1) * B_DK)
            vs = slice(h * B_DV, (h + 1) * B_DV)
            q = qk_ref[:, ks].astype(F32) * (B_DK ** -0.5)
            k = qk_ref[:, hk + h * B_DK:hk + (h + 1) * B_DK]
            kf = k.astype(F32)
            v = v_ref[:, vs]
            amat = _dot_nt(q.astype(BF16), k) * mask_ref[di, 0]
            for lvl in range(nlev):
                e = e_all[lvl * t:(lvl + 1) * t, ks]
                amat = amat + _dot_nt((q * e).astype(BF16), (kf * e).astype(BF16)) * mask_ref[di, lvl + 1]
            e_in = e_all[nlev * t:(nlev + 1) * t, ks]
            e_out = e_all[(nlev + 1) * t:(nlev + 2) * t, ks]
            st = st_scr[di, h]
            o = _dot(amat.astype(BF16), v) + _dot_nt((q * e_in).astype(BF16), st.astype(BF16))
            o_ref[:, vs] = o.astype(BF16)
            st_scr[di, h] = st * e_in[edge:edge + 1, :] + _dot_tn(v, (kf * e_out).astype(BF16))

    @pl.when(pl.program_id(1) == pl.num_programs(1) - 1)
    def _():
        if n_prev:
            for di in range(2):
                for h in range(B_HEADS):
                    for lp in range(n_prev):
                        stf_ref[lp, di, h] = prev_ref[lp, di, h].T
                    stf_ref[n_prev, di, h] = st_scr[di, h].T
        else:
            stf_ref[...] = st_scr[...]


def _gla_call(u, s_all, wgh, wgl, bg, mall, masks, st0, *, n_seq, seq_len, row0, prev=None):
    t = GLA_CHUNK
    nch = seq_len // t
    r0 = row0 // t
    hk, hv = B_HEADS * B_DK, B_HEADS * B_DV
    full = lambda a: pl.BlockSpec(a.shape, lambda b, c: (0,) * a.ndim)
    in_specs, args = [], []
    for cidx in (lambda c: c, lambda c: nch - 1 - c):
        in_specs += [pl.BlockSpec((t, 2 * hk), lambda b, c, cidx=cidx: (r0 + b * nch + cidx(c), U_QK // (2 * hk))),
                     pl.BlockSpec((t, hv), lambda b, c, cidx=cidx: (r0 + b * nch + cidx(c), U_V // hv)),
                     pl.BlockSpec((t, LANE), lambda b, c, cidx=cidx: (r0 + b * nch + cidx(c), 0))]
        args += [u, u, s_all]
    in_specs += [full(wgh), full(wgl), full(bg), full(mall), full(masks)]
    args += [wgh, wgl, bg, mall, masks]
    st_shape = (2, B_HEADS, B_DV, B_DK)
    st_spec = pl.BlockSpec((None,) + st_shape, lambda b, c: (b, 0, 0, 0, 0))
    if st0 is not None:
        in_specs.append(st_spec)
        args.append(st0)
    n_prev = 0 if prev is None else prev.shape[1]
    if n_prev:
        in_specs.append(pl.BlockSpec((None, n_prev) + st_shape, lambda b, c: (b, 0, 0, 0, 0, 0)))
        args.append(prev)
        st_shape = (n_prev + 1, 2, B_HEADS, B_DK, B_DV)
        st_spec = pl.BlockSpec((None,) + st_shape, lambda b, c: (b, 0, 0, 0, 0, 0))
    return pl.pallas_call(
        functools.partial(_gla_kernel, has_init=st0 is not None, n_prev=n_prev),
        grid=(n_seq, nch),
        in_specs=in_specs,
        out_specs=[pl.BlockSpec((t, hv), lambda b, c: (b * nch + c, 0)),
                   pl.BlockSpec((t, hv), lambda b, c: (b * nch + nch - 1 - c, 0)),
                   st_spec],
        out_shape=[jax.ShapeDtypeStruct((n_seq * seq_len, hv), BF16),
                   jax.ShapeDtypeStruct((n_seq * seq_len, hv), BF16),
                   jax.ShapeDtypeStruct((n_seq,) + st_shape, F32)],
        scratch_shapes=[pltpu.VMEM((2, B_HEADS, B_DV, B_DK), F32)],
        compiler_params=pltpu.CompilerParams(dimension_semantics=("parallel", "arbitrary")),
        name="gla_scan",
    )(*args)


def _merge_kernel(x_ref, mod_ref, oa_ref, gf_ref, gb_ref, og_ref, yf_ref, yb_ref, xc_ref, z_ref, gt_ref,
                  awo_ref, onw_ref, bwo_ref, dexp_ref, cnw_ref, cwo_ref, wout_ref, o_ref):
    o_a = _dot(oa_ref[...], awo_ref[...])

    og = gf_ref[...].astype(F32) + gb_ref[...].astype(F32)
    onw = onw_ref[...]
    parts = []
    for h in range(B_HEADS):
        th = og[:, h * B_DV:(h + 1) * B_DV]
        parts.append(th * lax.rsqrt(jnp.mean(th * th, axis=-1, keepdims=True) + EPS) * onw)
    ob_in = jnp.concatenate(parts, axis=-1) * _silu(og_ref[...].astype(F32))
    o_b = _dot(ob_in.astype(BF16), bwo_ref[...])

    y = yf_ref[...].astype(F32) + yb_ref[...].astype(F32) + dexp_ref[...] * xc_ref[...].astype(F32)
    y = y * _silu(z_ref[...].astype(F32))
    y = y * lax.rsqrt(jnp.mean(y * y, axis=-1, keepdims=True) + EPS) * cnw_ref[...]
    o_c = _dot(y.astype(BF16), cwo_ref[...])

    d = D_MODEL
    merged = (jax.nn.sigmoid(gt_ref[:, 0:d].astype(F32)) * o_a
              + jax.nn.sigmoid(gt_ref[:, d:2 * d].astype(F32)) * o_b
              + jax.nn.sigmoid(gt_ref[:, 2 * d:3 * d].astype(F32)) * o_c)
    out = _dot(merged.astype(BF16), wout_ref[...])
    o_ref[...] = x_ref[...] + mod_ref[:, 2 * d:3 * d] * out


def _merge_call(x, mod3, u, o_attn, g_f, g_b, y_f, y_b, xbc, wts, *, n_rows, row0, layer):
    tm = 512
    r0 = row0 // tm
    d = D_MODEL
    loc = lambda w: pl.BlockSpec((tm, w), lambda i: (i, 0))
    full = lambda a: pl.BlockSpec(a.shape, lambda i: (0,) * a.ndim)
    in_specs = [pl.BlockSpec((tm, d), lambda i: (i + r0, 0)),
                pl.BlockSpec((None, 1, 6 * d), lambda i: (_mod_row(i + r0, tm, layer), 0, 0)),
                loc(A_HEADS * A_V), loc(d), loc(d),
                pl.BlockSpec((tm, d), lambda i: (i + r0, U_OG // d)),
                loc(d), loc(d),
                pl.BlockSpec((tm, d), lambda i: (i, 0)),
                pl.BlockSpec((tm, d), lambda i: (i + r0, U_Z // d)),
                pl.BlockSpec((tm, 3 * d), lambda i: (i + r0, U_GATES // (3 * d)))]
    w_args = [wts[n] for n in ('awo', 'onw', 'bwo', 'dexp', 'cnw', 'cwo', 'wout')]
    in_specs += [full(a) for a in w_args]
    return pl.pallas_call(
        _merge_kernel,
        grid=(n_rows // tm,),
        in_specs=in_specs,
        out_specs=pl.BlockSpec((tm, d), lambda i: (i + r0, 0)),
        out_shape=jax.ShapeDtypeStruct((N_TOK, d), F32),
        input_output_aliases={0: 0},
        compiler_params=pltpu.CompilerParams(dimension_semantics=("parallel",)),
        name="mixer_merge",
    )(x, mod3, o_attn, g_f, g_b, u, y_f, y_b, xbc, u, u, *w_args)


MOE_TILE = 512
MOE_TOKEN_TILE = 1024
MOE_MAX_TILES = 2 * N_TOK // MOE_TILE + N_EXPERTS
MOE_ROWS = MOE_MAX_TILES * MOE_TILE
R_E1, R_E2, R_W1, R_W2, R_P1, R_P2 = 0, 1, 2, 3, 4, 5


def _first_index(vals, target):
    idx = jnp.full(target.shape, len(vals) - 1, jnp.int32)
    for k in reversed(range(len(vals) - 1)):
        idx = jnp.where(vals[k] == target, k, idx)
    return idx


def _top2_of(vals):
    m1 = functools.reduce(jnp.maximum, vals)
    i1 = _first_index(vals, m1)
    rest = [jnp.where(i1 == k, -jnp.inf, v) for k, v in enumerate(vals)]
    m2 = functools.reduce(jnp.maximum, rest)
    return m1, i1, m2, _first_index(rest, m2)


def _pick_by(vals, idx):
    out = vals[-1]
    for k in reversed(range(len(vals) - 1)):
        out = jnp.where(idx == k, vals[k], out)
    return out


def _route_kernel(x_ref, mod_ref, nw_ref, rwh_ref, rwl_ref, rb_ref, tri_ref, h_ref, route_ref, wrec_ref, cnt_ref,
                  base_scr):
    d = D_MODEL
    ng = N_EXPERT_GROUPS
    per_group = N_EXPERTS // ng

    @pl.when(pl.program_id(0) == 0)
    def _():
        base_scr[...] = jnp.zeros(base_scr.shape, F32)

    x = x_ref[...]
    tm = x.shape[0]
    y = x * lax.rsqrt(jnp.mean(x * x, axis=-1, keepdims=True) + EPS) * nw_ref[...]
    h = y * (1.0 + mod_ref[:, 4 * d:5 * d]) + mod_ref[:, 3 * d:4 * d]
    h_ref[...] = h
    h_hi, h_lo = _split2(h)
    rwh = rwh_ref[...]
    logits = _dot_nt(rwh, h_hi) + _dot_nt(rwh, h_lo) + _dot_nt(rwl_ref[...], h_hi)
    scores = jax.nn.sigmoid(logits)
    sel = scores + rb_ref[...]
    sel_k = [sel[k * ng:(k + 1) * ng] for k in range(per_group)]
    sc_k = [scores[k * ng:(k + 1) * ng] for k in range(per_group)]
    m1, i1, m2, i2 = _top2_of(sel_k)
    gsum = m1 + m2
    rows = lambda a: [a[g:g + 1] for g in range(ng)]
    gs = rows(gsum)
    best_v, best_g = gs[0], jnp.zeros((1, tm), jnp.int32)
    for g in range(1, ng):
        upd = gs[g] > best_v
        best_g = jnp.where(upd, g, best_g)
        best_v = jnp.where(upd, gs[g], best_v)
    k1 = _pick_by(rows(i1), best_g)
    k2 = _pick_by(rows(i2), best_g)
    s1 = _pick_by(rows(_pick_by(sc_k, i1)), best_g)
    s2 = _pick_by(rows(_pick_by(sc_k, i2)), best_g)
    tot = s1 + s2
    in_best = lax.broadcasted_iota(jnp.int32, (ng, tm), 0) == best_g
    hit = jnp.concatenate(
        [jnp.where(jnp.logical_and(in_best, jnp.logical_or(k1 == k, k2 == k)), 1.0, 0.0) for k in range(per_group)],
        axis=0)
    rank = _dot(hit.astype(BF16), tri_ref[...]) + base_scr[...]
    rank_k = [rank[k * ng:(k + 1) * ng] for k in range(per_group)]
    p1 = _pick_by(rows(_pick_by(rank_k, k1)), best_g)
    p2 = _pick_by(rows(_pick_by(rank_k, k2)), best_g)
    base_scr[...] += jnp.sum(hit, axis=1, keepdims=True)
    e1 = (best_g * per_group + k1).astype(F32)
    e2 = (best_g * per_group + k2).astype(F32)
    rec = jnp.concatenate([e1, e2, s1 / tot, s2 / tot, p1, p2, jnp.zeros((LANE - 6, tm), F32)], axis=0)
    route_ref[...] = rec[0:8]
    wrec_ref[...] = rec.T
    cnt_ref[...] = jnp.broadcast_to(base_scr[...], cnt_ref.shape)


def _route_call(x, mod3, nw, rwh, rwl, rb, tri, layer):
    tm = tri.shape[0]
    d = D_MODEL
    full = lambda a: pl.BlockSpec(a.shape, lambda i: (0,) * a.ndim)
    return pl.pallas_call(
        _route_kernel,
        grid=(N_TOK // tm,),
        in_specs=[pl.BlockSpec((tm, d), lambda i: (i, 0)),
                  pl.BlockSpec((None, 1, 6 * d), lambda i: (_mod_row(i, tm, layer), 0, 0)),
                  full(nw), full(rwh), full(rwl), full(rb), full(tri)],
        out_specs=[pl.BlockSpec((tm, d), lambda i: (i, 0)),
                   pl.BlockSpec((8, tm), lambda i: (0, i)),
                   pl.BlockSpec((tm, LANE), lambda i: (i, 0)),
                   pl.BlockSpec((N_EXPERTS, LANE), lambda i: (0, 0))],
        out_shape=[jax.ShapeDtypeStruct((N_TOK, d), F32),
                   jax.ShapeDtypeStruct((8, N_TOK), F32),
                   jax.ShapeDtypeStruct((N_TOK, LANE), F32),
                   jax.ShapeDtypeStruct((N_EXPERTS, LANE), F32)],
        scratch_shapes=[pltpu.VMEM((N_EXPERTS, 1), F32)],
        compiler_params=pltpu.CompilerParams(dimension_semantics=("arbitrary",)),
        name="moe_route",
    )(x, mod3, nw, rwh, rwl, rb, tri)


def _row_copy(src, src_row, dst, dst_row, sem):
    return pltpu.make_async_copy(src.at[pl.ds(src_row, 1), :], dst.at[pl.ds(dst_row, 1), :], sem)


def _dispatch_kernel(dst_ref, ztile_ref, h_ref, xs_ref, zbuf, sem):
    tm = h_ref.shape[0]

    @pl.when(pl.program_id(0) == 0)
    def _():
        zbuf[...] = jnp.zeros(zbuf.shape, F32)
        fills = []
        for e in range(N_EXPERTS):
            row = pl.multiple_of(jnp.maximum(ztile_ref[0, e], 0), MOE_TILE)
            fills.append((ztile_ref[0, e] >= 0, pltpu.make_async_copy(zbuf, xs_ref.at[pl.ds(row, MOE_TILE), :], sem)))
        for has_tile, cp in fills:
            pl.when(has_tile)(cp.start)
        for has_tile, cp in fills:
            pl.when(has_tile)(cp.wait)

        def fill_unused(t, carry):
            cp = pltpu.make_async_copy(zbuf, xs_ref.at[pl.ds(pl.multiple_of(t * MOE_TILE, MOE_TILE), MOE_TILE), :], sem)
            cp.start()
            cp.wait()
            return carry

        lax.fori_loop(ztile_ref[0, N_EXPERTS], MOE_MAX_TILES, fill_unused, 0)

    def issue(j, carry):
        for i in range(2):
            r = 2 * j + i
            _row_copy(h_ref, r, xs_ref, dst_ref[0, r], sem).start(priority=i)
            _row_copy(h_ref, r, xs_ref, dst_ref[0, tm + r], sem).start(priority=1 - i)
        return carry

    def drain(r, carry):
        _row_copy(h_ref, 0, xs_ref, 0, sem).wait()
        _row_copy(h_ref, 0, xs_ref, 0, sem).wait()
        return carry

    lax.fori_loop(0, tm // 2, issue, 0, unroll=4)
    lax.fori_loop(0, tm, drain, 0, unroll=8)


def _dispatch_call(dst, ztile, h):
    tm = dst.shape[-1] // 2
    d = D_MODEL
    return pl.pallas_call(
        _dispatch_kernel,
        grid=(N_TOK // tm,),
        in_specs=[pl.BlockSpec((None, 1, 2 * tm), lambda i: (i, 0, 0), memory_space=pltpu.SMEM),
                  pl.BlockSpec((1, N_EXPERTS + 1), lambda i: (0, 0), memory_space=pltpu.SMEM),
                  pl.BlockSpec((tm, d), lambda i: (i, 0))],
        out_specs=pl.BlockSpec(memory_space=pl.ANY),
        out_shape=jax.ShapeDtypeStruct((MOE_ROWS, d), F32),
        scratch_shapes=[pltpu.VMEM((MOE_TILE, d), F32), pltpu.SemaphoreType.DMA(())],
        compiler_params=pltpu.CompilerParams(dimension_semantics=("arbitrary",)),
        name="moe_dispatch",
    )(dst, ztile, h)


def _expert_kernel(te_ref, nt_ref, xs_ref, w1_ref, w3_ref, w2_ref, y_ref, w1b, w3b, w2b):
    t = pl.program_id(0)
    changed = jnp.logical_or(t == 0, te_ref[t] != te_ref[jnp.maximum(t - 1, 0)])

    @pl.when(changed)
    def _():
        w1b[...] = w1_ref[...].astype(BF16)
        w3b[...] = w3_ref[...].astype(BF16)
        w2b[...] = w2_ref[...].astype(BF16)

    @pl.when(t < nt_ref[0])
    def _():
        xb = xs_ref[...].astype(BF16)
        hid = _silu(_dot(xb, w1b[...])) * _dot(xb, w3b[...])
        y_ref[...] = _dot(hid.astype(BF16), w2b[...])

    @pl.when(t >= nt_ref[0])
    def _():
        y_ref[...] = jnp.zeros(y_ref.shape, F32)


def _expert_call(tile_expert, n_tiles, xs, w1, w3, w2, layer):
    d = D_MODEL
    grid_spec = pltpu.PrefetchScalarGridSpec(
        num_scalar_prefetch=2,
        grid=(MOE_MAX_TILES,),
        in_specs=[pl.BlockSpec((MOE_TILE, d), lambda t, te, nt: (jnp.minimum(t, nt[0] - 1), 0)),
                  pl.BlockSpec((None, None, d, D_EXPERT), lambda t, te, nt: (layer, te[t], 0, 0)),
                  pl.BlockSpec((None, None, d, D_EXPERT), lambda t, te, nt: (layer, te[t], 0, 0)),
                  pl.BlockSpec((None, None, D_EXPERT, d), lambda t, te, nt: (layer, te[t], 0, 0))],
        out_specs=pl.BlockSpec((MOE_TILE, d), lambda t, te, nt: (t, 0)),
        scratch_shapes=[pltpu.VMEM((d, D_EXPERT), BF16), pltpu.VMEM((d, D_EXPERT), BF16),
                        pltpu.VMEM((D_EXPERT, d), BF16)])
    return pl.pallas_call(
        _expert_kernel,
        grid_spec=grid_spec,
        out_shape=jax.ShapeDtypeStruct((MOE_ROWS, d), F32),
        compiler_params=pltpu.CompilerParams(dimension_semantics=("arbitrary",)),
        name="moe_experts",
    )(tile_expert, n_tiles, xs, w1, w3, w2)


def _combine_kernel(dst_ref, x_ref, mod_ref, route_ref, y_ref, *rest, split):
    o_refs, (b1, b2, sem) = rest[:-3], rest[-3:]
    tm = x_ref.shape[0]
    d = D_MODEL

    def issue(j, carry):
        for i in range(2):
            r = 2 * j + i
            _row_copy(y_ref, dst_ref[0, r], b1, r, sem).start(priority=i)
            _row_copy(y_ref, dst_ref[0, tm + r], b2, r, sem).start(priority=1 - i)
        return carry

    def drain(r, carry):
        _row_copy(y_ref, 0, b1, 0, sem).wait()
        _row_copy(y_ref, 0, b2, 0, sem).wait()
        return carry

    lax.fori_loop(0, tm // 2, issue, 0, unroll=4)
    lax.fori_loop(0, tm, drain, 0, unroll=8)
    rec = route_ref[...]
    moe = rec[:, R_W1:R_W1 + 1] * b1[...] + rec[:, R_W2:R_W2 + 1] * b2[...]
    out = x_ref[...] + mod_ref[:, 5 * d:6 * d] * moe
    if split:
        is_ctx = pl.program_id(0) < N_CTX // tm

        @pl.when(is_ctx)
        def _():
            o_refs[0][...] = out

        @pl.when(jnp.logical_not(is_ctx))
        def _():
            o_refs[1][...] = out
    else:
        o_refs[0][...] = out


def _combine_call(dst, x, mod3, route, y, layer, split):
    tm = dst.shape[-1] // 2
    d = D_MODEL
    n_ctx = N_CTX // tm
    if split:
        out_specs = [pl.BlockSpec((tm, d), lambda i: (jnp.minimum(i, n_ctx - 1), 0)),
                     pl.BlockSpec((tm, d), lambda i: (jnp.maximum(i - n_ctx, 0), 0))]
        out_shape = [jax.ShapeDtypeStruct((N_CTX, d), F32), jax.ShapeDtypeStruct((N_DEC, d), F32)]
    else:
        out_specs = pl.BlockSpec((tm, d), lambda i: (i, 0))
        out_shape = jax.ShapeDtypeStruct((N_TOK, d), F32)
    return pl.pallas_call(
        functools.partial(_combine_kernel, split=split),
        grid=(N_TOK // tm,),
        in_specs=[pl.BlockSpec((None, 1, 2 * tm), lambda i: (i, 0, 0), memory_space=pltpu.SMEM),
                  pl.BlockSpec((tm, d), lambda i: (i, 0)),
                  pl.BlockSpec((None, 1, 6 * d), lambda i: (_mod_row(i, tm, layer), 0, 0)),
                  pl.BlockSpec((tm, LANE), lambda i: (i, 0)),
                  pl.BlockSpec(memory_space=pl.ANY)],
        out_specs=out_specs,
        out_shape=out_shape,
        scratch_shapes=[pltpu.VMEM((tm, d), F32), pltpu.VMEM((tm, d), F32), pltpu.SemaphoreType.DMA(())],
        compiler_params=pltpu.CompilerParams(dimension_semantics=("arbitrary",)),
        name="moe_combine",
    )(dst, x, mod3, route, y)


def _moe_routed(x, mod3, nw, rwh, rwl, rb, tri, w1, w3, w2, layer, split):
    h, route, wrec, cnt = _route_call(x, mod3, nw, rwh, rwl, rb, tri, layer)
    per_group = N_EXPERTS // N_EXPERT_GROUPS
    counts = cnt[:, 0].reshape(per_group, N_EXPERT_GROUPS).T.reshape(N_EXPERTS).astype(jnp.int32)
    ntile = (counts + MOE_TILE - 1) // MOE_TILE
    tend = jnp.cumsum(ntile)
    row0 = (tend - ntile) * MOE_TILE
    n_tiles = tend[-1:]
    tids = jnp.arange(MOE_MAX_TILES)
    tile_expert = jnp.minimum(jnp.sum(tend[None, :] <= jnp.minimum(tids, n_tiles[0] - 1)[:, None], axis=1),
                              N_EXPERTS - 1).astype(jnp.int32)
    ztile = jnp.concatenate([jnp.where(ntile > 0, row0 + (ntile - 1) * MOE_TILE, -1), n_tiles])
    ztile = ztile.astype(jnp.int32).reshape(1, N_EXPERTS + 1)
    eids = jnp.arange(N_EXPERTS, dtype=F32)
    row0f = row0.astype(F32)

    def dest(e, p):
        return (jnp.sum(jnp.where(e[:, None] == eids[None, :], row0f[None, :], 0.0), axis=1) + p).astype(jnp.int32)

    tm = MOE_TOKEN_TILE
    dst = jnp.concatenate([dest(route[R_E1], route[R_P1]).reshape(N_TOK // tm, 1, tm),
                           dest(route[R_E2], route[R_P2]).reshape(N_TOK // tm, 1, tm)], axis=-1)
    xs = _dispatch_call(dst, ztile, h)
    y = _expert_call(tile_expert, n_tiles.astype(jnp.int32), xs, w1, w3, w2, layer)
    return _combine_call(dst, x, mod3, wrec, y, layer, split)


def _pack_w_in(w):
    p = _col_slices(w)
    z = lambda n: jnp.zeros((D_MODEL, n), w.dtype)
    main = jnp.concatenate([
        p['a_cq'], p['a_ckv'], _head_slot(jnp.concatenate([z(A_NOPE), p['a_krope']], axis=1)), z(U_QK - U_KR - HEAD_PAD),
        p['b_q'], p['b_k'], p['b_v'], p['b_og'], p['c_z'], p['c_xbc'][:, :C_INNER], p['gates'],
        p['c_xbc'][:, C_INNER:]], axis=1)
    small = jnp.concatenate([p['b_gf'], p['b_gb'], p['c_dtf'], p['c_dtb'], p['a_krope'],
                             z(LANE - S_KR - A_ROPE)], axis=1)
    return main.astype(BF16), small.astype(BF16)


def _pair_slots(v):
    z = jnp.zeros_like(v)
    odd = (np.arange(A_HEADS) % 2 == 1)[None, :, None]
    return jnp.concatenate([jnp.where(odd, z, v), jnp.where(odd, v, z)], axis=-1).reshape(v.shape[0], A_HEADS * HEAD_PAD)


def _head_slot(t):
    t = t[..., _SLOT_SRC]
    return jnp.concatenate([t, jnp.zeros(t.shape[:-1] + (HEAD_PAD - A_QK,), t.dtype)], axis=-1)


def _qk_gain(g, scale):
    return (_head_slot(g) * scale).reshape(1, HEAD_PAD)


def _lane_slot(vals, off):
    return jnp.zeros((1, LANE), F32).at[0, off:off + vals.shape[0]].set(vals.astype(F32))


def _rope_tables():
    t = jnp.arange(DEC_SEQ)
    row = (t // GRID_W).astype(F32)
    col = (t % GRID_W).astype(F32)
    n_freq = A_ROPE // 4
    inv_freq = 1.0 / (ROPE_BASE ** (jnp.arange(n_freq, dtype=F32) / n_freq))
    ang = jnp.concatenate([row[:, None] * inv_freq, col[:, None] * inv_freq], axis=-1)
    cos, sin = jnp.cos(ang), jnp.sin(ang)
    ones = lambda n: jnp.ones((DEC_SEQ, n), F32)
    zeros = lambda n: jnp.zeros((DEC_SEQ, n), F32)
    n_hi = HEAD_PAD // 2 - A_ROPE // 2
    ctab = jnp.concatenate([cos, ones(_N_LOW), cos, ones(n_hi)], axis=-1)
    stab = jnp.concatenate([-sin, zeros(_N_LOW), sin, zeros(n_hi)], axis=-1)
    return ctab, stab


def kernel(x_prompt, x_sample, cache_ckv, cache_krope, state_gla, state_ssd, c, c_ctx, w_ada, b_ada, norm1, norm2, w_in, a_q_norm, a_wq, a_kv_norm, a_wkv, a_qk_qnorm, a_qk_knorm, a_wo, b_wg, b_bg, b_onorm, b_wo, c_conv_w, c_conv_b, c_dt_bias, c_A_log, c_D, c_norm, c_wo, w_out, router_w, router_bias, e_w1, e_w3, e_w2):
    d = D_MODEL
    x = jnp.concatenate([x_prompt.reshape(N_CTX, d), x_sample.reshape(N_DEC, d)], axis=0)

    cvecs = jnp.zeros((8, d), F32).at[0].set(c_ctx).at[1:1 + DEC_BATCH].set(c)
    mod3 = _mod_call(cvecs, w_ada, b_ada).reshape(DEPTH * 8, 1, 6 * d)

    rope_tabs = _rope_tables()
    gla_np = (_gla_consts(False), _gla_consts(True))
    gla_mall = jnp.asarray(np.stack([gla_np[0][0], gla_np[1][0]]), BF16)
    gla_masks = jnp.asarray(np.stack([gla_np[0][1], gla_np[1][1]]))
    ssd_consts = _ssd_consts()
    per_group = N_EXPERTS // N_EXPERT_GROUPS
    k_major = np.arange(N_EXPERTS).reshape(N_EXPERT_GROUPS, per_group).T.reshape(-1)
    rwh, rwl = _split2(router_w.T[k_major])
    rb = router_bias.astype(F32)[k_major].reshape(N_EXPERTS, 1)
    ridx = np.arange(1024)
    route_tri = jnp.asarray(ridx[:, None] < ridx[None, :], BF16)

    ckvs, kropes, glas, ssds = [], [], [], []
    for l in range(DEPTH):
        w_main, w_small = _pack_w_in(w_in[l])
        u, s_all = _norm_proj_call(x, mod3, norm1[l].reshape(1, d), w_main, w_small, l)
        kropes.append(s_all[:N_CTX, S_KR:S_KR + A_ROPE].reshape(BATCH, SEQ, A_ROPE))

        pw = A_HEADS * HEAD_PAD
        wq = _head_slot(a_wq[l].reshape(A_QLORA, A_HEADS, A_QK)).reshape(A_QLORA, pw)
        wkv = a_wkv[l].reshape(A_KVLORA, A_HEADS, A_NOPE + A_V)
        wkn = jnp.concatenate([wkv[..., :A_NOPE], jnp.zeros((A_KVLORA, A_HEADS, A_ROPE), wkv.dtype)], axis=-1)
        aw = dict(qnw=a_q_norm[l].reshape(1, A_QLORA), wq=wq.astype(BF16),
                  gq=_qk_gain(a_qk_qnorm[l], A_QK ** -0.5),
                  kvnw=a_kv_norm[l].reshape(1, A_KVLORA),
                  wkn=_head_slot(wkn).reshape(A_KVLORA, pw).astype(BF16),
                  wv=_pair_slots(wkv[..., A_NOPE:]).astype(BF16),
                  gk=_qk_gain(a_qk_knorm[l], 1.0))
        q_c, k_c, v_c, ckvn = _attn_prep_call(N_CTX, 0, (u,), aw, has_q=True, norm_kv=True, rope=False, emit_ckv=True)
        o_c = _attn_call(q_c, k_c, v_c, None, n_seq=BATCH, seq_len=SEQ, hps=A_HEADS, tq=SEQ)
        ckvs.append(ckvn.reshape(BATCH, SEQ, A_KVLORA))
        q_d, k_d, v_d = _attn_prep_call(N_DEC, N_CTX, (u,), aw, has_q=True, norm_kv=True, rope=True, emit_ckv=False,
                                        tables=rope_tabs)
        kr_x = _head_slot(jnp.concatenate([jnp.zeros((DEC_BATCH * PAST_LEN, A_NOPE), F32),
                                            cache_krope[:, l].reshape(DEC_BATCH * PAST_LEN, A_ROPE)], axis=1))
        k_x, v_x = _attn_prep_call(DEC_BATCH * PAST_LEN, 0, (cache_ckv[:, l].reshape(DEC_BATCH * PAST_LEN, A_KVLORA), kr_x),
                                   aw, has_q=False, norm_kv=False, rope=False, emit_ckv=False)
        o_d = _attn_call(q_d, k_d, v_d, (k_x, v_x), n_seq=DEC_BATCH, seq_len=DEC_SEQ, hps=8, tq=512)

        hk = B_HEADS * B_DK
        wg = (jnp.zeros((2, LANE, hk), F32).at[0, S_GF:S_GF + B_GATE_RANK].set(b_wg[l, 0])
              .at[1, S_GB:S_GB + B_GATE_RANK].set(b_wg[l, 1]))
        wgh, wgl = _split2(wg)
        bg = b_bg[l].reshape(2, 1, hk)
        last = l == DEPTH - 1
        g_fc, g_bc, st_gc = _gla_call(u, s_all, wgh, wgl, bg, gla_mall, gla_masks, None, n_seq=BATCH, seq_len=SEQ, row0=0,
                                      prev=jnp.stack(glas, axis=1) if last and glas else None)
        g_fd, g_bd, _ = _gla_call(u, s_all, wgh, wgl, bg, gla_mall, gla_masks, jnp.swapaxes(state_gla[:, l], -1, -2),
                                  n_seq=DEC_BATCH, seq_len=DEC_SEQ, row0=N_CTX)
        glas.append(st_gc)

        w8 = jnp.zeros((8, C_XBC), F32).at[:C_CONV].set(c_conv_w[l])
        cb = c_conv_b[l].reshape(1, C_XBC)
        xbc_c = _conv_call(u, w8, cb, n_seq=BATCH, seq_len=SEQ, row0=0)
        xbc_d = _conv_call(u, w8, cb, n_seq=DEC_BATCH, seq_len=DEC_SEQ, row0=N_CTX)
        dtb = jnp.stack([_lane_slot(c_dt_bias[l, 0], S_DTF), _lane_slot(c_dt_bias[l, 1], S_DTB)])
        a_neg = -jnp.exp(c_A_log[l].astype(F32))
        aneg = jnp.stack([_lane_slot(a_neg[0], S_DTF), _lane_slot(a_neg[1], S_DTB)])
        y_fc, y_bc, st_c = _ssd_call(xbc_c, s_all, dtb, aneg, ssd_consts, None, n_seq=BATCH, seq_len=SEQ, row0=0,
                                     prev=jnp.stack(ssds, axis=1) if last and ssds else None)
        st0 = state_ssd[:, l].reshape(DEC_BATCH, 2, C_HEADS // 2, LANE, C_STATE)
        y_fd, y_bd, _ = _ssd_call(xbc_d, s_all, dtb, aneg, ssd_consts, st0, n_seq=DEC_BATCH, seq_len=DEC_SEQ, row0=N_CTX)
        ssds.append(st_c)

        mw = dict(awo=a_wo[l].astype(BF16), onw=b_onorm[l].reshape(1, B_DV),
                  bwo=b_wo[l].astype(BF16), dexp=jnp.repeat(c_D[l], C_HEADDIM).reshape(1, C_INNER),
                  cnw=c_norm[l].reshape(1, C_INNER), cwo=c_wo[l].astype(BF16), wout=w_out[l].astype(BF16))
        x = _merge_call(x, mod3, u, o_c, g_fc, g_bc, y_fc, y_bc, xbc_c, mw, n_rows=N_CTX, row0=0, layer=l)
        x = _merge_call(x, mod3, u, o_d, g_fd, g_bd, y_fd, y_bd, xbc_d, mw, n_rows=N_DEC, row0=N_CTX, layer=l)

        x = _moe_routed(x, mod3, norm2[l].reshape(1, d), rwh, rwl, rb, route_tri, e_w1, e_w3, e_w2, l, split=last)

    y_prompt = x[0].reshape(BATCH, SEQ, d)
    y_sample = x[1].reshape(DEC_BATCH, DEC_SEQ, d)
    new_ckv = jnp.stack(ckvs, axis=1)
    new_krope = jnp.stack(kropes, axis=1)
    new_state_gla = glas[-1] if DEPTH > 1 else jnp.swapaxes(glas[-1], -1, -2)[:, None]
    new_state_ssd = (ssds[-1] if DEPTH > 1 else ssds[-1][:, None]).reshape(BATCH, DEPTH, 2, C_HEADS, C_HEADDIM, C_STATE)
    return (y_prompt, y_sample, new_ckv, new_krope, new_state_gla, new_state_ssd)
```

```python
import functools
import math

import numpy as np
import jax
import jax.numpy as jnp
from jax import lax
from jax.experimental import pallas as pl
from jax.experimental.pallas import tpu as pltpu

F32 = jnp.float32
BF16 = jnp.bfloat16

D_MODEL = 1024
BATCH = 16
SEQ = 256
DEPTH = 2
DEC_BATCH = 2
DEC_SEQ = 2048
PAST_LEN = 512
GRID_W = 64
EPS = 1e-6
ROPE_BASE = 10000.0

A_HEADS = 16
A_NOPE = 64
A_ROPE = 32
A_QK = A_NOPE + A_ROPE
A_V = 64
A_QLORA = 512
A_KVLORA = 256

B_HEADS = 4
B_DK = 128
B_DV = 256
B_GATE_RANK = 16
B_GATE_NORM = 16.0

C_HEADS = 16
C_HEADDIM = 64
C_INNER = C_HEADS * C_HEADDIM
C_GROUPS = 2
C_STATE = 128
C_XBC = C_INNER + 2 * C_GROUPS * C_STATE
C_CONV = 5

N_EXPERTS = 16
N_EXPERT_GROUPS = 4
D_EXPERT = 512

N_CTX = BATCH * SEQ
N_DEC = DEC_BATCH * DEC_SEQ
N_TOK = N_CTX + N_DEC

LANE = 128
HEAD_PAD = 128
CHUNK = 128
GLA_CHUNK = 256

U_CQ = 0
U_CKV = 512
U_KR = 768
U_QK = 1024
U_V = 2048
U_OG = 3072
U_Z = 4096
U_X = 5120
U_GATES = 6144
U_BC = 9216
U_WIDTH = 9728
S_GF, S_GB, S_DTF, S_DTB, S_KR = 0, 16, 32, 48, 64

_IN_SPLITS = (
    ('a_cq', A_QLORA), ('a_ckv', A_KVLORA), ('a_krope', A_ROPE),
    ('b_q', B_HEADS * B_DK), ('b_k', B_HEADS * B_DK), ('b_v', B_HEADS * B_DV),
    ('b_og', B_HEADS * B_DV), ('b_gf', B_GATE_RANK), ('b_gb', B_GATE_RANK),
    ('c_z', C_INNER), ('c_xbc', C_XBC), ('c_dtf', C_HEADS), ('c_dtb', C_HEADS),
    ('gates', 3 * D_MODEL),
)
_N_LOW = HEAD_PAD // 2 - A_ROPE // 2
_SLOT_SRC = np.concatenate([A_NOPE + np.arange(0, A_ROPE, 2), np.arange(0, _N_LOW),
                            A_NOPE + np.arange(1, A_ROPE, 2), np.arange(_N_LOW, A_NOPE)])


def _col_slices(w):
    parts, start = {}, 0
    for name, size in _IN_SPLITS:
        parts[name] = w[:, start:start + size]
        start += size
    return parts


def _dot(a, b):
    return jnp.dot(a, b, preferred_element_type=F32)


def _dot_nt(a, b):
    return lax.dot_general(a, b, (((1,), (1,)), ((), ())), preferred_element_type=F32)


def _dot_tn(a, b):
    return lax.dot_general(a, b, (((0,), (0,)), ((), ())), preferred_element_type=F32)


def _split2(x):
    hi = x.astype(BF16)
    lo = (x - hi.astype(F32)).astype(BF16)
    return hi, lo


def _silu(x):
    return x * jax.nn.sigmoid(x)


def _softplus(x):
    return jnp.maximum(x, 0.0) + jnp.log1p(jnp.exp(-jnp.abs(x)))


def _log_sigmoid(x):
    return jnp.minimum(x, 0.0) - jnp.log1p(jnp.exp(-jnp.abs(x)))


def _mod_kernel(c_ref, w_ref, b_ref, o_ref):
    @pl.when(pl.program_id(1) == 0)
    def _():
        o_ref[...] = jnp.broadcast_to(b_ref[...], o_ref.shape)

    s = _silu(c_ref[...]).astype(BF16)
    o_ref[...] += _dot(s, w_ref[...].astype(BF16))


def _mod_call(cvecs, w_ada, b_ada):
    tk = 256
    return pl.pallas_call(
        _mod_kernel,
        grid=(DEPTH, D_MODEL // tk),
        in_specs=[pl.BlockSpec((8, tk), lambda l, k: (0, k)),
                  pl.BlockSpec((None, tk, 6 * D_MODEL), lambda l, k: (l, k, 0)),
                  pl.BlockSpec((None, 1, 6 * D_MODEL), lambda l, k: (l, 0, 0))],
        out_specs=pl.BlockSpec((None, 8, 6 * D_MODEL), lambda l, k: (l, 0, 0)),
        out_shape=jax.ShapeDtypeStruct((DEPTH, 8, 6 * D_MODEL), F32),
        compiler_params=pltpu.CompilerParams(dimension_semantics=("parallel", "arbitrary")),
        name="adaln_mod",
    )(cvecs, w_ada, b_ada.reshape(DEPTH, 1, 6 * D_MODEL))


def _mod_row(i, tm, layer):
    n_ctx = N_CTX // tm
    per_b = DEC_SEQ // tm
    return layer * 8 + jnp.where(i < n_ctx, 0, 1 + (i - n_ctx) // per_b)


def _norm_proj_kernel(x_ref, mod_ref, nw_ref, w_ref, ws_ref, u_ref, s_ref, h_scr):
    @pl.when(pl.program_id(1) == 0)
    def _():
        x = x_ref[...]
        m = mod_ref[...]
        y = x * lax.rsqrt(jnp.mean(x * x, axis=-1, keepdims=True) + EPS) * nw_ref[...]
        h = (y * (1.0 + m[:, D_MODEL:2 * D_MODEL]) + m[:, 0:D_MODEL]).astype(BF16)
        h_scr[...] = h
        s_ref[...] = _dot(h, ws_ref[...])

    u_ref[...] = _dot(h_scr[...], w_ref[...]).astype(BF16)


def _norm_proj_call(x, mod3, nw, w_main, w_small, layer):
    tm, tn = 2048, 512
    return pl.pallas_call(
        _norm_proj_kernel,
        grid=(N_TOK // tm, U_WIDTH // tn),
        in_specs=[pl.BlockSpec((tm, D_MODEL), lambda i, j: (i, 0)),
                  pl.BlockSpec((None, 1, 6 * D_MODEL), lambda i, j: (_mod_row(i, tm, layer), 0, 0)),
                  pl.BlockSpec((1, D_MODEL), lambda i, j: (0, 0)),
                  pl.BlockSpec((D_MODEL, tn), lambda i, j: (0, j)),
                  pl.BlockSpec((D_MODEL, LANE), lambda i, j: (0, 0))],
        out_specs=[pl.BlockSpec((tm, tn), lambda i, j: (i, j)),
                   pl.BlockSpec((tm, LANE), lambda i, j: (i, 0))],
        out_shape=[jax.ShapeDtypeStruct((N_TOK, U_WIDTH), BF16),
                   jax.ShapeDtypeStruct((N_TOK, LANE), F32)],
        scratch_shapes=[pltpu.VMEM((tm, D_MODEL), BF16)],
        compiler_params=pltpu.CompilerParams(dimension_semantics=("parallel", "arbitrary")),
        name="norm_in_proj",
    )(x, mod3, nw, w_main, w_small)


def _rope(t, cos, sin):
    return t * cos + pltpu.roll(t, HEAD_PAD // 2, 1) * sin


def _head_scale(ss):
    return lax.rsqrt(ss * (1.0 / A_QK) + EPS)


def _attn_prep_kernel(*refs, has_q, norm_kv, rope, emit_ckv):
    refs = list(refs)
    cq_ref = refs.pop(0) if has_q else None
    ckv_ref, kr_ref = refs.pop(0), refs.pop(0)
    if has_q:
        qnw_ref, wq_ref, gq_ref = refs.pop(0), refs.pop(0), refs.pop(0)
    kvnw_ref, wkn_ref, wv_ref, gk_ref = refs.pop(0), refs.pop(0), refs.pop(0), refs.pop(0)
    cos = sin = None
    if rope:
        cos, sin = refs.pop(0)[...], refs.pop(0)[...]
    q_ref = refs.pop(0) if has_q else None
    k_ref, v_ref = refs.pop(0), refs.pop(0)
    ckvn_ref = refs.pop(0) if emit_ckv else None

    ckv = ckv_ref[...].astype(F32)
    if norm_kv:
        ckv = ckv * lax.rsqrt(jnp.mean(ckv * ckv, axis=-1, keepdims=True) + EPS) * kvnw_ref[...]
    if emit_ckv:
        ckvn_ref[...] = ckv
    ckv_b = ckv.astype(BF16)
    v_ref[...] = _dot(ckv_b, wv_ref[...]).astype(BF16)
    kf = _dot(ckv_b, wkn_ref[...])
    kr = kr_ref[...].astype(F32)
    gk = gk_ref[...]
    ss_kr = jnp.sum(kr * kr, axis=-1, keepdims=True)
    krg = kr * gk
    if rope:
        krg = _rope(krg, cos, sin)
    for h in range(A_HEADS):
        sl = slice(h * HEAD_PAD, (h + 1) * HEAD_PAD)
        kn = kf[:, sl]
        scale = _head_scale(jnp.sum(kn * kn, axis=-1, keepdims=True) + ss_kr)
        k_ref[:, sl] = ((kn * gk + krg) * scale).astype(BF16)

    if has_q:
        cq = cq_ref[...].astype(F32)
        cq = cq * lax.rsqrt(jnp.mean(cq * cq, axis=-1, keepdims=True) + EPS) * qnw_ref[...]
        qf = _dot(cq.astype(BF16), wq_ref[...])
        gq = gq_ref[...]
        for h in range(A_HEADS):
            sl = slice(h * HEAD_PAD, (h + 1) * HEAD_PAD)
            t = qf[:, sl]
            t = t * _head_scale(jnp.sum(t * t, axis=-1, keepdims=True)) * gq
            q_ref[:, sl] = (_rope(t, cos, sin) if rope else t).astype(BF16)


def _attn_prep_call(n_rows, row0, srcs, wts, *, has_q, norm_kv, rope, emit_ckv, tables=None):
    tm = 512
    r0 = row0 // tm
    pw = A_HEADS * HEAD_PAD
    full = lambda shape: pl.BlockSpec(shape, lambda i: (0,) * len(shape))
    args, in_specs = [], []
    if len(srcs) == 1:
        u = srcs[0]
        if has_q:
            args.append(u)
            in_specs.append(pl.BlockSpec((tm, A_QLORA), lambda i: (i + r0, U_CQ // A_QLORA)))
        args += [u, u]
        in_specs += [pl.BlockSpec((tm, A_KVLORA), lambda i: (i + r0, U_CKV // A_KVLORA)),
                     pl.BlockSpec((tm, LANE), lambda i: (i + r0, U_KR // LANE))]
    else:
        args += list(srcs)
        in_specs += [pl.BlockSpec((tm, A_KVLORA), lambda i: (i, 0)),
                     pl.BlockSpec((tm, LANE), lambda i: (i, 0))]
    if has_q:
        args += [wts['qnw'], wts['wq'], wts['gq']]
        in_specs += [full((1, A_QLORA)), full((A_QLORA, pw)), full((1, HEAD_PAD))]
    args += [wts['kvnw'], wts['wkn'], wts['wv'], wts['gk']]
    in_specs += [full((1, A_KVLORA)), full((A_KVLORA, pw)), full((A_KVLORA, pw)), full((1, HEAD_PAD))]
    if rope:
        per_seq = DEC_SEQ // tm
        args += list(tables)
        in_specs += [pl.BlockSpec((tm, HEAD_PAD), lambda i: (i % per_seq, 0))] * 2
    out_specs, out_shape = [], []
    for _ in range((1 if has_q else 0) + 2):
        out_specs.append(pl.BlockSpec((tm, pw), lambda i: (i, 0)))
        out_shape.append(jax.ShapeDtypeStruct((n_rows, pw), BF16))
    if emit_ckv:
        out_specs.append(pl.BlockSpec((tm, A_KVLORA), lambda i: (i, 0)))
        out_shape.append(jax.ShapeDtypeStruct((n_rows, A_KVLORA), F32))
    return pl.pallas_call(
        functools.partial(_attn_prep_kernel, has_q=has_q, norm_kv=norm_kv, rope=rope, emit_ckv=emit_ckv),
        grid=(n_rows // tm,),
        in_specs=in_specs, out_specs=out_specs, out_shape=out_shape,
        compiler_params=pltpu.CompilerParams(dimension_semantics=("parallel",)),
        name="mla_prep",
    )(*args)


def _attn_kernel(*refs, hps, has_ctx):
    if has_ctx:
        q_ref, k_ref, v_ref, kc_ref, vc_ref, o_ref = refs
    else:
        q_ref, k_ref, v_ref, o_ref = refs

    def scores(hh):
        sl = slice(hh * HEAD_PAD, (hh + 1) * HEAD_PAD)
        q = q_ref[:, sl]
        return _dot_nt(q, k_ref[:, sl]), (_dot_nt(q, kc_ref[:, sl]) if has_ctx else None)

    nxt = scores(0)
    even = None
    for hh in range(hps):
        sl = slice(hh * HEAD_PAD, (hh + 1) * HEAD_PAD)
        s, s2 = nxt
        if hh + 1 < hps:
            nxt = scores(hh + 1)
        m = jnp.max(s, axis=-1, keepdims=True)
        if has_ctx:
            m = jnp.maximum(m, jnp.max(s2, axis=-1, keepdims=True))
        p = jnp.exp(s - m)
        den = jnp.sum(p, axis=-1, keepdims=True)
        o = _dot(p.astype(BF16), v_ref[:, sl])
        if has_ctx:
            p2 = jnp.exp(s2 - m)
            den = den + jnp.sum(p2, axis=-1, keepdims=True)
            o = o + _dot(p2.astype(BF16), vc_ref[:, sl])
        o = o / den
        if hh % 2 == 0:
            even = o
        else:
            pair = hh // 2
            o_ref[:, pair * LANE:(pair + 1) * LANE] = (even + o).astype(BF16)


def _attn_call(q, k, v, ctx_kv, *, n_seq, seq_len, hps, tq):
    pw = A_HEADS * HEAD_PAD
    bw = hps * HEAD_PAD
    nq = seq_len // tq
    in_specs = [pl.BlockSpec((tq, bw), lambda b, h, i: (b * nq + i, h)),
                pl.BlockSpec((seq_len, bw), lambda b, h, i: (b, h)),
                pl.BlockSpec((seq_len, bw), lambda b, h, i: (b, h))]
    args = [q, k, v]
    if ctx_kv is not None:
        in_specs += [pl.BlockSpec((PAST_LEN, bw), lambda b, h, i: (b, h))] * 2
        args += list(ctx_kv)
    return pl.pallas_call(
        functools.partial(_attn_kernel, hps=hps, has_ctx=ctx_kv is not None),
        grid=(n_seq, A_HEADS // hps, nq),
        in_specs=in_specs,
        out_specs=pl.BlockSpec((tq, hps * A_V), lambda b, h, i: (b * nq + i, h)),
        out_shape=jax.ShapeDtypeStruct((n_seq * seq_len, A_HEADS * A_V), BF16),
        compiler_params=pltpu.CompilerParams(dimension_semantics=("parallel", "parallel", "arbitrary")),
        name="mla_attention",
    )(*args)


def _conv_kernel(x_ref, w_ref, b_ref, o_ref):
    x = x_ref[...].astype(F32)
    n = x.shape[0]
    row = lax.broadcasted_iota(jnp.int32, x.shape, 0)
    half = (C_CONV - 1) // 2
    acc = x * w_ref[half:half + 1, :] + b_ref[...]
    for d in range(-half, half + 1):
        if d == 0:
            continue
        shifted = pltpu.roll(x, (-d) % n, 0)
        valid = jnp.logical_and(row + d >= 0, row + d < n)
        acc = acc + jnp.where(valid, shifted, 0.0) * w_ref[half + d:half + d + 1, :]
    o_ref[...] = _silu(acc).astype(BF16)


def _conv_call(u, w8, b, *, n_seq, seq_len, row0):
    tc = 512
    b0 = row0 // seq_len
    nx = C_INNER // tc
    ucol = lambda j: jnp.where(j < nx, U_X // tc + j, U_BC // tc + j - nx)
    return pl.pallas_call(
        _conv_kernel,
        grid=(n_seq, C_XBC // tc),
        in_specs=[pl.BlockSpec((seq_len, tc), lambda s, j: (s + b0, ucol(j))),
                  pl.BlockSpec((8, tc), lambda s, j: (0, j)),
                  pl.BlockSpec((1, tc), lambda s, j: (0, j))],
        out_specs=pl.BlockSpec((seq_len, tc), lambda s, j: (s, j)),
        out_shape=jax.ShapeDtypeStruct((n_seq * seq_len, C_XBC), BF16),
        compiler_params=pltpu.CompilerParams(dimension_semantics=("parallel", "parallel")),
        name="ssd_conv",
    )(u, w8, b)


def _ssd_consts():
    idx = np.arange(CHUNK)
    tri = np.stack([idx[None, :] <= idx[:, None], idx[None, :] >= idx[:, None]]).astype(np.float32)
    expand = np.zeros((2, LANE, C_INNER), np.float32)
    tile = np.zeros((2, LANE, C_HEADS * LANE), np.float32)
    for di, off in enumerate((S_DTF, S_DTB)):
        for h in range(C_HEADS):
            expand[di, off + h, h * C_HEADDIM:(h + 1) * C_HEADDIM] = 1.0
            tile[di, off + h, h * LANE:(h + 1) * LANE] = 1.0
    return jnp.asarray(tri, BF16), jnp.asarray(expand, BF16), jnp.asarray(tile, BF16)


def _ssd_kernel(*refs, has_init, n_prev):
    refs = list(refs)
    xs = [(refs.pop(0), refs.pop(0), refs.pop(0)) for _ in range(2)]
    dtb_ref, an_ref, tri_ref, exp_ref, tile_ref = (refs.pop(0) for _ in range(5))
    st0_ref = refs.pop(0) if has_init else None
    prev_ref = refs.pop(0) if n_prev else None
    y_refs = (refs.pop(0), refs.pop(0))
    stf_ref, st_scr = refs

    @pl.when(pl.program_id(1) == 0)
    def _():
        st_scr[...] = st0_ref[...] if has_init else jnp.zeros(st_scr.shape, F32)

    t = CHUNK
    ri = lax.broadcasted_iota(jnp.int32, (t, t), 0)
    ci = lax.broadcasted_iota(jnp.int32, (t, t), 1)
    prow = lax.broadcasted_iota(jnp.int32, (LANE, C_STATE), 0)
    gs = C_GROUPS * C_STATE
    pairs_per_group = C_HEADS // C_GROUPS // 2
    for di in range(2):
        x_ref, bc_ref, s_ref = xs[di]
        y_ref = y_refs[di]
        off = (S_DTF, S_DTB)[di]
        dt = _softplus(s_ref[...] + dtb_ref[di])
        a = dt * an_ref[di]
        a_hi, a_lo = _split2(a)
        tri = tri_ref[di]
        acum = _dot(tri, a_hi) + _dot(tri, a_lo)
        acum_t = acum.T
        dt_t = dt.T
        edge = 0 if di else t - 1
        atot = acum[edge:edge + 1, :]
        causal = (ci >= ri) if di else (ci <= ri)
        dec_all = jnp.exp(atot)
        e_exp = _dot(jnp.exp(acum).astype(BF16), exp_ref[di])
        wj_exp = _dot((jnp.exp(atot - acum) * dt).astype(BF16), exp_ref[di])
        c_hi, c_lo = _split2(acum)
        acol = _dot(c_hi, tile_ref[di]) + _dot(c_lo, tile_ref[di])
        bc = bc_ref[...]
        cb, bmat, cmat = [], [], []
        for g in range(C_GROUPS):
            bmat.append(bc[:, g * C_STATE:(g + 1) * C_STATE])
            cmat.append(bc[:, gs + g * C_STATE:gs + (g + 1) * C_STATE])
            cb.append(_dot_nt(cmat[g], bmat[g]))
        lane = lax.broadcasted_iota(jnp.int32, (t, LANE), 1)
        low = lane < C_HEADDIM
        for m in range(C_HEADS // 2):
            g = m // pairs_per_group
            ps = slice(m * LANE, (m + 1) * LANE)
            xp = x_ref[:, ps]
            y_diag = []
            for hh in range(2):
                h = 2 * m + hh
                la = off + h
                lm = jnp.exp(jnp.where(causal, acol[:, h * LANE:(h + 1) * LANE] - acum_t[la:la + 1, :], -jnp.inf))
                w = (cb[g] * lm * dt_t[la:la + 1, :]).astype(BF16)
                y_diag.append(_dot(w, xp))
            st = st_scr[di, m]
            y_off = _dot_nt(cmat[g], st.astype(BF16)) * e_exp[:, ps]
            y_ref[:, ps] = (jnp.where(low, y_diag[0], y_diag[1]) + y_off).astype(BF16)
            xw = (xp.astype(F32) * wj_exp[:, ps]).astype(BF16)
            la = off + 2 * m
            dec = jnp.where(prow < C_HEADDIM, dec_all[:, la:la + 1], dec_all[:, la + 1:la + 2])
            st_scr[di, m] = dec * st + _dot_tn(xw, bmat[g])

    @pl.when(pl.program_id(1) == pl.num_programs(1) - 1)
    def _():
        if n_prev:
            stf_ref[0:n_prev] = prev_ref[...]
            stf_ref[n_prev] = st_scr[...]
        else:
            stf_ref[...] = st_scr[...]


def _ssd_call(xbc, s_all, dtb, aneg, consts, st0, *, n_seq, seq_len, row0, prev=None):
    t = CHUNK
    nch = seq_len // t
    r0 = row0 // t
    npair = C_HEADS // 2
    bcw = 2 * C_GROUPS * C_STATE
    tri, expand, tile = consts
    full = lambda a: pl.BlockSpec(a.shape, lambda b, c: (0,) * a.ndim)
    in_specs, args = [], []
    for cidx in (lambda c: c, lambda c: nch - 1 - c):
        in_specs += [pl.BlockSpec((t, C_INNER), lambda b, c, cidx=cidx: (b * nch + cidx(c), 0)),
                     pl.BlockSpec((t, bcw), lambda b, c, cidx=cidx: (b * nch + cidx(c), C_INNER // bcw)),
                     pl.BlockSpec((t, LANE), lambda b, c, cidx=cidx: (r0 + b * nch + cidx(c), 0))]
        args += [xbc, xbc, s_all]
    in_specs += [full(dtb), full(aneg), full(tri), full(expand), full(tile)]
    args += [dtb, aneg, tri, expand, tile]
    st_shape = (2, npair, LANE, C_STATE)
    st_spec = pl.BlockSpec((None,) + st_shape, lambda b, c: (b, 0, 0, 0, 0))
    if st0 is not None:
        in_specs.append(st_spec)
        args.append(st0)
    n_prev = 0 if prev is None else prev.shape[1]
    if n_prev:
        in_specs.append(pl.BlockSpec((None, n_prev) + st_shape, lambda b, c: (b, 0, 0, 0, 0, 0)))
        args.append(prev)
        st_shape = (n_prev + 1,) + st_shape
        st_spec = pl.BlockSpec((None,) + st_shape, lambda b, c: (b, 0, 0, 0, 0, 0))
    return pl.pallas_call(
        functools.partial(_ssd_kernel, has_init=st0 is not None, n_prev=n_prev),
        grid=(n_seq, nch),
        in_specs=in_specs,
        out_specs=[pl.BlockSpec((t, C_INNER), lambda b, c: (b * nch + c, 0)),
                   pl.BlockSpec((t, C_INNER), lambda b, c: (b * nch + nch - 1 - c, 0)),
                   st_spec],
        out_shape=[jax.ShapeDtypeStruct((n_seq * seq_len, C_INNER), BF16),
                   jax.ShapeDtypeStruct((n_seq * seq_len, C_INNER), BF16),
                   jax.ShapeDtypeStruct((n_seq,) + st_shape, F32)],
        scratch_shapes=[pltpu.VMEM((2, npair, LANE, C_STATE), F32)],
        compiler_params=pltpu.CompilerParams(dimension_semantics=("parallel", "arbitrary")),
        name="ssd_scan",
    )(*args)


def _gla_consts(bwd):
    t = GLA_CHUNK
    nlev = int(math.log2(t))
    idx = np.arange(t)
    mats = np.zeros((nlev + 2, t, t), np.float32)
    masks = np.zeros((nlev + 1, t, t), np.float32)
    masks[0] = np.eye(t)
    for lvl in range(nlev):
        s = 1 << lvl
        blk = idx // (2 * s)
        upper = (idx % (2 * s)) >= s
        last_low = blk * 2 * s + s - 1
        rowtok = ~upper if bwd else upper
        for i in range(t):
            r = last_low[i]
            if not bwd:
                if upper[i]:
                    mats[lvl, i, r + 1:i + 1] = 1.0
                else:
                    mats[lvl, i, i + 1:r + 1] = 1.0
            else:
                if upper[i]:
                    mats[lvl, i, r + 1:i] = 1.0
                else:
                    mats[lvl, i, i:r + 1] = 1.0
        masks[lvl + 1] = ((blk[:, None] == blk[None, :]) & rowtok[:, None] & (~rowtok)[None, :])
    incl = (idx[None, :] >= idx[:, None]) if bwd else (idx[None, :] <= idx[:, None])
    mats[nlev] = incl
    mats[nlev + 1] = 1.0 - incl
    return mats.reshape((nlev + 2) * t, t), masks


def _gla_kernel(*refs, has_init, n_prev):
    refs = list(refs)
    xs = [(refs.pop(0), refs.pop(0), refs.pop(0)) for _ in range(2)]
    wgh_ref, wgl_ref, bg_ref, mall_ref, mask_ref = (refs.pop(0) for _ in range(5))
    st0_ref = refs.pop(0) if has_init else None
    prev_ref = refs.pop(0) if n_prev else None
    o_refs = (refs.pop(0), refs.pop(0))
    stf_ref, st_scr = refs

    @pl.when(pl.program_id(1) == 0)
    def _():
        st_scr[...] = st0_ref[...] if has_init else jnp.zeros(st_scr.shape, F32)

    t = GLA_CHUNK
    nlev = mask_ref.shape[1] - 1
    hk = B_HEADS * B_DK
    e_alls = []
    for di in range(2):
        s_hi, s_lo = _split2(xs[di][2][...])
        wgh = wgh_ref[di]
        logit = _dot(s_hi, wgh) + _dot(s_lo, wgh) + _dot(s_hi, wgl_ref[di]) + bg_ref[di]
        g_all = _log_sigmoid(logit) * (1.0 / B_GATE_NORM)
        e_alls.append(jnp.exp(_dot(mall_ref[di], g_all.astype(BF16))))
    for h in range(B_HEADS):
        for di in range(2):
            qk_ref, v_ref, _ = xs[di]
            o_ref = o_refs[di]
            e_all = e_alls[di]
            edge = 0 if di else t - 1
            ks = slice(h * B_DK, (h + 1) * B_DK)
            vs = slice(h * B_DV, (h + 1) * B_DV)
            q = qk_ref[:, ks].astype(F32) * (B_DK ** -0.5)
            k = qk_ref[:, hk + h * B_DK:hk + (h + 1) * B_DK]
            kf = k.astype(F32)
            v = v_ref[:, vs]
            amat = _dot_nt(q.astype(BF16), k) * mask_ref[di, 0]
            for lvl in range(nlev):
                e = e_all[lvl * t:(lvl + 1) * t, ks]
                amat = amat + _dot_nt((q * e).astype(BF16), (kf * e).astype(BF16)) * mask_ref[di, lvl + 1]
            e_in = e_all[nlev * t:(nlev + 1) * t, ks]
            e_out = e_all[(nlev + 1) * t:(nlev + 2) * t, ks]
            st = st_scr[di, h]
            o = _dot(amat.astype(BF16), v) + _dot_nt((q * e_in).astype(BF16), st.astype(BF16))
            o_ref[:, vs] = o.astype(BF16)
            st_scr[di, h] = st * e_in[edge:edge + 1, :] + _dot_tn(v, (kf * e_out).astype(BF16))

    @pl.when(pl.program_id(1) == pl.num_programs(1) - 1)
    def _():
        if n_prev:
            for di in range(2):
                for h in range(B_HEADS):
                    for lp in range(n_prev):
                        stf_ref[lp, di, h] = prev_ref[lp, di, h].T
                    stf_ref[n_prev, di, h] = st_scr[di, h].T
        else:
            stf_ref[...] = st_scr[...]


def _gla_call(u, s_all, wgh, wgl, bg, mall, masks, st0, *, n_seq, seq_len, row0, prev=None):
    t = GLA_CHUNK
    nch = seq_len // t
    r0 = row0 // t
    hk, hv = B_HEADS * B_DK, B_HEADS * B_DV
    full = lambda a: pl.BlockSpec(a.shape, lambda b, c: (0,) * a.ndim)
    in_specs, args = [], []
    for cidx in (lambda c: c, lambda c: nch - 1 - c):
        in_specs += [pl.BlockSpec((t, 2 * hk), lambda b, c, cidx=cidx: (r0 + b * nch + cidx(c), U_QK // (2 * hk))),
                     pl.BlockSpec((t, hv), lambda b, c, cidx=cidx: (r0 + b * nch + cidx(c), U_V // hv)),
                     pl.BlockSpec((t, LANE), lambda b, c, cidx=cidx: (r0 + b * nch + cidx(c), 0))]
        args += [u, u, s_all]
    in_specs += [full(wgh), full(wgl), full(bg), full(mall), full(masks)]
    args += [wgh, wgl, bg, mall, masks]
    st_shape = (2, B_HEADS, B_DV, B_DK)
    st_spec = pl.BlockSpec((None,) + st_shape, lambda b, c: (b, 0, 0, 0, 0))
    if st0 is not None:
        in_specs.append(st_spec)
        args.append(st0)
    n_prev = 0 if prev is None else prev.shape[1]
    if n_prev:
        in_specs.append(pl.BlockSpec((None, n_prev) + st_shape, lambda b, c: (b, 0, 0, 0, 0, 0)))
        args.append(prev)
        st_shape = (n_prev + 1, 2, B_HEADS, B_DK, B_DV)
        st_spec = pl.BlockSpec((None,) + st_shape, lambda b, c: (b, 0, 0, 0, 0, 0))
    return pl.pallas_call(
        functools.partial(_gla_kernel, has_init=st0 is not None, n_prev=n_prev),
        grid=(n_seq, nch),
        in_specs=in_specs,
        out_specs=[pl.BlockSpec((t, hv), lambda b, c: (b * nch + c, 0)),
                   pl.BlockSpec((t, hv), lambda b, c: (b * nch + nch - 1 - c, 0)),
                   st_spec],
        out_shape=[jax.ShapeDtypeStruct((n_seq * seq_len, hv), BF16),
                   jax.ShapeDtypeStruct((n_seq * seq_len, hv), BF16),
                   jax.ShapeDtypeStruct((n_seq,) + st_shape, F32)],
        scratch_shapes=[pltpu.VMEM((2, B_HEADS, B_DV, B_DK), F32)],
        compiler_params=pltpu.CompilerParams(dimension_semantics=("parallel", "arbitrary")),
        name="gla_scan",
    )(*args)


def _merge_kernel(x_ref, mod_ref, oa_ref, gf_ref, gb_ref, og_ref, yf_ref, yb_ref, xc_ref, z_ref, gt_ref,
                  awo_ref, onw_ref, bwo_ref, dexp_ref, cnw_ref, cwo_ref, wout_ref, o_ref):
    o_a = _dot(oa_ref[...], awo_ref[...])

    og = gf_ref[...].astype(F32) + gb_ref[...].astype(F32)
    onw = onw_ref[...]
    parts = []
    for h in range(B_HEADS):
        th = og[:, h * B_DV:(h + 1) * B_DV]
        parts.append(th * lax.rsqrt(jnp.mean(th * th, axis=-1, keepdims=True) + EPS) * onw)
    ob_in = jnp.concatenate(parts, axis=-1) * _silu(og_ref[...].astype(F32))
    o_b = _dot(ob_in.astype(BF16), bwo_ref[...])

    y = yf_ref[...].astype(F32) + yb_ref[...].astype(F32) + dexp_ref[...] * xc_ref[...].astype(F32)
    y = y * _silu(z_ref[...].astype(F32))
    y = y * lax.rsqrt(jnp.mean(y * y, axis=-1, keepdims=True) + EPS) * cnw_ref[...]
    o_c = _dot(y.astype(BF16), cwo_ref[...])

    d = D_MODEL
    merged = (jax.nn.sigmoid(gt_ref[:, 0:d].astype(F32)) * o_a
              + jax.nn.sigmoid(gt_ref[:, d:2 * d].astype(F32)) * o_b
              + jax.nn.sigmoid(gt_ref[:, 2 * d:3 * d].astype(F32)) * o_c)
    out = _dot(merged.astype(BF16), wout_ref[...])
    o_ref[...] = x_ref[...] + mod_ref[:, 2 * d:3 * d] * out


def _merge_call(x, mod3, u, o_attn, g_f, g_b, y_f, y_b, xbc, wts, *, n_rows, row0, layer):
    tm = 512
    r0 = row0 // tm
    d = D_MODEL
    loc = lambda w: pl.BlockSpec((tm, w), lambda i: (i, 0))
    full = lambda a: pl.BlockSpec(a.shape, lambda i: (0,) * a.ndim)
    in_specs = [pl.BlockSpec((tm, d), lambda i: (i + r0, 0)),
                pl.BlockSpec((None, 1, 6 * d), lambda i: (_mod_row(i + r0, tm, layer), 0, 0)),
                loc(A_HEADS * A_V), loc(d), loc(d),
                pl.BlockSpec((tm, d), lambda i: (i + r0, U_OG // d)),
                loc(d), loc(d),
                pl.BlockSpec((tm, d), lambda i: (i, 0)),
                pl.BlockSpec((tm, d), lambda i: (i + r0, U_Z // d)),
                pl.BlockSpec((tm, 3 * d), lambda i: (i + r0, U_GATES // (3 * d)))]
    w_args = [wts[n] for n in ('awo', 'onw', 'bwo', 'dexp', 'cnw', 'cwo', 'wout')]
    in_specs += [full(a) for a in w_args]
    return pl.pallas_call(
        _merge_kernel,
        grid=(n_rows // tm,),
        in_specs=in_specs,
        out_specs=pl.BlockSpec((tm, d), lambda i: (i + r0, 0)),
        out_shape=jax.ShapeDtypeStruct((N_TOK, d), F32),
        input_output_aliases={0: 0},
        compiler_params=pltpu.CompilerParams(dimension_semantics=("parallel",)),
        name="mixer_merge",
    )(x, mod3, o_attn, g_f, g_b, u, y_f, y_b, xbc, u, u, *w_args)


MOE_TILE = 512
MOE_TOKEN_TILE = 1024
MOE_MAX_TILES = 2 * N_TOK // MOE_TILE + N_EXPERTS
MOE_ROWS = MOE_MAX_TILES * MOE_TILE
R_E1, R_E2, R_W1, R_W2, R_P1, R_P2 = 0, 1, 2, 3, 4, 5


def _first_index(vals, target):
    idx = jnp.full(target.shape, len(vals) - 1, jnp.int32)
    for k in reversed(range(len(vals) - 1)):
        idx = jnp.where(vals[k] == target, k, idx)
    return idx


def _top2_of(vals):
    m1 = functools.reduce(jnp.maximum, vals)
    i1 = _first_index(vals, m1)
    rest = [jnp.where(i1 == k, -jnp.inf, v) for k, v in enumerate(vals)]
    m2 = functools.reduce(jnp.maximum, rest)
    return m1, i1, m2, _first_index(rest, m2)


def _pick_by(vals, idx):
    out = vals[-1]
    for k in reversed(range(len(vals) - 1)):
        out = jnp.where(idx == k, vals[k], out)
    return out


def _route_kernel(x_ref, mod_ref, nw_ref, rwh_ref, rwl_ref, rb_ref, tri_ref, h_ref, route_ref, wrec_ref, cnt_ref,
                  base_scr):
    d = D_MODEL
    ng = N_EXPERT_GROUPS
    per_group = N_EXPERTS // ng

    @pl.when(pl.program_id(0) == 0)
    def _():
        base_scr[...] = jnp.zeros(base_scr.shape, F32)

    x = x_ref[...]
    tm = x.shape[0]
    y = x * lax.rsqrt(jnp.mean(x * x, axis=-1, keepdims=True) + EPS) * nw_ref[...]
    h = y * (1.0 + mod_ref[:, 4 * d:5 * d]) + mod_ref[:, 3 * d:4 * d]
    h_ref[...] = h
    h_hi, h_lo = _split2(h)
    rwh = rwh_ref[...]
    logits = _dot_nt(rwh, h_hi) + _dot_nt(rwh, h_lo) + _dot_nt(rwl_ref[...], h_hi)
    scores = jax.nn.sigmoid(logits)
    sel = scores + rb_ref[...]
    sel_k = [sel[k * ng:(k + 1) * ng] for k in range(per_group)]
    sc_k = [scores[k * ng:(k + 1) * ng] for k in range(per_group)]
    m1, i1, m2, i2 = _top2_of(sel_k)
    gsum = m1 + m2
    rows = lambda a: [a[g:g + 1] for g in range(ng)]
    gs = rows(gsum)
    best_v, best_g = gs[0], jnp.zeros((1, tm), jnp.int32)
    for g in range(1, ng):
        upd = gs[g] > best_v
        best_g = jnp.where(upd, g, best_g)
        best_v = jnp.where(upd, gs[g], best_v)
    k1 = _pick_by(rows(i1), best_g)
    k2 = _pick_by(rows(i2), best_g)
    s1 = _pick_by(rows(_pick_by(sc_k, i1)), best_g)
    s2 = _pick_by(rows(_pick_by(sc_k, i2)), best_g)
    tot = s1 + s2
    in_best = lax.broadcasted_iota(jnp.int32, (ng, tm), 0) == best_g
    hit = jnp.concatenate(
        [jnp.where(jnp.logical_and(in_best, jnp.logical_or(k1 == k, k2 == k)), 1.0, 0.0) for k in range(per_group)],
        axis=0)
    rank = _dot(hit.astype(BF16), tri_ref[...]) + base_scr[...]
    rank_k = [rank[k * ng:(k + 1) * ng] for k in range(per_group)]
    p1 = _pick_by(rows(_pick_by(rank_k, k1)), best_g)
    p2 = _pick_by(rows(_pick_by(rank_k, k2)), best_g)
    base_scr[...] += jnp.sum(hit, axis=1, keepdims=True)
    e1 = (best_g * per_group + k1).astype(F32)
    e2 = (best_g * per_group + k2).astype(F32)
    rec = jnp.concatenate([e1, e2, s1 / tot, s2 / tot, p1, p2, jnp.zeros((LANE - 6, tm), F32)], axis=0)
    route_ref[...] = rec[0:8]
    wrec_ref[...] = rec.T
    cnt_ref[...] = jnp.broadcast_to(base_scr[...], cnt_ref.shape)


def _route_call(x, mod3, nw, rwh, rwl, rb, tri, layer):
    tm = tri.shape[0]
    d = D_MODEL
    full = lambda a: pl.BlockSpec(a.shape, lambda i: (0,) * a.ndim)
    return pl.pallas_call(
        _route_kernel,
        grid=(N_TOK // tm,),
        in_specs=[pl.BlockSpec((tm, d), lambda i: (i, 0)),
                  pl.BlockSpec((None, 1, 6 * d), lambda i: (_mod_row(i, tm, layer), 0, 0)),
                  full(nw), full(rwh), full(rwl), full(rb), full(tri)],
        out_specs=[pl.BlockSpec((tm, d), lambda i: (i, 0)),
                   pl.BlockSpec((8, tm), lambda i: (0, i)),
                   pl.BlockSpec((tm, LANE), lambda i: (i, 0)),
                   pl.BlockSpec((N_EXPERTS, LANE), lambda i: (0, 0))],
        out_shape=[jax.ShapeDtypeStruct((N_TOK, d), F32),
                   jax.ShapeDtypeStruct((8, N_TOK), F32),
                   jax.ShapeDtypeStruct((N_TOK, LANE), F32),
                   jax.ShapeDtypeStruct((N_EXPERTS, LANE), F32)],
        scratch_shapes=[pltpu.VMEM((N_EXPERTS, 1), F32)],
        compiler_params=pltpu.CompilerParams(dimension_semantics=("arbitrary",)),
        name="moe_route",
    )(x, mod3, nw, rwh, rwl, rb, tri)


def _row_copy(src, src_row, dst, dst_row, sem):
    return pltpu.make_async_copy(src.at[pl.ds(src_row, 1), :], dst.at[pl.ds(dst_row, 1), :], sem)


def _dispatch_kernel(dst_ref, ztile_ref, h_ref, xs_ref, zbuf, sem):
    tm = h_ref.shape[0]

    @pl.when(pl.program_id(0) == 0)
    def _():
        zbuf[...] = jnp.zeros(zbuf.shape, F32)
        fills = []
        for e in range(N_EXPERTS):
            row = pl.multiple_of(jnp.maximum(ztile_ref[0, e], 0), MOE_TILE)
            fills.append((ztile_ref[0, e] >= 0, pltpu.make_async_copy(zbuf, xs_ref.at[pl.ds(row, MOE_TILE), :], sem)))
        for has_tile, cp in fills:
            pl.when(has_tile)(cp.start)
        for has_tile, cp in fills:
            pl.when(has_tile)(cp.wait)

        def fill_unused(t, carry):
            cp = pltpu.make_async_copy(zbuf, xs_ref.at[pl.ds(pl.multiple_of(t * MOE_TILE, MOE_TILE), MOE_TILE), :], sem)
            cp.start()
            cp.wait()
            return carry

        lax.fori_loop(ztile_ref[0, N_EXPERTS], MOE_MAX_TILES, fill_unused, 0)

    def issue(j, carry):
        for i in range(2):
            r = 2 * j + i
            _row_copy(h_ref, r, xs_ref, dst_ref[0, r], sem).start(priority=i)
            _row_copy(h_ref, r, xs_ref, dst_ref[0, tm + r], sem).start(priority=1 - i)
        return carry

    lax.fori_loop(0, tm // 2, issue, 0, unroll=4)
    for _ in range(2):
        pltpu.make_async_copy(h_ref, xs_ref.at[pl.ds(0, tm), :], sem).wait()


def _dispatch_call(dst, ztile, h):
    tm = dst.shape[-1] // 2
    d = D_MODEL
    return pl.pallas_call(
        _dispatch_kernel,
        grid=(N_TOK // tm,),
        in_specs=[pl.BlockSpec((None, 1, 2 * tm), lambda i: (i, 0, 0), memory_space=pltpu.SMEM),
                  pl.BlockSpec((1, N_EXPERTS + 1), lambda i: (0, 0), memory_space=pltpu.SMEM),
                  pl.BlockSpec((tm, d), lambda i: (i, 0))],
        out_specs=pl.BlockSpec(memory_space=pl.ANY),
        out_shape=jax.ShapeDtypeStruct((MOE_ROWS, d), F32),
        scratch_shapes=[pltpu.VMEM((MOE_TILE, d), F32), pltpu.SemaphoreType.DMA(())],
        compiler_params=pltpu.CompilerParams(dimension_semantics=("arbitrary",)),
        name="moe_dispatch",
    )(dst, ztile, h)


def _expert_kernel(te_ref, nt_ref, xs_ref, w1_ref, w3_ref, w2_ref, y_ref, w1b, w3b, w2b):
    t = pl.program_id(0)
    changed = jnp.logical_or(t == 0, te_ref[t] != te_ref[jnp.maximum(t - 1, 0)])

    @pl.when(changed)
    def _():
        w1b[...] = w1_ref[...].astype(BF16)
        w3b[...] = w3_ref[...].astype(BF16)
        w2b[...] = w2_ref[...].astype(BF16)

    @pl.when(t < nt_ref[0])
    def _():
        xb = xs_ref[...].astype(BF16)
        hid = _silu(_dot(xb, w1b[...])) * _dot(xb, w3b[...])
        y_ref[...] = _dot(hid.astype(BF16), w2b[...])

    @pl.when(t >= nt_ref[0])
    def _():
        y_ref[...] = jnp.zeros(y_ref.shape, F32)


def _expert_call(tile_expert, n_tiles, xs, w1, w3, w2, layer):
    d = D_MODEL
    grid_spec = pltpu.PrefetchScalarGridSpec(
        num_scalar_prefetch=2,
        grid=(MOE_MAX_TILES,),
        in_specs=[pl.BlockSpec((MOE_TILE, d), lambda t, te, nt: (jnp.minimum(t, nt[0] - 1), 0)),
                  pl.BlockSpec((None, None, d, D_EXPERT), lambda t, te, nt: (layer, te[t], 0, 0)),
                  pl.BlockSpec((None, None, d, D_EXPERT), lambda t, te, nt: (layer, te[t], 0, 0)),
                  pl.BlockSpec((None, None, D_EXPERT, d), lambda t, te, nt: (layer, te[t], 0, 0))],
        out_specs=pl.BlockSpec((MOE_TILE, d), lambda t, te, nt: (t, 0)),
        scratch_shapes=[pltpu.VMEM((d, D_EXPERT), BF16), pltpu.VMEM((d, D_EXPERT), BF16),
                        pltpu.VMEM((D_EXPERT, d), BF16)])
    return pl.pallas_call(
        _expert_kernel,
        grid_spec=grid_spec,
        out_shape=jax.ShapeDtypeStruct((MOE_ROWS, d), F32),
        compiler_params=pltpu.CompilerParams(dimension_semantics=("arbitrary",)),
        name="moe_experts",
    )(tile_expert, n_tiles, xs, w1, w3, w2)


def _combine_kernel(dst_ref, x_ref, mod_ref, route_ref, y_ref, *rest, split):
    o_refs, (b1, b2, sem) = rest[:-3], rest[-3:]
    tm = x_ref.shape[0]
    d = D_MODEL

    def issue(j, carry):
        for i in range(2):
            r = 2 * j + i
            _row_copy(y_ref, dst_ref[0, r], b1, r, sem).start(priority=i)
            _row_copy(y_ref, dst_ref[0, tm + r], b2, r, sem).start(priority=1 - i)
        return carry

    lax.fori_loop(0, tm // 2, issue, 0, unroll=4)
    for buf in (b1, b2):
        pltpu.make_async_copy(y_ref.at[pl.ds(0, tm), :], buf, sem).wait()
    rec = route_ref[...]
    moe = rec[:, R_W1:R_W1 + 1] * b1[...] + rec[:, R_W2:R_W2 + 1] * b2[...]
    out = x_ref[...] + mod_ref[:, 5 * d:6 * d] * moe
    if split:
        is_ctx = pl.program_id(0) < N_CTX // tm

        @pl.when(is_ctx)
        def _():
            o_refs[0][...] = out

        @pl.when(jnp.logical_not(is_ctx))
        def _():
            o_refs[1][...] = out
    else:
        o_refs[0][...] = out


def _combine_call(dst, x, mod3, route, y, layer, split):
    tm = dst.shape[-1] // 2
    d = D_MODEL
    n_ctx = N_CTX // tm
    if split:
        out_specs = [pl.BlockSpec((tm, d), lambda i: (jnp.minimum(i, n_ctx - 1), 0)),
                     pl.BlockSpec((tm, d), lambda i: (jnp.maximum(i - n_ctx, 0), 0))]
        out_shape = [jax.ShapeDtypeStruct((N_CTX, d), F32), jax.ShapeDtypeStruct((N_DEC, d), F32)]
    else:
        out_specs = pl.BlockSpec((tm, d), lambda i: (i, 0))
        out_shape = jax.ShapeDtypeStruct((N_TOK, d), F32)
    return pl.pallas_call(
        functools.partial(_combine_kernel, split=split),
        grid=(N_TOK // tm,),
        in_specs=[pl.BlockSpec((None, 1, 2 * tm), lambda i: (i, 0, 0), memory_space=pltpu.SMEM),
                  pl.BlockSpec((tm, d), lambda i: (i, 0)),
                  pl.BlockSpec((None, 1, 6 * d), lambda i: (_mod_row(i, tm, layer), 0, 0)),
                  pl.BlockSpec((tm, LANE), lambda i: (i, 0)),
                  pl.BlockSpec(memory_space=pl.ANY)],
        out_specs=out_specs,
        out_shape=out_shape,
        scratch_shapes=[pltpu.VMEM((tm, d), F32), pltpu.VMEM((tm, d), F32), pltpu.SemaphoreType.DMA(())],
        compiler_params=pltpu.CompilerParams(dimension_semantics=("arbitrary",)),
        name="moe_combine",
    )(dst, x, mod3, route, y)


def _moe_routed(x, mod3, nw, rwh, rwl, rb, tri, w1, w3, w2, layer, split):
    h, route, wrec, cnt = _route_call(x, mod3, nw, rwh, rwl, rb, tri, layer)
    per_group = N_EXPERTS // N_EXPERT_GROUPS
    counts = cnt[:, 0].reshape(per_group, N_EXPERT_GROUPS).T.reshape(N_EXPERTS).astype(jnp.int32)
    ntile = (counts + MOE_TILE - 1) // MOE_TILE
    tend = jnp.cumsum(ntile)
    row0 = (tend - ntile) * MOE_TILE
    n_tiles = tend[-1:]
    tids = jnp.arange(MOE_MAX_TILES)
    tile_expert = jnp.minimum(jnp.sum(tend[None, :] <= jnp.minimum(tids, n_tiles[0] - 1)[:, None], axis=1),
                              N_EXPERTS - 1).astype(jnp.int32)
    ztile = jnp.concatenate([jnp.where(ntile > 0, row0 + (ntile - 1) * MOE_TILE, -1), n_tiles])
    ztile = ztile.astype(jnp.int32).reshape(1, N_EXPERTS + 1)
    eids = jnp.arange(N_EXPERTS, dtype=F32)
    row0f = row0.astype(F32)

    def dest(e, p):
        return (jnp.sum(jnp.where(e[:, None] == eids[None, :], row0f[None, :], 0.0), axis=1) + p).astype(jnp.int32)

    tm = MOE_TOKEN_TILE
    dst = jnp.concatenate([dest(route[R_E1], route[R_P1]).reshape(N_TOK // tm, 1, tm),
                           dest(route[R_E2], route[R_P2]).reshape(N_TOK // tm, 1, tm)], axis=-1)
    xs = _dispatch_call(dst, ztile, h)
    y = _expert_call(tile_expert, n_tiles.astype(jnp.int32), xs, w1, w3, w2, layer)
    return _combine_call(dst, x, mod3, wrec, y, layer, split)


def _pack_w_in(w):
    p = _col_slices(w)
    z = lambda n: jnp.zeros((D_MODEL, n), w.dtype)
    main = jnp.concatenate([
        p['a_cq'], p['a_ckv'], _head_slot(jnp.concatenate([z(A_NOPE), p['a_krope']], axis=1)), z(U_QK - U_KR - HEAD_PAD),
        p['b_q'], p['b_k'], p['b_v'], p['b_og'], p['c_z'], p['c_xbc'][:, :C_INNER], p['gates'],
        p['c_xbc'][:, C_INNER:]], axis=1)
    small = jnp.concatenate([p['b_gf'], p['b_gb'], p['c_dtf'], p['c_dtb'], p['a_krope'],
                             z(LANE - S_KR - A_ROPE)], axis=1)
    return main.astype(BF16), small.astype(BF16)


def _pair_slots(v):
    z = jnp.zeros_like(v)
    odd = (np.arange(A_HEADS) % 2 == 1)[None, :, None]
    return jnp.concatenate([jnp.where(odd, z, v), jnp.where(odd, v, z)], axis=-1).reshape(v.shape[0], A_HEADS * HEAD_PAD)


def _head_slot(t):
    t = t[..., _SLOT_SRC]
    return jnp.concatenate([t, jnp.zeros(t.shape[:-1] + (HEAD_PAD - A_QK,), t.dtype)], axis=-1)


def _qk_gain(g, scale):
    return (_head_slot(g) * scale).reshape(1, HEAD_PAD)


def _lane_slot(vals, off):
    return jnp.zeros((1, LANE), F32).at[0, off:off + vals.shape[0]].set(vals.astype(F32))


def _rope_tables():
    t = jnp.arange(DEC_SEQ)
    row = (t // GRID_W).astype(F32)
    col = (t % GRID_W).astype(F32)
    n_freq = A_ROPE // 4
    inv_freq = 1.0 / (ROPE_BASE ** (jnp.arange(n_freq, dtype=F32) / n_freq))
    ang = jnp.concatenate([row[:, None] * inv_freq, col[:, None] * inv_freq], axis=-1)
    cos, sin = jnp.cos(ang), jnp.sin(ang)
    ones = lambda n: jnp.ones((DEC_SEQ, n), F32)
    zeros = lambda n: jnp.zeros((DEC_SEQ, n), F32)
    n_hi = HEAD_PAD // 2 - A_ROPE // 2
    ctab = jnp.concatenate([cos, ones(_N_LOW), cos, ones(n_hi)], axis=-1)
    stab = jnp.concatenate([-sin, zeros(_N_LOW), sin, zeros(n_hi)], axis=-1)
    return ctab, stab


def kernel(x_prompt, x_sample, cache_ckv, cache_krope, state_gla, state_ssd, c, c_ctx, w_ada, b_ada, norm1, norm2, w_in, a_q_norm, a_wq, a_kv_norm, a_wkv, a_qk_qnorm, a_qk_knorm, a_wo, b_wg, b_bg, b_onorm, b_wo, c_conv_w, c_conv_b, c_dt_bias, c_A_log, c_D, c_norm, c_wo, w_out, router_w, router_bias, e_w1, e_w3, e_w2):
    d = D_MODEL
    x = jnp.concatenate([x_prompt.reshape(N_CTX, d), x_sample.reshape(N_DEC, d)], axis=0)

    cvecs = jnp.zeros((8, d), F32).at[0].set(c_ctx).at[1:1 + DEC_BATCH].set(c)
    mod3 = _mod_call(cvecs, w_ada, b_ada).reshape(DEPTH * 8, 1, 6 * d)

    rope_tabs = _rope_tables()
    gla_np = (_gla_consts(False), _gla_consts(True))
    gla_mall = jnp.asarray(np.stack([gla_np[0][0], gla_np[1][0]]), BF16)
    gla_masks = jnp.asarray(np.stack([gla_np[0][1], gla_np[1][1]]))
    ssd_consts = _ssd_consts()
    per_group = N_EXPERTS // N_EXPERT_GROUPS
    k_major = np.arange(N_EXPERTS).reshape(N_EXPERT_GROUPS, per_group).T.reshape(-1)
    rwh, rwl = _split2(router_w.T[k_major])
    rb = router_bias.astype(F32)[k_major].reshape(N_EXPERTS, 1)
    ridx = np.arange(1024)
    route_tri = jnp.asarray(ridx[:, None] < ridx[None, :], BF16)

    ckvs, kropes, glas, ssds = [], [], [], []
    for l in range(DEPTH):
        w_main, w_small = _pack_w_in(w_in[l])
        u, s_all = _norm_proj_call(x, mod3, norm1[l].reshape(1, d), w_main, w_small, l)
        kropes.append(s_all[:N_CTX, S_KR:S_KR + A_ROPE].reshape(BATCH, SEQ, A_ROPE))

        pw = A_HEADS * HEAD_PAD
        wq = _head_slot(a_wq[l].reshape(A_QLORA, A_HEADS, A_QK)).reshape(A_QLORA, pw)
        wkv = a_wkv[l].reshape(A_KVLORA, A_HEADS, A_NOPE + A_V)
        wkn = jnp.concatenate([wkv[..., :A_NOPE], jnp.zeros((A_KVLORA, A_HEADS, A_ROPE), wkv.dtype)], axis=-1)
        aw = dict(qnw=a_q_norm[l].reshape(1, A_QLORA), wq=wq.astype(BF16),
                  gq=_qk_gain(a_qk_qnorm[l], A_QK ** -0.5),
                  kvnw=a_kv_norm[l].reshape(1, A_KVLORA),
                  wkn=_head_slot(wkn).reshape(A_KVLORA, pw).astype(BF16),
                  wv=_pair_slots(wkv[..., A_NOPE:]).astype(BF16),
                  gk=_qk_gain(a_qk_knorm[l], 1.0))
        q_c, k_c, v_c, ckvn = _attn_prep_call(N_CTX, 0, (u,), aw, has_q=True, norm_kv=True, rope=False, emit_ckv=True)
        o_c = _attn_call(q_c, k_c, v_c, None, n_seq=BATCH, seq_len=SEQ, hps=A_HEADS, tq=SEQ)
        ckvs.append(ckvn.reshape(BATCH, SEQ, A_KVLORA))
        q_d, k_d, v_d = _attn_prep_call(N_DEC, N_CTX, (u,), aw, has_q=True, norm_kv=True, rope=True, emit_ckv=False,
                                        tables=rope_tabs)
        kr_x = _head_slot(jnp.concatenate([jnp.zeros((DEC_BATCH * PAST_LEN, A_NOPE), F32),
                                            cache_krope[:, l].reshape(DEC_BATCH * PAST_LEN, A_ROPE)], axis=1))
        k_x, v_x = _attn_prep_call(DEC_BATCH * PAST_LEN, 0, (cache_ckv[:, l].reshape(DEC_BATCH * PAST_LEN, A_KVLORA), kr_x),
                                   aw, has_q=False, norm_kv=False, rope=False, emit_ckv=False)
        o_d = _attn_call(q_d, k_d, v_d, (k_x, v_x), n_seq=DEC_BATCH, seq_len=DEC_SEQ, hps=8, tq=512)

        hk = B_HEADS * B_DK
        wg = (jnp.zeros((2, LANE, hk), F32).at[0, S_GF:S_GF + B_GATE_RANK].set(b_wg[l, 0])
              .at[1, S_GB:S_GB + B_GATE_RANK].set(b_wg[l, 1]))
        wgh, wgl = _split2(wg)
        bg = b_bg[l].reshape(2, 1, hk)
        last = l == DEPTH - 1
        g_fc, g_bc, st_gc = _gla_call(u, s_all, wgh, wgl, bg, gla_mall, gla_masks, None, n_seq=BATCH, seq_len=SEQ, row0=0,
                                      prev=jnp.stack(glas, axis=1) if last and glas else None)
        g_fd, g_bd, _ = _gla_call(u, s_all, wgh, wgl, bg, gla_mall, gla_masks, jnp.swapaxes(state_gla[:, l], -1, -2),
                                  n_seq=DEC_BATCH, seq_len=DEC_SEQ, row0=N_CTX)
        glas.append(st_gc)

        w8 = jnp.zeros((8, C_XBC), F32).at[:C_CONV].set(c_conv_w[l])
        cb = c_conv_b[l].reshape(1, C_XBC)
        xbc_c = _conv_call(u, w8, cb, n_seq=BATCH, seq_len=SEQ, row0=0)
        xbc_d = _conv_call(u, w8, cb, n_seq=DEC_BATCH, seq_len=DEC_SEQ, row0=N_CTX)
        dtb = jnp.stack([_lane_slot(c_dt_bias[l, 0], S_DTF), _lane_slot(c_dt_bias[l, 1], S_DTB)])
        a_neg = -jnp.exp(c_A_log[l].astype(F32))
        aneg = jnp.stack([_lane_slot(a_neg[0], S_DTF), _lane_slot(a_neg[1], S_DTB)])
        y_fc, y_bc, st_c = _ssd_call(xbc_c, s_all, dtb, aneg, ssd_consts, None, n_seq=BATCH, seq_len=SEQ, row0=0,
                                     prev=jnp.stack(ssds, axis=1) if last and ssds else None)
        st0 = state_ssd[:, l].reshape(DEC_BATCH, 2, C_HEADS // 2, LANE, C_STATE)
        y_fd, y_bd, _ = _ssd_call(xbc_d, s_all, dtb, aneg, ssd_consts, st0, n_seq=DEC_BATCH, seq_len=DEC_SEQ, row0=N_CTX)
        ssds.append(st_c)

        mw = dict(awo=a_wo[l].astype(BF16), onw=b_onorm[l].reshape(1, B_DV),
                  bwo=b_wo[l].astype(BF16), dexp=jnp.repeat(c_D[l], C_HEADDIM).reshape(1, C_INNER),
                  cnw=c_norm[l].reshape(1, C_INNER), cwo=c_wo[l].astype(BF16), wout=w_out[l].astype(BF16))
        x = _merge_call(x, mod3, u, o_c, g_fc, g_bc, y_fc, y_bc, xbc_c, mw, n_rows=N_CTX, row0=0, layer=l)
        x = _merge_call(x, mod3, u, o_d, g_fd, g_bd, y_fd, y_bd, xbc_d, mw, n_rows=N_DEC, row0=N_CTX, layer=l)

        x = _moe_routed(x, mod3, norm2[l].reshape(1, d), rwh, rwl, rb, route_tri, e_w1, e_w3, e_w2, l, split=last)

    y_prompt = x[0].reshape(BATCH, SEQ, d)
    y_sample = x[1].reshape(DEC_BATCH, DEC_SEQ, d)
    new_ckv = jnp.stack(ckvs, axis=1)
    new_krope = jnp.stack(kropes, axis=1)
    new_state_gla = glas[-1] if DEPTH > 1 else jnp.swapaxes(glas[-1], -1, -2)[:, None]
    new_state_ssd = (ssds[-1] if DEPTH > 1 else ssds[-1][:, None]).reshape(BATCH, DEPTH, 2, C_HEADS, C_HEADDIM, C_STATE)
    return (y_prompt, y_sample, new_ckv, new_krope, new_state_gla, new_state_ssd)
```
